```python
import math
import jax, jax.numpy as jnp
from jax import lax
import numpy as np

D_MODEL = 2048
BATCH = 4
SEQ = 2048
DEPTH = 1
DEC_BATCH = 128
DEC_SEQ = 8
PAST_LEN = 16384
PAGE_SIZE = 128

D_SSM = 1024
SSM_GROUP_CH = 16
SSM_GROUPS = D_SSM // SSM_GROUP_CH
SSM_STATE = 64
DT_MIN = 0.001
DT_MAX = 0.1
N_HEADS = 16
QK_NOPE = 128
QK_ROPE = 64
V_DIM = 128
KV_LORA = 512
D_ATTN = N_HEADS * V_DIM
ROPE_BASE = 10000.0
SOFTMAX_SCALE = (QK_NOPE + QK_ROPE) ** -0.5
Q_BLOCK = 128
LN_EPS = 1e-5
RMS_EPS = 1e-6
DEEPNORM_ALPHA = (2 * DEPTH) ** 0.25
DEEPNORM_BETA = (8 * DEPTH) ** -0.25

kernel_name = "hybrid_s5_mla_gated_decoder_step"

F32 = jnp.float32


def _layer_norm(x, g, b):
    xf = x.astype(F32)
    mu = jnp.mean(xf, -1, keepdims=True)
    var = jnp.mean(jnp.square(xf - mu), -1, keepdims=True)
    return ((xf - mu) * lax.rsqrt(var + LN_EPS) * g.astype(F32) + b.astype(F32)).astype(x.dtype)


def _rms_norm(x, g):
    xf = x.astype(F32)
    return (xf * lax.rsqrt(jnp.mean(jnp.square(xf), -1, keepdims=True) + RMS_EPS) * g.astype(F32)).astype(x.dtype)


def _rope_cos_sin(pos):
    freqs = ROPE_BASE ** (-jnp.arange(0, QK_ROPE, 2, dtype=F32) / QK_ROPE)
    ang = pos.astype(F32)[:, None] * freqs[None, :]
    return jnp.cos(ang), jnp.sin(ang)


def _rope(x, cos, sin):
    xf = x.astype(F32)
    x1, x2 = jnp.split(xf, 2, axis=-1)
    return jnp.concatenate([x1 * cos - x2 * sin, x2 * cos + x1 * sin], -1).astype(x.dtype)


def _front(x, c, pos, p):
    bsz, t, _ = x.shape
    mod = jax.nn.silu(c) @ p["w_ada"] + p["b_ada"]
    shift, scale, gate = jnp.split(mod, 3, axis=-1)
    h = x * (1.0 + scale[:, None, :]) + shift[:, None, :]
    proj = h @ p["w_in"]
    sizes = (D_SSM, D_SSM, N_HEADS * (QK_NOPE + QK_ROPE), KV_LORA, QK_ROPE, D_ATTN, D_MODEL, D_MODEL)
    idx = np.cumsum(sizes)[:-1].tolist()
    u, z_s, q, ckv_raw, kr_raw, z_a, g_s, g_a = jnp.split(proj, idx, axis=-1)
    q = q.reshape(bsz, t, N_HEADS, QK_NOPE + QK_ROPE)
    q_nope, q_rope = q[..., :QK_NOPE], q[..., QK_NOPE:]
    cos, sin = _rope_cos_sin(pos)
    q_rope = _rope(q_rope, cos[:, None, :], sin[:, None, :])
    k_rope = _rope(kr_raw, cos, sin)
    ckv = _rms_norm(ckv_raw, p["g_kv"])
    q_lat = jnp.einsum('bthd,chd->bthc', q_nope, p["w_uk"])
    return dict(u=u, z_s=z_s, z_a=z_a, g_s=g_s, g_a=g_a, gate=gate,
                q_lat=q_lat, q_rope=q_rope, ckv=ckv, k_rope=k_rope)


def _s5(u, h0, p):
    bsz, t, _ = u.shape
    uf = u.astype(F32).reshape(bsz, t, SSM_GROUPS, SSM_GROUP_CH)
    lam = lax.complex(p["lam_re"].astype(F32), p["lam_im"].astype(F32))
    dt = jnp.exp(p["log_dt"].astype(F32))[:, None]
    a_bar = jnp.exp(lam * dt)
    b_mat = lax.complex(p["b_re"].astype(F32), p["b_im"].astype(F32))
    c_mat = lax.complex(p["c_re"].astype(F32), p["c_im"].astype(F32))
    b_bar = ((a_bar - 1.0) / lam)[..., None] * b_mat
    bu = jnp.einsum('btgn,gpn->btgp', uf.astype(jnp.complex64), b_bar)
    bu = bu.at[:, 0].add(a_bar[None] * h0)
    a_seq = jnp.broadcast_to(a_bar, bu.shape)

    def combine(e1, e2):
        a1, b1 = e1
        a2, b2 = e2
        return a1 * a2, a2 * b1 + b2

    _, hs = lax.associative_scan(combine, (a_seq, bu), axis=1)
    y = jnp.einsum('gnp,btgp->btgn', c_mat, hs).real + p["d_skip"].astype(F32).reshape(SSM_GROUPS, SSM_GROUP_CH) * uf
    return y.reshape(bsz, t, D_SSM), hs[:, -1]


def _attend(q_lat, q_rope, k_lat, k_rope, q_pos, k_pos):
    s = (jnp.einsum('qhc,kc->hqk', q_lat, k_lat) + jnp.einsum('qhr,kr->hqk', q_rope, k_rope)).astype(F32) * SOFTMAX_SCALE
    s = jnp.where(k_pos[None, None, :] <= q_pos[None, :, None], s, -jnp.inf)
    pr = jax.nn.softmax(s, axis=-1)
    return jnp.einsum('hqk,kc->qhc', pr.astype(k_lat.dtype), k_lat)


def _prompt_attention(q_lat, q_rope, ckv, k_rope, pos):
    bsz, t = q_lat.shape[:2]
    nqb = t // Q_BLOCK
    ql = q_lat.reshape(bsz, nqb, Q_BLOCK, N_HEADS, KV_LORA).swapaxes(0, 1)
    qr = q_rope.reshape(bsz, nqb, Q_BLOCK, N_HEADS, QK_ROPE).swapaxes(0, 1)
    qp = pos.reshape(nqb, Q_BLOCK)
    batched = jax.vmap(_attend, in_axes=(0, 0, 0, 0, None, None))

    def block(args):
        ql_b, qr_b, qp_b = args
        return batched(ql_b, qr_b, ckv, k_rope, qp_b, pos)

    out = lax.map(block, (ql, qr, qp))
    return out.swapaxes(0, 1).reshape(bsz, t, N_HEADS, KV_LORA)


def _sample_attention(q_lat, q_rope, ckv_new, kr_new, pos, cache_ckv, cache_krope, page_table):
    t = q_lat.shape[1]
    past = page_table.shape[1] * cache_ckv.shape[1]
    k_pos = jnp.arange(past + t)

    def one_seq(args):
        ql, qr, pages, cn, kn = args
        k_lat = jnp.concatenate([cache_ckv[pages].reshape(-1, KV_LORA), cn], 0)
        k_r = jnp.concatenate([cache_krope[pages].reshape(-1, QK_ROPE), kn], 0)
        return _attend(ql, qr, k_lat, k_r, pos, k_pos)

    return lax.map(one_seq, (q_lat, q_rope, page_table, ckv_new, kr_new))


def _back(x, fr, y_ssm, o_lat, p):
    bsz, t, _ = x.shape
    ys = jax.nn.gelu(y_ssm.astype(x.dtype), approximate=False)
    ga, gb = jnp.split(ys @ p["w_glu"], 2, axis=-1)
    ys = ga * jax.nn.sigmoid(gb) * jax.nn.silu(fr["z_s"])
    p_s = ys @ p["w_bs"]
    v = jnp.einsum('bthc,chd->bthd', o_lat, p["w_uv"]).reshape(bsz, t, D_ATTN) * jax.nn.silu(fr["z_a"])
    p_a = v @ p["w_o"]
    merged = jax.nn.sigmoid(fr["g_s"]) * p_s + jax.nn.sigmoid(fr["g_a"]) * p_a
    out = merged @ p["w_out"]
    return _layer_norm(DEEPNORM_ALPHA * x + fr["gate"][:, None, :] * out, p["ln_g"], p["ln_b"])


def setup_inputs(seed: int = 0) -> dict:
    key = jax.random.key(seed)
    ks = iter(jax.random.split(key, 40))
    n_pages = PAST_LEN // PAGE_SIZE
    n_used = DEC_BATCH * n_pages
    n_pool = n_used + max(1, n_used // 4)
    nrm = lambda shape, s: jax.random.normal(next(ks), shape, F32) * s
    w_in_cols = 2 * D_SSM + N_HEADS * (QK_NOPE + QK_ROPE) + KV_LORA + QK_ROPE + D_ATTN + 2 * D_MODEL
    page_table = jax.random.permutation(next(ks), n_pool)[:n_used].reshape(DEC_BATCH, n_pages).astype(jnp.int32)
    lam_im0 = jnp.pi * jnp.arange(SSM_STATE, dtype=F32)
    return {
        "x_prompt": nrm((BATCH, SEQ, D_MODEL), 1.0),
        "x_sample": nrm((DEC_BATCH, DEC_SEQ, D_MODEL), 1.0),
        "c_prompt": nrm((BATCH, D_MODEL), 1.0),
        "c_sample": nrm((DEC_BATCH, D_MODEL), 1.0),
        "cache_ckv": nrm((DEPTH, n_pool, PAGE_SIZE, KV_LORA), 1.0),
        "cache_krope": nrm((DEPTH, n_pool, PAGE_SIZE, QK_ROPE), 1.0),
        "state_ssm_re": nrm((DEPTH, DEC_BATCH, SSM_GROUPS, SSM_STATE), 0.3),
        "state_ssm_im": nrm((DEPTH, DEC_BATCH, SSM_GROUPS, SSM_STATE), 0.3),
        "page_table": page_table,
        "w_ada": nrm((DEPTH, D_MODEL, 3 * D_MODEL), 0.5 * D_MODEL ** -0.5),
        "b_ada": nrm((DEPTH, 3 * D_MODEL), 0.02),
        "w_in": nrm((DEPTH, D_MODEL, w_in_cols), D_MODEL ** -0.5),
        "g_kv": 1.0 + nrm((DEPTH, KV_LORA), 0.02),
        "w_uk": nrm((DEPTH, KV_LORA, N_HEADS, QK_NOPE), KV_LORA ** -0.5),
        "w_uv": nrm((DEPTH, KV_LORA, N_HEADS, V_DIM), KV_LORA ** -0.5),
        "w_o": nrm((DEPTH, D_ATTN, D_MODEL), DEEPNORM_BETA * D_ATTN ** -0.5),
        "lam_re": -0.5 + nrm((DEPTH, SSM_GROUPS, SSM_STATE), 0.01),
        "lam_im": lam_im0 + nrm((DEPTH, SSM_GROUPS, SSM_STATE), 0.01),
        "log_dt": jax.random.uniform(next(ks), (DEPTH, SSM_GROUPS), F32, math.log(DT_MIN), math.log(DT_MAX)),
        "b_re": nrm((DEPTH, SSM_GROUPS, SSM_STATE, SSM_GROUP_CH), (2 * SSM_GROUP_CH) ** -0.5),
        "b_im": nrm((DEPTH, SSM_GROUPS, SSM_STATE, SSM_GROUP_CH), (2 * SSM_GROUP_CH) ** -0.5),
        "c_re": nrm((DEPTH, SSM_GROUPS, SSM_GROUP_CH, SSM_STATE), SSM_STATE ** -0.5),
        "c_im": nrm((DEPTH, SSM_GROUPS, SSM_GROUP_CH, SSM_STATE), SSM_STATE ** -0.5),
        "d_skip": nrm((DEPTH, D_SSM), 1.0),
        "w_glu": nrm((DEPTH, D_SSM, 2 * D_SSM), D_SSM ** -0.5),
        "w_bs": nrm((DEPTH, D_SSM, D_MODEL), DEEPNORM_BETA * D_SSM ** -0.5),
        "w_out": nrm((DEPTH, D_MODEL, D_MODEL), DEEPNORM_BETA * D_MODEL ** -0.5),
        "ln_g": 1.0 + nrm((DEPTH, D_MODEL), 0.02),
        "ln_b": nrm((DEPTH, D_MODEL), 0.02),
    }


def reference(x_prompt, x_sample, c_prompt, c_sample, cache_ckv, cache_krope, state_ssm_re, state_ssm_im,
              page_table, w_ada, b_ada, w_in, g_kv, w_uk, w_uv, w_o, lam_re, lam_im, log_dt, b_re, b_im,
              c_re, c_im, d_skip, w_glu, w_bs, w_out, ln_g, ln_b):
    t_p = x_prompt.shape[1]
    t_s = x_sample.shape[1]
    past = page_table.shape[1] * cache_ckv.shape[2]
    pos_p = jnp.arange(t_p)
    pos_s = past + jnp.arange(t_s)
    xp, xs = x_prompt, x_sample
    ckv_p, kr_p, sre_p, sim_p = [], [], [], []
    ckv_s, kr_s, sre_s, sim_s = [], [], [], []
    for l in range(DEPTH):
        p = dict(w_ada=w_ada[l], b_ada=b_ada[l], w_in=w_in[l], g_kv=g_kv[l], w_uk=w_uk[l], w_uv=w_uv[l],
                 w_o=w_o[l], lam_re=lam_re[l], lam_im=lam_im[l], log_dt=log_dt[l], b_re=b_re[l], b_im=b_im[l],
                 c_re=c_re[l], c_im=c_im[l], d_skip=d_skip[l], w_glu=w_glu[l], w_bs=w_bs[l], w_out=w_out[l],
                 ln_g=ln_g[l], ln_b=ln_b[l])
        fp = _front(xp, c_prompt, pos_p, p)
        h0_p = jnp.zeros((xp.shape[0], SSM_GROUPS, SSM_STATE), jnp.complex64)
        y_sp, hfin_p = _s5(fp["u"], h0_p, p)
        o_p = _prompt_attention(fp["q_lat"], fp["q_rope"], fp["ckv"], fp["k_rope"], pos_p)
        ckv_p.append(fp["ckv"]); kr_p.append(fp["k_rope"])
        sre_p.append(hfin_p.real.astype(state_ssm_re.dtype)); sim_p.append(hfin_p.imag.astype(state_ssm_im.dtype))
        xp = _back(xp, fp, y_sp, o_p, p)
        fs = _front(xs, c_sample, pos_s, p)
        h0_s = lax.complex(state_ssm_re[l].astype(F32), state_ssm_im[l].astype(F32))
        y_ss, hfin_s = _s5(fs["u"], h0_s, p)
        o_s = _sample_attention(fs["q_lat"], fs["q_rope"], fs["ckv"], fs["k_rope"], pos_s,
                                cache_ckv[l], cache_krope[l], page_table)
        ckv_s.append(fs["ckv"]); kr_s.append(fs["k_rope"])
        sre_s.append(hfin_s.real.astype(state_ssm_re.dtype)); sim_s.append(hfin_s.imag.astype(state_ssm_im.dtype))
        xs = _back(xs, fs, y_ss, o_s, p)
    return (xp, xs,
            jnp.stack(ckv_p), jnp.stack(kr_p), jnp.stack(sre_p), jnp.stack(sim_p),
            jnp.stack(ckv_s), jnp.stack(kr_s), jnp.stack(sre_s), jnp.stack(sim_s))
```

```python
import functools
import math

import jax
import jax.numpy as jnp
from jax import lax
from jax.experimental import pallas as pl
from jax.experimental.pallas import tpu as pltpu

F32 = jnp.float32
BF16 = jnp.bfloat16

LN_EPS = 1e-5
RMS_EPS = 1e-6
ROPE_BASE = 10000.0
NEG_BIG = -1e30
VMEM_LIMIT = 48 * 1024 * 1024
PAGES_PER_STEP = 16
SSM_GROUP_BLOCK = 8
LANES = 128
S5_TIME_CHUNK = 256
FLASH_BLOCK = 1024


def _cparams(sem):
    return pltpu.CompilerParams(dimension_semantics=sem, vmem_limit_bytes=VMEM_LIMIT)


def _sigmoid(x):
    return 1.0 / (1.0 + jnp.exp(-x))


def _silu(x):
    return x * _sigmoid(x)


def _dot(a, b):
    return jnp.dot(a, b, preferred_element_type=F32)


def _dot_nt(a, b):
    return lax.dot_general(a, b, (((1,), (1,)), ((), ())), preferred_element_type=F32)


def _mod_kernel(c_ref, w_ref, b_ref, o_ref):
    a = _silu(c_ref[...]).astype(BF16)
    o_ref[...] = _dot(a, w_ref[...].astype(BF16)) + b_ref[...]


def _mod_call(c_all, w_ada, b_ada, tn):
    r, d = c_all.shape
    n = w_ada.shape[1]
    return pl.pallas_call(
        _mod_kernel,
        grid=(n // tn,),
        in_specs=[pl.BlockSpec((r, d), lambda j: (0, 0)),
                  pl.BlockSpec((d, tn), lambda j: (0, j)),
                  pl.BlockSpec((1, tn), lambda j: (0, j))],
        out_specs=pl.BlockSpec((r, tn), lambda j: (0, j)),
        out_shape=jax.ShapeDtypeStruct((r, n), F32),
        compiler_params=_cparams(("arbitrary",)),
        name="mod",
    )(c_all, w_ada, b_ada.reshape(1, n))


def _hmod_kernel(x_ref, sh_ref, sc_ref, *rest):
    o_ref = rest[-1]
    h = x_ref[...] * (1.0 + sc_ref[...]) + sh_ref[...]
    o_ref[...] = h.reshape(o_ref.shape).astype(o_ref.dtype)


def _hmod_prompt(x, mod3, mod_row0, m_total, tm):
    b, t, d = x.shape
    nt = t // tm
    return pl.pallas_call(
        _hmod_kernel,
        grid=(b, nt),
        in_specs=[pl.BlockSpec((1, tm, d), lambda i, j: (i, j, 0)),
                  pl.BlockSpec((1, 1, d), lambda i, j: (mod_row0 + i, 0, 0)),
                  pl.BlockSpec((1, 1, d), lambda i, j: (mod_row0 + i, 0, 1))],
        out_specs=pl.BlockSpec((tm, d), lambda i, j: (i * nt + j, 0)),
        out_shape=jax.ShapeDtypeStruct((m_total, d), BF16),
        compiler_params=_cparams(("arbitrary", "arbitrary")),
        name="hmod_prompt",
    )(x, mod3, mod3)


def _hmod_sample(x, mod3, h_all, row_block0, sb):
    ns, ts, d = x.shape
    return pl.pallas_call(
        _hmod_kernel,
        grid=(ns // sb,),
        in_specs=[pl.BlockSpec((sb, ts, d), lambda i: (i, 0, 0)),
                  pl.BlockSpec((sb, 1, d), lambda i: (i, 0, 0)),
                  pl.BlockSpec((sb, 1, d), lambda i: (i, 0, 1)),
                  pl.BlockSpec(memory_space=pl.ANY)],
        out_specs=pl.BlockSpec((sb * ts, d), lambda i: (row_block0 + i, 0)),
        out_shape=jax.ShapeDtypeStruct(h_all.shape, h_all.dtype),
        input_output_aliases={3: 0},
        compiler_params=_cparams(("arbitrary",)),
        name="hmod_sample",
    )(x, mod3, mod3, h_all)


def _proj_kernel(h_ref, w_ref, o_ref, *, act):
    acc = _dot(h_ref[...], w_ref[...])
    if act == "silu":
        acc = _silu(acc)
    elif act == "sigmoid":
        acc = _sigmoid(acc)
    o_ref[...] = acc.astype(o_ref.dtype)


def _proj_call(h, w, act, out_dtype, tm, tn, name):
    m, k = h.shape
    n = w.shape[1]
    return pl.pallas_call(
        functools.partial(_proj_kernel, act=act),
        grid=(m // tm, n // tn),
        in_specs=[pl.BlockSpec((tm, k), lambda i, j: (i, 0)),
                  pl.BlockSpec((k, tn), lambda i, j: (0, j))],
        out_specs=pl.BlockSpec((tm, tn), lambda i, j: (i, j)),
        out_shape=jax.ShapeDtypeStruct((m, n), out_dtype),
        compiler_params=_cparams(("arbitrary", "arbitrary")),
        name=name,
    )(h, w)


def _qproj_kernel(h_ref, w_ref, cs_ref, o_ref, *, nope, rope, scale):
    acc = _dot(h_ref[...], w_ref[0])
    t = acc[:, nope:] * cs_ref[...]
    r = t + pltpu.roll(t, rope, 1)
    lane = lax.broadcasted_iota(jnp.int32, r.shape, 1)
    r = jnp.where(lane < rope, r, 0.0)
    o_ref[0, :, :nope] = (acc[:, :nope] * scale).astype(o_ref.dtype)
    o_ref[0, :, nope:] = (r * scale).astype(o_ref.dtype)


def _qproj_call(h, w_q3, cs, tm, nope, rope, scale):
    m, k = h.shape
    nh, _, wd = w_q3.shape
    return pl.pallas_call(
        functools.partial(_qproj_kernel, nope=nope, rope=rope, scale=scale),
        grid=(m // tm, nh),
        in_specs=[pl.BlockSpec((tm, k), lambda i, j: (i, 0)),
                  pl.BlockSpec((1, k, wd), lambda i, j: (j, 0, 0)),
                  pl.BlockSpec((tm, 2 * rope), lambda i, j: (i, 0))],
        out_specs=pl.BlockSpec((1, tm, wd), lambda i, j: (j, i, 0)),
        out_shape=jax.ShapeDtypeStruct((nh, m, wd), BF16),
        compiler_params=_cparams(("arbitrary", "arbitrary")),
        name="qproj",
    )(h, w_q3, cs)


def _ckv_kernel(h_ref, w_ref, g_ref, cs_ref, ckv_ref, kr_ref, ckvb_ref, krb_ref, *, lora, rope):
    acc = _dot(h_ref[...], w_ref[...])
    c = acc[:, :lora]
    ms = jnp.mean(c * c, axis=-1, keepdims=True)
    ckv = c * lax.rsqrt(ms + RMS_EPS) * g_ref[...]
    ckv_ref[...] = ckv
    ckvb_ref[...] = ckv.astype(BF16)
    t = acc[:, lora:] * cs_ref[...]
    r = t + pltpu.roll(t, rope, 1)
    kr_ref[...] = r[:, :rope]
    lane = lax.broadcasted_iota(jnp.int32, r.shape, 1)
    krb_ref[...] = jnp.where(lane < rope, r, 0.0).astype(BF16)


def _ckv_call(h, w_ck, g_kv, cs, tm, lora, rope):
    m, k = h.shape
    wd = w_ck.shape[1]
    return pl.pallas_call(
        functools.partial(_ckv_kernel, lora=lora, rope=rope),
        grid=(m // tm,),
        in_specs=[pl.BlockSpec((tm, k), lambda i: (i, 0)),
                  pl.BlockSpec((k, wd), lambda i: (0, 0)),
                  pl.BlockSpec((1, lora), lambda i: (0, 0)),
                  pl.BlockSpec((tm, 2 * rope), lambda i: (i, 0))],
        out_specs=[pl.BlockSpec((tm, lora), lambda i: (i, 0)),
                   pl.BlockSpec((tm, rope), lambda i: (i, 0)),
                   pl.BlockSpec((tm, lora), lambda i: (i, 0)),
                   pl.BlockSpec((tm, 2 * rope), lambda i: (i, 0))],
        out_shape=[jax.ShapeDtypeStruct((m, lora), F32),
                   jax.ShapeDtypeStruct((m, rope), F32),
                   jax.ShapeDtypeStruct((m, lora), BF16),
                   jax.ShapeDtypeStruct((m, 2 * rope), BF16)],
        compiler_params=_cparams(("arbitrary",)),
        name="ckvproj",
    )(h, w_ck, g_kv.reshape(1, lora), cs)


def _s5_prep_kernel(lre_ref, lim_ref, ldt_ref, bre_ref, bim_ref, are_ref, aim_ref, bbre_ref, bbim_ref):
    lre = lre_ref[...]
    lim = lim_ref[...]
    dt = jnp.exp(ldt_ref[...])
    mag = jnp.exp(lre * dt)
    a_re = mag * jnp.cos(lim * dt)
    a_im = mag * jnp.sin(lim * dt)
    are_ref[...] = a_re
    aim_ref[...] = a_im
    den = lre * lre + lim * lim
    nr = a_re - 1.0
    c_re = (nr * lre + a_im * lim) / den
    c_im = (a_im * lre - nr * lim) / den
    bre = bre_ref[...]
    bim = bim_ref[...]
    bbre_ref[...] = c_re * bre - c_im * bim
    bbim_ref[...] = c_re * bim + c_im * bre


def _s5_prep_call(lam_re, lam_im, log_dt, b_re_t, b_im_t):
    g, p = lam_re.shape
    n = b_re_t.shape[1]
    a_re, a_im, bb_re, bb_im = pl.pallas_call(
        _s5_prep_kernel,
        out_shape=[jax.ShapeDtypeStruct((g, 1, p), F32), jax.ShapeDtypeStruct((g, 1, p), F32),
                   jax.ShapeDtypeStruct((g, n, p), F32), jax.ShapeDtypeStruct((g, n, p), F32)],
        name="s5prep",
    )(lam_re.reshape(g, 1, p), lam_im.reshape(g, 1, p), log_dt.reshape(g, 1, 1), b_re_t, b_im_t)
    return a_re.reshape(g, p), a_im.reshape(g, p), bb_re, bb_im


def _s5_kernel(*refs, n_u, tc, nseq, sw, has_h0):
    u_refs = refs[:n_u]
    i = n_u
    bhi_ref, blo_ref, c_ref, a_ref, d_ref = refs[i:i + 5]
    i += 5
    if has_h0:
        h0re_ref, h0im_ref = refs[i:i + 2]
        i += 2
    gy_ref, sre_ref, sim_ref = refs[i:i + 3]
    i += 3
    bu_sc, st_sc = refs[i:i + 2]

    step = pl.program_id(1)

    @pl.when(step == 0)
    def _init():
        if has_h0:
            st_sc[:, :sw] = h0re_ref[...]
            st_sc[:, sw:] = h0im_ref[...]
        else:
            st_sc[...] = jnp.zeros_like(st_sc)

    if n_u == 1:
        u = u_refs[0][...]
    else:
        u = jnp.concatenate([r[...] for r in u_refs], axis=0)
    u_hi = u.astype(BF16)
    u_lo = (u - u_hi.astype(F32)).astype(BF16)
    bhi = bhi_ref[0]
    bu = _dot(u_hi, bhi) + _dot(u_lo, bhi) + _dot(u_hi, blo_ref[0])
    nj = sw // LANES
    for j in range(2 * nj):
        bu_sc[j] = bu[:, j * LANES:(j + 1) * LANES]

    for sg in range(0, nseq, 8):
        n8 = min(8, nseq - sg)
        base = sg * tc
        a_b = [jnp.broadcast_to(a_ref[0, r:r + 1, j * LANES:(j + 1) * LANES], (n8, LANES))
               for r in range(2) for j in range(nj)]

        def body(t, carry, base=base, n8=n8, a_b=a_b):
            rows = pl.ds(base + t, n8, stride=tc)
            new = [None] * (2 * nj)
            for j in range(nj):
                h_re, h_im = carry[j], carry[nj + j]
                new[j] = a_b[j] * h_re - a_b[nj + j] * h_im + bu_sc[j, rows, :]
                new[nj + j] = a_b[j] * h_im + a_b[nj + j] * h_re + bu_sc[nj + j, rows, :]
            for j in range(2 * nj):
                bu_sc[j, rows, :] = new[j]
            return tuple(new)

        init = tuple(st_sc[sg:sg + n8, j * LANES:(j + 1) * LANES] for j in range(2 * nj))
        fin = lax.fori_loop(0, tc, body, init, unroll=8)
        for j in range(2 * nj):
            st_sc[sg:sg + n8, j * LANES:(j + 1) * LANES] = fin[j]

    hs = jnp.concatenate([bu_sc[j] for j in range(2 * nj)], axis=1)
    y = _dot(hs.astype(BF16), c_ref[0]) + d_ref[...] * u
    gy = 0.5 * y * (1.0 + lax.erf(y * (1.0 / math.sqrt(2.0))))
    gy_ref[...] = gy.reshape(gy_ref.shape).astype(gy_ref.dtype)

    @pl.when(step == pl.num_programs(1) - 1)
    def _fin():
        sre_ref[...] = st_sc[:, :sw]
        sim_ref[...] = st_sc[:, sw:]


def _s5_weight_specs(gk, sw, idx):
    return [pl.BlockSpec((1, gk, 2 * sw), lambda *a: (idx(*a), 0, 0)),
            pl.BlockSpec((1, gk, 2 * sw), lambda *a: (idx(*a), 0, 0)),
            pl.BlockSpec((1, 2 * sw, gk), lambda *a: (idx(*a), 0, 0)),
            pl.BlockSpec((1, 2, sw), lambda *a: (idx(*a), 0, 0)),
            pl.BlockSpec((1, gk), lambda *a: (0, idx(*a)))]


def _s5_prompt_call(u_all, wts, nb, t, tc):
    bhi, blo, cmat, amat, dsk = wts
    ngb, gk, sw2 = bhi.shape
    sw = sw2 // 2
    ntc = t // tc
    gb_of = lambda g, s: g
    u_specs = [pl.BlockSpec((tc, gk), functools.partial(lambda g, s, b: (b * ntc + s, g), b=b)) for b in range(nb)]
    kern = functools.partial(_s5_kernel, n_u=nb, tc=tc, nseq=nb, sw=sw, has_h0=False)
    return pl.pallas_call(
        kern,
        grid=(ngb, ntc),
        in_specs=u_specs + _s5_weight_specs(gk, sw, gb_of),
        out_specs=[pl.BlockSpec((nb, tc, gk), lambda g, s: (0, s, g)),
                   pl.BlockSpec((nb, sw), lambda g, s: (0, g)),
                   pl.BlockSpec((nb, sw), lambda g, s: (0, g))],
        out_shape=[jax.ShapeDtypeStruct((nb, t, ngb * gk), BF16),
                   jax.ShapeDtypeStruct((nb, ngb * sw), F32),
                   jax.ShapeDtypeStruct((nb, ngb * sw), F32)],
        scratch_shapes=[pltpu.VMEM((2 * sw // LANES, nb * tc, LANES), F32), pltpu.VMEM((nb, 2 * sw), F32)],
        compiler_params=_cparams(("arbitrary", "arbitrary")),
        name="s5_prompt",
    )(*([u_all] * nb), bhi, blo, cmat, amat, dsk)


def _s5_sample_call(u_all, wts, h0_re, h0_im, ns, ts, row_block0, sb):
    bhi, blo, cmat, amat, dsk = wts
    ngb, gk, sw2 = bhi.shape
    sw = sw2 // 2
    gb_of = lambda g, s, z: g
    kern = functools.partial(_s5_kernel, n_u=1, tc=ts, nseq=sb, sw=sw, has_h0=True)
    return pl.pallas_call(
        kern,
        grid=(ngb, 1, ns // sb),
        in_specs=[pl.BlockSpec((sb * ts, gk), lambda g, s, z: (row_block0 + z, g))]
        + _s5_weight_specs(gk, sw, gb_of)
        + [pl.BlockSpec((sb, sw), lambda g, s, z: (z, g)), pl.BlockSpec((sb, sw), lambda g, s, z: (z, g))],
        out_specs=[pl.BlockSpec((sb * ts, gk), lambda g, s, z: (z, g)),
                   pl.BlockSpec((sb, sw), lambda g, s, z: (z, g)),
                   pl.BlockSpec((sb, sw), lambda g, s, z: (z, g))],
        out_shape=[jax.ShapeDtypeStruct((ns * ts, ngb * gk), BF16),
                   jax.ShapeDtypeStruct((ns, ngb * sw), F32),
                   jax.ShapeDtypeStruct((ns, ngb * sw), F32)],
        scratch_shapes=[pltpu.VMEM((2 * sw // LANES, sb * ts, LANES), F32), pltpu.VMEM((sb, 2 * sw), F32)],
        compiler_params=_cparams(("arbitrary", "arbitrary", "arbitrary")),
        name="s5_sample",
    )(u_all, bhi, blo, cmat, amat, dsk, h0_re, h0_im)


def _kvup_kernel(c_ref, kr_ref, w_ref, k_ref, v_ref, *, nope):
    acc = _dot(c_ref[...], w_ref[0])
    k_ref[0, :, :nope] = acc[:, :nope].astype(k_ref.dtype)
    k_ref[0, :, nope:] = kr_ref[...]
    v_ref[0] = acc[:, nope:].astype(v_ref.dtype)


def _kvup_call(ckv_b, kr_b, w_ukv, mp, tm, nope):
    lora = ckv_b.shape[1]
    nh, _, wd = w_ukv.shape
    vd = wd - nope
    rw = kr_b.shape[1]
    return pl.pallas_call(
        functools.partial(_kvup_kernel, nope=nope),
        grid=(mp // tm, nh),
        in_specs=[pl.BlockSpec((tm, lora), lambda i, j: (i, 0)),
                  pl.BlockSpec((tm, rw), lambda i, j: (i, 0)),
                  pl.BlockSpec((1, lora, wd), lambda i, j: (j, 0, 0))],
        out_specs=[pl.BlockSpec((1, tm, nope + rw), lambda i, j: (j, i, 0)),
                   pl.BlockSpec((1, tm, vd), lambda i, j: (j, i, 0))],
        out_shape=[jax.ShapeDtypeStruct((nh, mp, nope + rw), BF16),
                   jax.ShapeDtypeStruct((nh, mp, vd), BF16)],
        compiler_params=_cparams(("arbitrary", "arbitrary")),
        name="kvup",
    )(ckv_b, kr_b, w_ukv)


def _flash_kernel(q_ref, k_ref, v_ref, o_ref, m_sc, l_sc, acc_sc, *, tq):
    qi = pl.program_id(2)
    ki = pl.program_id(3)

    @pl.when(ki == 0)
    def _init():
        m_sc[...] = jnp.full_like(m_sc, NEG_BIG)
        l_sc[...] = jnp.zeros_like(l_sc)
        acc_sc[...] = jnp.zeros_like(acc_sc)

    @pl.when(ki <= qi)
    def _step():
        s = _dot_nt(q_ref[0], k_ref[0])
        row = qi * tq + lax.broadcasted_iota(jnp.int32, s.shape, 0)
        col = ki * tq + lax.broadcasted_iota(jnp.int32, s.shape, 1)
        s = jnp.where(col <= row, s, NEG_BIG)
        m_prev = m_sc[...]
        m_new = jnp.maximum(m_prev, jnp.max(s, axis=-1, keepdims=True))
        alpha = jnp.exp(m_prev - m_new)
        p = jnp.exp(s - m_new)
        l_sc[...] = alpha * l_sc[...] + jnp.sum(p, axis=-1, keepdims=True)
        acc_sc[...] = alpha * acc_sc[...] + _dot(p.astype(BF16), v_ref[0])
        m_sc[...] = m_new

    @pl.when(ki == pl.num_programs(3) - 1)
    def _fin():
        o_ref[...] = (acc_sc[...] / l_sc[...]).astype(o_ref.dtype)


def _flash_call(q_hm, k_hm, v_hm, nb, t, m_total, tq):
    nh, _, qd = q_hm.shape
    vd = v_hm.shape[2]
    nq = t // tq
    return pl.pallas_call(
        functools.partial(_flash_kernel, tq=tq),
        grid=(nb, nh, nq, nq),
        in_specs=[pl.BlockSpec((1, tq, qd), lambda b, h, i, j: (h, b * nq + i, 0)),
                  pl.BlockSpec((1, tq, qd), lambda b, h, i, j: (h, b * nq + jnp.minimum(i, j), 0)),
                  pl.BlockSpec((1, tq, vd), lambda b, h, i, j: (h, b * nq + jnp.minimum(i, j), 0))],
        out_specs=pl.BlockSpec((tq, vd), lambda b, h, i, j: (b * nq + i, h)),
        out_shape=jax.ShapeDtypeStruct((m_total, nh * vd), BF16),
        scratch_shapes=[pltpu.VMEM((tq, 1), F32), pltpu.VMEM((tq, 1), F32), pltpu.VMEM((tq, vd), F32)],
        compiler_params=_cparams(("arbitrary", "arbitrary", "arbitrary", "arbitrary")),
        name="flash",
    )(q_hm, k_hm, v_hm)


def _qlat_kernel(q_ref, w_ref, o_ref):
    o_ref[0] = _dot(q_ref[0], w_ref[0])


def _qlat_call(q_hm, w_ukt, ms, row_block0):
    nh, nope, lora = w_ukt.shape
    return pl.pallas_call(
        _qlat_kernel,
        grid=(nh,),
        in_specs=[pl.BlockSpec((1, ms, nope), lambda h: (h, row_block0, 0)),
                  pl.BlockSpec((1, nope, lora), lambda h: (h, 0, 0))],
        out_specs=pl.BlockSpec((1, ms, lora), lambda h: (h, 0, 0)),
        out_shape=jax.ShapeDtypeStruct((nh, ms, lora), F32),
        compiler_params=_cparams(("arbitrary",)),
        name="qlat",
    )(q_hm, w_ukt)


def _decode_kernel(pt_ref, ql_ref, qr_ref, cn_ref, kn_ref, *rest, npg, ts, rope):
    ck_refs = rest[:npg]
    kr_refs = rest[npg:2 * npg]
    o_ref = rest[2 * npg]
    qlb_sc, qrb_sc, kb_sc, rb_sc, m_sc, l_sc, acc_sc = rest[2 * npg + 1:]
    j = pl.program_id(1)
    rows, lora = qlb_sc.shape
    pg = ck_refs[0].shape[0]

    @pl.when(j == 0)
    def _init():
        qlb_sc[...] = ql_ref[...].reshape(rows, lora).astype(BF16)
        qrb_sc[...] = qr_ref[...].reshape(rows, rope).astype(BF16)
        m_sc[...] = jnp.full_like(m_sc, NEG_BIG)
        l_sc[...] = jnp.zeros_like(l_sc)
        acc_sc[...] = jnp.zeros_like(acc_sc)

    def update(s, kb):
        m_prev = m_sc[...]
        m_new = jnp.maximum(m_prev, jnp.max(s, axis=-1, keepdims=True))
        alpha = jnp.exp(m_prev - m_new)
        p = jnp.exp(s - m_new)
        l_sc[...] = alpha * l_sc[...] + jnp.sum(p, axis=-1, keepdims=True)
        acc_sc[...] = alpha * acc_sc[...] + _dot(p.astype(BF16), kb)
        m_sc[...] = m_new

    for i in range(npg):
        kb_sc[i * pg:(i + 1) * pg, :] = ck_refs[i][...].astype(BF16)
        rb_sc[i * pg:(i + 1) * pg, :] = kr_refs[i][...].astype(BF16)
    kb = kb_sc[...]
    update(_dot_nt(qlb_sc[...], kb) + _dot_nt(qrb_sc[...], rb_sc[...]), kb)

    @pl.when(j == pl.num_programs(1) - 1)
    def _fin():
        kn = jnp.concatenate([cn_ref[...], jnp.zeros((pg - ts, lora), F32)], axis=0).astype(BF16)
        rn = jnp.concatenate([kn_ref[...], jnp.zeros((pg - ts, rope), F32)], axis=0).astype(BF16)
        s = _dot_nt(qlb_sc[...], kn) + _dot_nt(qrb_sc[...], rn)
        tq = lax.broadcasted_iota(jnp.int32, s.shape, 0) % ts
        tk = lax.broadcasted_iota(jnp.int32, s.shape, 1)
        update(jnp.where(tk <= tq, s, NEG_BIG), kn)
        o = acc_sc[...] / l_sc[...]
        o_ref[...] = o.reshape(o_ref.shape)


def _decode_call(page_table, q_lat, q_rope, ckv_all, kr_all, cache_ckv, cache_krope, ns, ts, row_block0, npg):
    nh, ms, lora = q_lat.shape
    rope = q_rope.shape[2]
    n_pages = page_table.shape[1]
    pg = cache_ckv.shape[1]
    nsteps = n_pages // npg
    rows = nh * ts

    def page_map(s, j, pt, i):
        return (pt[s * n_pages + j * npg + i], 0, 0)

    ck_specs = [pl.BlockSpec((None, pg, lora), functools.partial(page_map, i=i)) for i in range(npg)]
    kr_specs = [pl.BlockSpec((None, pg, rope), functools.partial(page_map, i=i)) for i in range(npg)]
    grid_spec = pltpu.PrefetchScalarGridSpec(
        num_scalar_prefetch=1,
        grid=(ns, nsteps),
        in_specs=[pl.BlockSpec((nh, ts, lora), lambda s, j, pt: (0, s, 0)),
                  pl.BlockSpec((nh, ts, rope), lambda s, j, pt: (0, s, 0)),
                  pl.BlockSpec((ts, lora), lambda s, j, pt: (row_block0 + s, 0)),
                  pl.BlockSpec((ts, rope), lambda s, j, pt: (row_block0 + s, 0))] + ck_specs + kr_specs,
        out_specs=pl.BlockSpec((nh, ts, lora), lambda s, j, pt: (0, s, 0)),
        scratch_shapes=[pltpu.VMEM((rows, lora), BF16), pltpu.VMEM((rows, rope), BF16),
                        pltpu.VMEM((npg * pg, lora), BF16), pltpu.VMEM((npg * pg, rope), BF16),
                        pltpu.VMEM((rows, 1), F32), pltpu.VMEM((rows, 1), F32), pltpu.VMEM((rows, lora), F32)],
    )
    return pl.pallas_call(
        functools.partial(_decode_kernel, npg=npg, ts=ts, rope=rope),
        grid_spec=grid_spec,
        out_shape=jax.ShapeDtypeStruct((nh, ms, lora), F32),
        compiler_params=_cparams(("arbitrary", "arbitrary")),
        name="decode",
    )(page_table.reshape(-1), q_lat, q_rope, ckv_all, kr_all, *([cache_ckv] * npg), *([cache_krope] * npg))


def _vup_kernel(o_ref, w_ref, att_any, out_ref):
    del att_any
    out_ref[...] = _dot(o_ref[0].astype(BF16), w_ref[0]).astype(out_ref.dtype)


def _vup_call(o_lat, w_uv3, att_all, row_block0):
    nh, ms, lora = o_lat.shape
    vd = w_uv3.shape[2]
    return pl.pallas_call(
        _vup_kernel,
        grid=(nh,),
        in_specs=[pl.BlockSpec((1, ms, lora), lambda h: (h, 0, 0)),
                  pl.BlockSpec((1, lora, vd), lambda h: (h, 0, 0)),
                  pl.BlockSpec(memory_space=pl.ANY)],
        out_specs=pl.BlockSpec((ms, vd), lambda h: (row_block0, h)),
        out_shape=jax.ShapeDtypeStruct(att_all.shape, att_all.dtype),
        input_output_aliases={2: 0},
        compiler_params=_cparams(("arbitrary",)),
        name="vup",
    )(o_lat, w_uv3, att_all)


def _glu_kernel(g_ref, wa_ref, wb_ref, z_ref, o_ref):
    g = g_ref[...]
    ga = _dot(g, wa_ref[...])
    gb = _dot(g, wb_ref[...])
    o_ref[...] = (ga * _sigmoid(gb) * z_ref[...].astype(F32)).astype(o_ref.dtype)


def _glu_call(gy, w_glu, zazs, tm, tn):
    m, k = gy.shape
    n = w_glu.shape[1] // 2
    nn = n // tn
    zs_blk0 = (zazs.shape[1] - n) // tn
    return pl.pallas_call(
        _glu_kernel,
        grid=(m // tm, nn),
        in_specs=[pl.BlockSpec((tm, k), lambda i, j: (i, 0)),
                  pl.BlockSpec((k, tn), lambda i, j: (0, j)),
                  pl.BlockSpec((k, tn), lambda i, j: (0, nn + j)),
                  pl.BlockSpec((tm, tn), lambda i, j: (i, zs_blk0 + j))],
        out_specs=pl.BlockSpec((tm, tn), lambda i, j: (i, j)),
        out_shape=jax.ShapeDtypeStruct((m, n), BF16),
        compiler_params=_cparams(("arbitrary", "arbitrary")),
        name="glu",
    )(gy, w_glu, w_glu, zazs)


def _merge_kernel(ys_ref, att_ref, za_ref, wbs_ref, wo_ref, gs_ref, ga_ref, o_ref, v_sc):
    @pl.when(pl.program_id(1) == 0)
    def _gate():
        v_sc[...] = (att_ref[...].astype(F32) * za_ref[...].astype(F32)).astype(BF16)

    p_s = _dot(ys_ref[...], wbs_ref[...])
    p_a = _dot(v_sc[...], wo_ref[...])
    o_ref[...] = (gs_ref[...].astype(F32) * p_s + ga_ref[...].astype(F32) * p_a).astype(o_ref.dtype)


def _merge_call(ys2, att, zazs, w_bs, w_o, gates, tm, tn):
    m, ks = ys2.shape
    ka = att.shape[1]
    n = w_bs.shape[1]
    nn = n // tn
    return pl.pallas_call(
        _merge_kernel,
        grid=(m // tm, nn),
        in_specs=[pl.BlockSpec((tm, ks), lambda i, j: (i, 0)),
                  pl.BlockSpec((tm, ka), lambda i, j: (i, 0)),
                  pl.BlockSpec((tm, ka), lambda i, j: (i, 0)),
                  pl.BlockSpec((ks, tn), lambda i, j: (0, j)),
                  pl.BlockSpec((ka, tn), lambda i, j: (0, j)),
                  pl.BlockSpec((tm, tn), lambda i, j: (i, j)),
                  pl.BlockSpec((tm, tn), lambda i, j: (i, nn + j))],
        out_specs=pl.BlockSpec((tm, tn), lambda i, j: (i, j)),
        out_shape=jax.ShapeDtypeStruct((m, n), BF16),
        scratch_shapes=[pltpu.VMEM((tm, ka), BF16)],
        compiler_params=_cparams(("arbitrary", "arbitrary")),
        name="merge",
    )(ys2, att, zazs, w_bs, w_o, gates, gates)


def _final_kernel(mg_ref, w_ref, x_ref, gate_ref, g_ref, b_ref, o_ref, *, alpha):
    out = _dot(mg_ref[...], w_ref[...])
    x = x_ref[...]
    y = alpha * x + gate_ref[...] * out.reshape(x.shape)
    mu = jnp.mean(y, axis=-1, keepdims=True)
    yc = y - mu
    var = jnp.mean(yc * yc, axis=-1, keepdims=True)
    o_ref[...] = yc * lax.rsqrt(var + LN_EPS) * g_ref[...] + b_ref[...]


def _final_call(merged, w_out, x, mod3, ln_g, ln_b, alpha, x_blk, gate_blk, gate_map, merged_map, grid):
    d = w_out.shape[0]
    rows = x_blk[0] * x_blk[1]
    nd = len(grid)
    x_map = (lambda i, j: (i, j, 0)) if nd == 2 else (lambda i: (i, 0, 0))
    const2 = (lambda i, j: (0, 0)) if nd == 2 else (lambda i: (0, 0))
    const3 = (lambda i, j: (0, 0, 0)) if nd == 2 else (lambda i: (0, 0, 0))
    return pl.pallas_call(
        functools.partial(_final_kernel, alpha=alpha),
        grid=grid,
        in_specs=[pl.BlockSpec((rows, d), merged_map),
                  pl.BlockSpec((d, d), const2),
                  pl.BlockSpec(x_blk, x_map),
                  pl.BlockSpec(gate_blk, gate_map),
                  pl.BlockSpec((1, 1, d), const3),
                  pl.BlockSpec((1, 1, d), const3)],
        out_specs=pl.BlockSpec(x_blk, x_map),
        out_shape=jax.ShapeDtypeStruct(x.shape, F32),
        compiler_params=_cparams(("arbitrary",) * nd),
        name="final",
    )(merged, w_out, x, mod3, ln_g.reshape(1, 1, d), ln_b.reshape(1, 1, d))


def _rope_tables(pos, rope):
    freqs = ROPE_BASE ** (-jnp.arange(0, rope, 2, dtype=F32) / rope)
    ang = pos.astype(F32)[:, None] * freqs[None, :]
    cos, sin = jnp.cos(ang), jnp.sin(ang)
    return jnp.concatenate([cos, cos, sin, sin], axis=-1)


def _rot_cols(w, rope):
    half = rope // 2
    return jnp.concatenate([-w[..., half:], w[..., :half]], axis=-1)


def _block_diag(x, gblk):
    g, r, c = x.shape
    x4 = x.reshape(g // gblk, gblk, r, c)
    eye = jnp.eye(gblk, dtype=x.dtype)
    return jnp.einsum("bgrc,gh->bgrhc", x4, eye).reshape(g // gblk, gblk * r, gblk * c)


def kernel(x_prompt, x_sample, c_prompt, c_sample, cache_ckv, cache_krope, state_ssm_re, state_ssm_im, page_table, w_ada, b_ada, w_in, g_kv, w_uk, w_uv, w_o, lam_re, lam_im, log_dt, b_re, b_im, c_re, c_im, d_skip, w_glu, w_bs, w_out, ln_g, ln_b):
    nb, t, d = x_prompt.shape
    ns, ts, _ = x_sample.shape
    depth = w_in.shape[0]
    assert depth == 1, "single-layer step"
    lora, nh, nope = w_uk.shape[1:]
    vd = w_uv.shape[3]
    rope = cache_krope.shape[3]
    d_ssm = d_skip.shape[1]
    ngrp, nstate, gch = b_re.shape[1:]
    d_attn = nh * vd
    pg = cache_ckv.shape[2]
    past = page_table.shape[1] * pg
    mp, ms = nb * t, ns * ts
    m = mp + ms
    scale = float((nope + rope) ** -0.5)
    alpha = float((2 * depth) ** 0.25)

    tm = math.gcd(math.gcd(mp, ms), 1024)
    tmh = min(tm, 512)
    tn = 512
    assert rope * 2 == 128 and nope == 128 and vd == 128

    wi = w_in[0]
    o_u, o_zs, o_q = 0, d_ssm, 2 * d_ssm
    o_ckv = o_q + nh * (nope + rope)
    o_kr = o_ckv + lora
    o_za = o_kr + rope
    o_gs = o_za + d_attn
    o_ga = o_gs + d
    w_u = wi[:, o_u:o_zs].astype(BF16)
    w_z = jnp.concatenate([wi[:, o_za:o_gs], wi[:, o_zs:o_q]], axis=1).astype(BF16)
    w_g = wi[:, o_gs:].astype(BF16)
    wq = wi[:, o_q:o_ckv].reshape(d, nh, nope + rope)
    wq_r = wq[:, :, nope:]
    w_q3 = jnp.concatenate([wq, _rot_cols(wq_r, rope)], axis=-1).transpose(1, 0, 2).astype(BF16)
    wkr = wi[:, o_kr:o_za]
    w_ck = jnp.concatenate([wi[:, o_ckv:o_kr], wkr, _rot_cols(wkr, rope)], axis=1).astype(BF16)
    w_ukv = jnp.concatenate([w_uk[0], w_uv[0]], axis=-1).transpose(1, 0, 2).astype(BF16)
    w_ukt = w_uk[0].transpose(1, 2, 0).astype(BF16)
    w_uv3 = w_uv[0].transpose(1, 0, 2).astype(BF16)
    w_glu_b = w_glu[0].astype(BF16)
    w_bs_b = w_bs[0].astype(BF16)
    w_o_b = w_o[0].astype(BF16)
    w_out_b = w_out[0].astype(BF16)

    pos = jnp.concatenate([jnp.tile(jnp.arange(t), nb), jnp.tile(past + jnp.arange(ts), ns)])
    cs = _rope_tables(pos, rope)

    c_all = jnp.concatenate([c_sample, c_prompt], axis=0)
    pad = (-c_all.shape[0]) % 8
    c_all = jnp.concatenate([c_all, jnp.zeros((pad, d), F32)], axis=0)
    mod = _mod_call(c_all, w_ada[0], b_ada[0], tn)
    mod3 = mod.reshape(mod.shape[0], 1, 3 * d)

    sb = max(1, min(ns, tmh // ts))
    h_all = _hmod_prompt(x_prompt, mod3, ns, m, tmh)
    h_all = _hmod_sample(x_sample, mod3, h_all, mp // (sb * ts), sb)

    u_all = _proj_call(h_all, w_u, None, F32, tm, tn, "proj_u")
    zazs = _proj_call(h_all, w_z, "silu", BF16, tm, tn, "proj_z")
    gates = _proj_call(h_all, w_g, "sigmoid", BF16, tm, tn, "proj_g")
    q_hm = _qproj_call(h_all, w_q3, cs, tm, nope, rope, scale)
    ckv_all, kr_all, ckv_b, kr_b = _ckv_call(h_all, w_ck, g_kv[0], cs, tmh, lora, rope)

    gblk = SSM_GROUP_BLOCK
    a_re, a_im, bb_re, bb_im = _s5_prep_call(lam_re[0], lam_im[0], log_dt[0],
                                             b_re[0].transpose(0, 2, 1), b_im[0].transpose(0, 2, 1))
    bcat = jnp.concatenate([_block_diag(bb_re, gblk), _block_diag(bb_im, gblk)], axis=-1)
    b_hi = bcat.astype(BF16)
    b_lo = (bcat - b_hi.astype(F32)).astype(BF16)
    ccat = jnp.concatenate([_block_diag(c_re[0].transpose(0, 2, 1), gblk),
                            _block_diag(-c_im[0].transpose(0, 2, 1), gblk)], axis=1).astype(BF16)
    ngb = ngrp // gblk
    sw = gblk * nstate
    amat = jnp.stack([a_re.reshape(ngb, sw), a_im.reshape(ngb, sw)], axis=1)
    s5w = (b_hi, b_lo, ccat, amat, d_skip)
    tc = min(t, S5_TIME_CHUNK)
    gy_p, sre_p, sim_p = _s5_prompt_call(u_all, s5w, nb, t, tc)
    ssb = min(ns, 32)
    gy_s, sre_s, sim_s = _s5_sample_call(u_all, s5w, state_ssm_re[0].reshape(ns, ngrp * nstate),
                                         state_ssm_im[0].reshape(ns, ngrp * nstate), ns, ts, mp // (ssb * ts), ssb)
    gy = jnp.concatenate([gy_p.reshape(mp, d_ssm), gy_s], axis=0)

    k_hm, v_hm = _kvup_call(ckv_b, kr_b, w_ukv, mp, tm, nope)
    tq = min(t, FLASH_BLOCK)
    att = _flash_call(q_hm, k_hm, v_hm, nb, t, m, tq)
    q_lat = _qlat_call(q_hm, w_ukt, ms, mp // ms)
    q_rope_s = q_hm[:, mp:, nope:nope + rope].astype(F32)
    o_lat = _decode_call(page_table, q_lat, q_rope_s, ckv_all, kr_all, cache_ckv.reshape(cache_ckv.shape[1:]), cache_krope.reshape(cache_krope.shape[1:]),
                         ns, ts, mp // ts, PAGES_PER_STEP)
    att = _vup_call(o_lat, w_uv3, att, mp // ms)

    ys2 = _glu_call(gy, w_glu_b, zazs, tm, tn)
    merged = _merge_call(ys2, att, zazs, w_bs_b, w_o_b, gates, tmh, tn)
    ntp = t // tmh
    y_p = _final_call(merged, w_out_b, x_prompt, mod3, ln_g[0], ln_b[0], alpha,
                      (1, tmh, d), (1, 1, d), lambda i, j: (ns + i, 0, 2), lambda i, j: (i * ntp + j, 0), (nb, ntp))
    y_s = _final_call(merged, w_out_b, x_sample, mod3, ln_g[0], ln_b[0], alpha,
                      (sb, ts, d), (sb, 1, d), lambda i: (i, 0, 2), lambda i: (mp // (sb * ts) + i, 0), (ns // sb,))

    st = lambda a, n: a.reshape(1, n, ngrp, nstate)
    return (y_p, y_s,
            ckv_all[:mp].reshape(1, nb, t, lora), kr_all[:mp].reshape(1, nb, t, rope), st(sre_p, nb), st(sim_p, nb),
            ckv_all[mp:].reshape(1, ns, ts, lora), kr_all[mp:].reshape(1, ns, ts, rope), st(sre_s, ns), st(sim_s, ns))
```

```python
import functools
import math

import jax
import jax.numpy as jnp
from jax import lax
from jax.experimental import pallas as pl
from jax.experimental.pallas import tpu as pltpu

F32 = jnp.float32
BF16 = jnp.bfloat16

LN_EPS = 1e-5
RMS_EPS = 1e-6
ROPE_BASE = 10000.0
NEG_BIG = -1e30
VMEM_LIMIT = 48 * 1024 * 1024
PAGES_PER_STEP = 16
DECODE_SEQS_PER_STEP = 2
LOG2E = 1.4426950408889634
SSM_GROUP_BLOCK = 8
LANES = 128
S5_TIME_CHUNK = 256
FLASH_BLOCK = 1024


def _cparams(sem):
    return pltpu.CompilerParams(dimension_semantics=sem, vmem_limit_bytes=VMEM_LIMIT)


def _sigmoid(x):
    return 1.0 / (1.0 + jnp.exp(-x))


def _silu(x):
    return x * _sigmoid(x)


def _dot(a, b):
    return jnp.dot(a, b, preferred_element_type=F32)


def _dot_nt(a, b):
    return lax.dot_general(a, b, (((1,), (1,)), ((), ())), preferred_element_type=F32)


def _mod_kernel(c_ref, w_ref, b_ref, o_ref):
    a = _silu(c_ref[...]).astype(BF16)
    o_ref[...] = _dot(a, w_ref[...].astype(BF16)) + b_ref[...]


def _mod_call(c_all, w_ada, b_ada, tn):
    r, d = c_all.shape
    n = w_ada.shape[1]
    return pl.pallas_call(
        _mod_kernel,
        grid=(n // tn,),
        in_specs=[pl.BlockSpec((r, d), lambda j: (0, 0)),
                  pl.BlockSpec((d, tn), lambda j: (0, j)),
                  pl.BlockSpec((1, tn), lambda j: (0, j))],
        out_specs=pl.BlockSpec((r, tn), lambda j: (0, j)),
        out_shape=jax.ShapeDtypeStruct((r, n), F32),
        compiler_params=_cparams(("arbitrary",)),
        name="mod",
    )(c_all, w_ada, b_ada.reshape(1, n))


def _hmod_kernel(x_ref, sh_ref, sc_ref, *rest):
    o_ref = rest[-1]
    h = x_ref[...] * (1.0 + sc_ref[...]) + sh_ref[...]
    o_ref[...] = h.reshape(o_ref.shape).astype(o_ref.dtype)


def _hmod_prompt(x, mod3, mod_row0, m_total, tm):
    b, t, d = x.shape
    nt = t // tm
    return pl.pallas_call(
        _hmod_kernel,
        grid=(b, nt),
        in_specs=[pl.BlockSpec((1, tm, d), lambda i, j: (i, j, 0)),
                  pl.BlockSpec((1, 1, d), lambda i, j: (mod_row0 + i, 0, 0)),
                  pl.BlockSpec((1, 1, d), lambda i, j: (mod_row0 + i, 0, 1))],
        out_specs=pl.BlockSpec((tm, d), lambda i, j: (i * nt + j, 0)),
        out_shape=jax.ShapeDtypeStruct((m_total, d), BF16),
        compiler_params=_cparams(("arbitrary", "arbitrary")),
        name="hmod_prompt",
    )(x, mod3, mod3)


def _hmod_sample(x, mod3, h_all, row_block0, sb):
    ns, ts, d = x.shape
    return pl.pallas_call(
        _hmod_kernel,
        grid=(ns // sb,),
        in_specs=[pl.BlockSpec((sb, ts, d), lambda i: (i, 0, 0)),
                  pl.BlockSpec((sb, 1, d), lambda i: (i, 0, 0)),
                  pl.BlockSpec((sb, 1, d), lambda i: (i, 0, 1)),
                  pl.BlockSpec(memory_space=pl.ANY)],
        out_specs=pl.BlockSpec((sb * ts, d), lambda i: (row_block0 + i, 0)),
        out_shape=jax.ShapeDtypeStruct(h_all.shape, h_all.dtype),
        input_output_aliases={3: 0},
        compiler_params=_cparams(("arbitrary",)),
        name="hmod_sample",
    )(x, mod3, mod3, h_all)


def _proj_kernel(h_ref, w_ref, o_ref, *, act):
    acc = _dot(h_ref[...], w_ref[...])
    if act == "silu":
        acc = _silu(acc)
    elif act == "sigmoid":
        acc = _sigmoid(acc)
    o_ref[...] = acc.astype(o_ref.dtype)


def _proj_call(h, w, act, out_dtype, tm, tn, name):
    m, k = h.shape
    n = w.shape[1]
    return pl.pallas_call(
        functools.partial(_proj_kernel, act=act),
        grid=(m // tm, n // tn),
        in_specs=[pl.BlockSpec((tm, k), lambda i, j: (i, 0)),
                  pl.BlockSpec((k, tn), lambda i, j: (0, j))],
        out_specs=pl.BlockSpec((tm, tn), lambda i, j: (i, j)),
        out_shape=jax.ShapeDtypeStruct((m, n), out_dtype),
        compiler_params=_cparams(("arbitrary", "arbitrary")),
        name=name,
    )(h, w)


def _qproj_kernel(h_ref, w_ref, cs_ref, o_ref, *, nope, rope, scale):
    acc = _dot(h_ref[...], w_ref[0])
    t = acc[:, nope:] * cs_ref[...]
    r = t + pltpu.roll(t, rope, 1)
    lane = lax.broadcasted_iota(jnp.int32, r.shape, 1)
    r = jnp.where(lane < rope, r, 0.0)
    o_ref[0, :, :nope] = (acc[:, :nope] * scale).astype(o_ref.dtype)
    o_ref[0, :, nope:] = (r * scale).astype(o_ref.dtype)


def _qproj_call(h, w_q3, cs, tm, nope, rope, scale):
    m, k = h.shape
    nh, _, wd = w_q3.shape
    return pl.pallas_call(
        functools.partial(_qproj_kernel, nope=nope, rope=rope, scale=scale),
        grid=(m // tm, nh),
        in_specs=[pl.BlockSpec((tm, k), lambda i, j: (i, 0)),
                  pl.BlockSpec((1, k, wd), lambda i, j: (j, 0, 0)),
                  pl.BlockSpec((tm, 2 * rope), lambda i, j: (i, 0))],
        out_specs=pl.BlockSpec((1, tm, wd), lambda i, j: (j, i, 0)),
        out_shape=jax.ShapeDtypeStruct((nh, m, wd), BF16),
        compiler_params=_cparams(("arbitrary", "arbitrary")),
        name="qproj",
    )(h, w_q3, cs)


def _ckv_kernel(h_ref, w_ref, g_ref, cs_ref, ckv_ref, kr_ref, ckvb_ref, krb_ref, *, lora, rope):
    acc = _dot(h_ref[...], w_ref[...])
    c = acc[:, :lora]
    ms = jnp.mean(c * c, axis=-1, keepdims=True)
    ckv = c * lax.rsqrt(ms + RMS_EPS) * g_ref[...]
    ckv_ref[...] = ckv
    ckvb_ref[...] = ckv.astype(BF16)
    t = acc[:, lora:] * cs_ref[...]
    r = t + pltpu.roll(t, rope, 1)
    kr_ref[...] = r[:, :rope]
    lane = lax.broadcasted_iota(jnp.int32, r.shape, 1)
    krb_ref[...] = jnp.where(lane < rope, r, 0.0).astype(BF16)


def _ckv_call(h, w_ck, g_kv, cs, tm, lora, rope):
    m, k = h.shape
    wd = w_ck.shape[1]
    return pl.pallas_call(
        functools.partial(_ckv_kernel, lora=lora, rope=rope),
        grid=(m // tm,),
        in_specs=[pl.BlockSpec((tm, k), lambda i: (i, 0)),
                  pl.BlockSpec((k, wd), lambda i: (0, 0)),
                  pl.BlockSpec((1, lora), lambda i: (0, 0)),
                  pl.BlockSpec((tm, 2 * rope), lambda i: (i, 0))],
        out_specs=[pl.BlockSpec((tm, lora), lambda i: (i, 0)),
                   pl.BlockSpec((tm, rope), lambda i: (i, 0)),
                   pl.BlockSpec((tm, lora), lambda i: (i, 0)),
                   pl.BlockSpec((tm, 2 * rope), lambda i: (i, 0))],
        out_shape=[jax.ShapeDtypeStruct((m, lora), F32),
                   jax.ShapeDtypeStruct((m, rope), F32),
                   jax.ShapeDtypeStruct((m, lora), BF16),
                   jax.ShapeDtypeStruct((m, 2 * rope), BF16)],
        compiler_params=_cparams(("arbitrary",)),
        name="ckvproj",
    )(h, w_ck, g_kv.reshape(1, lora), cs)


def _s5_prep_kernel(lre_ref, lim_ref, ldt_ref, bre_ref, bim_ref, are_ref, aim_ref, bbre_ref, bbim_ref):
    lre = lre_ref[...]
    lim = lim_ref[...]
    dt = jnp.exp(ldt_ref[...])
    mag = jnp.exp(lre * dt)
    a_re = mag * jnp.cos(lim * dt)
    a_im = mag * jnp.sin(lim * dt)
    are_ref[...] = a_re
    aim_ref[...] = a_im
    den = lre * lre + lim * lim
    nr = a_re - 1.0
    c_re = (nr * lre + a_im * lim) / den
    c_im = (a_im * lre - nr * lim) / den
    bre = bre_ref[...]
    bim = bim_ref[...]
    bbre_ref[...] = c_re * bre - c_im * bim
    bbim_ref[...] = c_re * bim + c_im * bre


def _s5_prep_call(lam_re, lam_im, log_dt, b_re_t, b_im_t):
    g, p = lam_re.shape
    n = b_re_t.shape[1]
    a_re, a_im, bb_re, bb_im = pl.pallas_call(
        _s5_prep_kernel,
        out_shape=[jax.ShapeDtypeStruct((g, 1, p), F32), jax.ShapeDtypeStruct((g, 1, p), F32),
                   jax.ShapeDtypeStruct((g, n, p), F32), jax.ShapeDtypeStruct((g, n, p), F32)],
        name="s5prep",
    )(lam_re.reshape(g, 1, p), lam_im.reshape(g, 1, p), log_dt.reshape(g, 1, 1), b_re_t, b_im_t)
    return a_re.reshape(g, p), a_im.reshape(g, p), bb_re, bb_im


def _s5_kernel(*refs, n_u, tc, nseq, sw, has_h0):
    u_refs = refs[:n_u]
    i = n_u
    b2_ref, c_ref, a_ref, d_ref = refs[i:i + 4]
    i += 4
    if has_h0:
        h0re_ref, h0im_ref = refs[i:i + 2]
        i += 2
    gy_ref, sre_ref, sim_ref = refs[i:i + 3]
    i += 3
    bu_sc, hs_sc, y_sc, st_sc = refs[i:i + 4]
    assert tc % 8 == 0 and (nseq <= 8 or nseq % 8 == 0)

    step = pl.program_id(1)

    @pl.when(step == 0)
    def _init():
        if has_h0:
            st_sc[:, :sw] = h0re_ref[...]
            st_sc[:, sw:] = h0im_ref[...]
        else:
            st_sc[...] = jnp.zeros_like(st_sc)

    def u_of(q):
        return u_refs[q][...] if n_u > 1 else u_refs[0][q * tc:(q + 1) * tc, :]

    u = u_refs[0][...] if n_u == 1 else jnp.concatenate([r[...] for r in u_refs], axis=0)
    u_hi = u.astype(BF16)
    u_lo = (u - u_hi.astype(F32)).astype(BF16)
    bu = _dot(jnp.concatenate([u_hi, u_lo], axis=1), b2_ref[0])
    nj = sw // LANES
    for j in range(2 * nj):
        bu_sc[j] = bu[:, j * LANES:(j + 1) * LANES]

    for sg in range(0, nseq, 8):
        n8 = min(8, nseq - sg)
        base = sg * tc
        a_b = [jnp.broadcast_to(a_ref[0, r:r + 1, j * LANES:(j + 1) * LANES], (n8, LANES))
               for r in range(2) for j in range(nj)]

        def body(i8, carry, base=base, n8=n8, a_b=a_b):
            slab = base // (8 * n8) + i8
            h = list(carry)
            for k in range(8):
                rows = pl.ds(base + i8 * 8 + k, n8, stride=tc)
                new = [None] * (2 * nj)
                for j in range(nj):
                    new[j] = a_b[j] * h[j] - a_b[nj + j] * h[nj + j] + bu_sc[j, rows, :]
                    new[nj + j] = a_b[j] * h[nj + j] + a_b[nj + j] * h[j] + bu_sc[nj + j, rows, :]
                for j in range(2 * nj):
                    hs_sc[j, slab, k * n8:(k + 1) * n8, :] = new[j]
                h = new
            return tuple(h)

        init = tuple(st_sc[sg:sg + n8, j * LANES:(j + 1) * LANES] for j in range(2 * nj))
        fin = lax.fori_loop(0, tc // 8, body, init)
        for j in range(2 * nj):
            st_sc[sg:sg + n8, j * LANES:(j + 1) * LANES] = fin[j]

    hs = jnp.concatenate([hs_sc[j].reshape(nseq * tc, LANES) for j in range(2 * nj)], axis=1)
    yp = _dot(hs.astype(BF16), c_ref[0])
    nyb = yp.shape[1] // LANES
    for jb in range(nyb):
        y_sc[jb] = yp[:, jb * LANES:(jb + 1) * LANES]
    d = d_ref[...]
    for q in range(nseq):
        sg = (q // 8) * 8
        n8 = min(8, nseq - sg)
        rows = pl.ds(sg * tc + (q - sg), tc, stride=n8)
        y = jnp.concatenate([y_sc[jb, rows, :] for jb in range(nyb)], axis=1) + d * u_of(q)
        gy = (0.5 * y * (1.0 + lax.erf(y * (1.0 / math.sqrt(2.0))))).astype(gy_ref.dtype)
        if len(gy_ref.shape) == 3:
            gy_ref[q] = gy
        else:
            gy_ref[q * tc:(q + 1) * tc, :] = gy

    @pl.when(step == pl.num_programs(1) - 1)
    def _fin():
        sre_ref[...] = st_sc[:, :sw]
        sim_ref[...] = st_sc[:, sw:]


def _s5_scratch(nseq, tc, gk, sw):
    n8 = min(nseq, 8)
    nlb = 2 * sw // LANES
    return [pltpu.VMEM((nlb, nseq * tc, LANES), F32),
            pltpu.VMEM((nlb, nseq * tc // (8 * n8), 8 * n8, LANES), F32),
            pltpu.VMEM((gk // LANES, nseq * tc, LANES), F32),
            pltpu.VMEM((nseq, 2 * sw), F32)]


def _s5_weight_specs(gk, sw, idx):
    return [pl.BlockSpec((1, 2 * gk, 2 * sw), lambda *a: (idx(*a), 0, 0)),
            pl.BlockSpec((1, 2 * sw, gk), lambda *a: (idx(*a), 0, 0)),
            pl.BlockSpec((1, 2, sw), lambda *a: (idx(*a), 0, 0)),
            pl.BlockSpec((1, gk), lambda *a: (0, idx(*a)))]


def _s5_prompt_call(u_all, wts, nb, t, tc):
    b2, cmat, amat, dsk = wts
    ngb, gk2, sw2 = b2.shape
    gk, sw = gk2 // 2, sw2 // 2
    ntc = t // tc
    gb_of = lambda g, s: g
    u_specs = [pl.BlockSpec((tc, gk), functools.partial(lambda g, s, b: (b * ntc + s, g), b=b)) for b in range(nb)]
    kern = functools.partial(_s5_kernel, n_u=nb, tc=tc, nseq=nb, sw=sw, has_h0=False)
    return pl.pallas_call(
        kern,
        grid=(ngb, ntc),
        in_specs=u_specs + _s5_weight_specs(gk, sw, gb_of),
        out_specs=[pl.BlockSpec((nb, tc, gk), lambda g, s: (0, s, g)),
                   pl.BlockSpec((nb, sw), lambda g, s: (0, g)),
                   pl.BlockSpec((nb, sw), lambda g, s: (0, g))],
        out_shape=[jax.ShapeDtypeStruct((nb, t, ngb * gk), BF16),
                   jax.ShapeDtypeStruct((nb, ngb * sw), F32),
                   jax.ShapeDtypeStruct((nb, ngb * sw), F32)],
        scratch_shapes=_s5_scratch(nb, tc, gk, sw),
        compiler_params=_cparams(("arbitrary", "arbitrary")),
        name="s5_prompt",
    )(*([u_all] * nb), b2, cmat, amat, dsk)


def _s5_sample_call(u_all, wts, h0_re, h0_im, ns, ts, row_block0, sb):
    b2, cmat, amat, dsk = wts
    ngb, gk2, sw2 = b2.shape
    gk, sw = gk2 // 2, sw2 // 2
    gb_of = lambda g, s, z: g
    kern = functools.partial(_s5_kernel, n_u=1, tc=ts, nseq=sb, sw=sw, has_h0=True)
    return pl.pallas_call(
        kern,
        grid=(ngb, 1, ns // sb),
        in_specs=[pl.BlockSpec((sb * ts, gk), lambda g, s, z: (row_block0 + z, g))]
        + _s5_weight_specs(gk, sw, gb_of)
        + [pl.BlockSpec((sb, sw), lambda g, s, z: (z, g)), pl.BlockSpec((sb, sw), lambda g, s, z: (z, g))],
        out_specs=[pl.BlockSpec((sb * ts, gk), lambda g, s, z: (z, g)),
                   pl.BlockSpec((sb, sw), lambda g, s, z: (z, g)),
                   pl.BlockSpec((sb, sw), lambda g, s, z: (z, g))],
        out_shape=[jax.ShapeDtypeStruct((ns * ts, ngb * gk), BF16),
                   jax.ShapeDtypeStruct((ns, ngb * sw), F32),
                   jax.ShapeDtypeStruct((ns, ngb * sw), F32)],
        scratch_shapes=_s5_scratch(sb, ts, gk, sw),
        compiler_params=_cparams(("arbitrary", "arbitrary", "arbitrary")),
        name="s5_sample",
    )(u_all, b2, cmat, amat, dsk, h0_re, h0_im)


def _kvup_kernel(c_ref, kr_ref, w_ref, k_ref, v_ref, *, nope):
    acc = _dot(c_ref[...], w_ref[0])
    k_ref[0, :, :nope] = acc[:, :nope].astype(k_ref.dtype)
    k_ref[0, :, nope:] = kr_ref[...]
    v_ref[0] = acc[:, nope:].astype(v_ref.dtype)


def _kvup_call(ckv_b, kr_b, w_ukv, mp, tm, nope):
    lora = ckv_b.shape[1]
    nh, _, wd = w_ukv.shape
    vd = wd - nope
    rw = kr_b.shape[1]
    return pl.pallas_call(
        functools.partial(_kvup_kernel, nope=nope),
        grid=(mp // tm, nh),
        in_specs=[pl.BlockSpec((tm, lora), lambda i, j: (i, 0)),
                  pl.BlockSpec((tm, rw), lambda i, j: (i, 0)),
                  pl.BlockSpec((1, lora, wd), lambda i, j: (j, 0, 0))],
        out_specs=[pl.BlockSpec((1, tm, nope + rw), lambda i, j: (j, i, 0)),
                   pl.BlockSpec((1, tm, vd), lambda i, j: (j, i, 0))],
        out_shape=[jax.ShapeDtypeStruct((nh, mp, nope + rw), BF16),
                   jax.ShapeDtypeStruct((nh, mp, vd), BF16)],
        compiler_params=_cparams(("arbitrary", "arbitrary")),
        name="kvup",
    )(ckv_b, kr_b, w_ukv)


def _flash_kernel(q_ref, k_ref, v_ref, o_ref, m_sc, l_sc, acc_sc, *, tq):
    qi = pl.program_id(2)
    q = q_ref[0]
    m_sc[...] = jnp.full_like(m_sc, NEG_BIG)
    l_sc[...] = jnp.zeros_like(l_sc)
    acc_sc[...] = jnp.zeros_like(acc_sc)

    def block(ki, diagonal):
        start = pl.multiple_of(ki * tq, tq)
        s = _dot_nt(q, k_ref[0, pl.ds(start, tq), :])
        if diagonal:
            row = lax.broadcasted_iota(jnp.int32, s.shape, 0)
            col = lax.broadcasted_iota(jnp.int32, s.shape, 1)
            s = jnp.where(col <= row, s, NEG_BIG)
        m_prev = m_sc[...]
        m_new = jnp.maximum(m_prev, jnp.max(s, axis=-1, keepdims=True))
        alpha = jnp.exp2(m_prev - m_new)
        p = jnp.exp2(s - m_new)
        l_sc[...] = alpha * l_sc[...] + jnp.sum(p, axis=-1, keepdims=True)
        acc_sc[...] = alpha * acc_sc[...] + _dot(p.astype(BF16), v_ref[0, pl.ds(start, tq), :])
        m_sc[...] = m_new

    def below_diagonal(ki, carry):
        block(ki, False)
        return carry

    lax.fori_loop(0, qi, below_diagonal, 0)
    block(qi, True)
    o_ref[...] = (acc_sc[...] / l_sc[...]).astype(o_ref.dtype)


def _flash_call(q_hm, k_hm, v_hm, nb, t, m_total, tq):
    nh, _, qd = q_hm.shape
    vd = v_hm.shape[2]
    nq = t // tq
    return pl.pallas_call(
        functools.partial(_flash_kernel, tq=tq),
        grid=(nb, nh, nq),
        in_specs=[pl.BlockSpec((1, tq, qd), lambda b, h, i: (h, b * nq + i, 0)),
                  pl.BlockSpec((1, t, qd), lambda b, h, i: (h, b, 0)),
                  pl.BlockSpec((1, t, vd), lambda b, h, i: (h, b, 0))],
        out_specs=pl.BlockSpec((tq, vd), lambda b, h, i: (b * nq + i, h)),
        out_shape=jax.ShapeDtypeStruct((m_total, nh * vd), BF16),
        scratch_shapes=[pltpu.VMEM((tq, 1), F32), pltpu.VMEM((tq, 1), F32), pltpu.VMEM((tq, vd), F32)],
        compiler_params=_cparams(("arbitrary", "arbitrary", "arbitrary")),
        name="flash",
    )(q_hm, k_hm, v_hm)


def _qlat_kernel(q_ref, w_ref, o_ref):
    o_ref[0] = _dot(q_ref[0], w_ref[0])


def _qlat_call(q_hm, w_ukt, ms, row_block0):
    nh, nope, lora = w_ukt.shape
    return pl.pallas_call(
        _qlat_kernel,
        grid=(nh,),
        in_specs=[pl.BlockSpec((1, ms, nope), lambda h: (h, row_block0, 0)),
                  pl.BlockSpec((1, nope, lora), lambda h: (h, 0, 0))],
        out_specs=pl.BlockSpec((1, ms, lora), lambda h: (h, 0, 0)),
        out_shape=jax.ShapeDtypeStruct((nh, ms, lora), F32),
        compiler_params=_cparams(("arbitrary",)),
        name="qlat",
    )(q_hm, w_ukt)


def _decode_kernel(pt_ref, ql_ref, qr_ref, cn_ref, kn_ref, *rest, npg, sq, ts, rope):
    n = sq * npg
    ck_refs = rest[:n]
    kr_refs = rest[n:2 * n]
    o_ref = rest[2 * n]
    qlb_sc, qrb_sc, kb_sc, rb_sc, m_sc, l_sc, acc_sc = rest[2 * n + 1:]
    j = pl.program_id(1)
    nh, _, lora = ql_ref.shape
    rows = nh * ts
    pg = ck_refs[0].shape[0]

    @pl.when(j == 0)
    def _init():
        for a in range(sq):
            qlb_sc[a] = ql_ref[:, a * ts:(a + 1) * ts, :].reshape(rows, lora).astype(BF16)
            qrb_sc[a] = qr_ref[:, a * ts:(a + 1) * ts, :].reshape(rows, rope).astype(BF16)
        m_sc[...] = jnp.full_like(m_sc, NEG_BIG)
        l_sc[...] = jnp.zeros_like(l_sc)
        acc_sc[...] = jnp.zeros_like(acc_sc)

    def update(a, s, kb):
        m_prev = m_sc[a]
        m_new = jnp.maximum(m_prev, jnp.max(s, axis=-1, keepdims=True))
        alpha = jnp.exp2(m_prev - m_new)
        p = jnp.exp2(s - m_new)
        l_sc[a] = alpha * l_sc[a] + jnp.sum(p, axis=-1, keepdims=True)
        acc_sc[a] = alpha * acc_sc[a] + _dot(p.astype(BF16), kb)
        m_sc[a] = m_new

    for a in range(sq):
        for i in range(npg):
            kb_sc[a, i * pg:(i + 1) * pg, :] = ck_refs[a * npg + i][...].astype(BF16)
            rb_sc[a, :, i * pg:(i + 1) * pg] = kr_refs[a * npg + i][...].astype(BF16)
    for a in range(sq):
        kb = kb_sc[a]
        update(a, _dot_nt(qlb_sc[a], kb) + _dot(qrb_sc[a], rb_sc[a]), kb)

    @pl.when(j == pl.num_programs(1) - 1)
    def _fin():
        for a in range(sq):
            kn = jnp.concatenate([cn_ref[a * ts:(a + 1) * ts, :], jnp.zeros((pg - ts, lora), F32)], axis=0).astype(BF16)
            rn = jnp.concatenate([kn_ref[a * ts:(a + 1) * ts, :], jnp.zeros((pg - ts, rope), F32)], axis=0).astype(BF16)
            s = _dot_nt(qlb_sc[a], kn) + _dot_nt(qrb_sc[a], rn)
            tq = lax.broadcasted_iota(jnp.int32, s.shape, 0) % ts
            tk = lax.broadcasted_iota(jnp.int32, s.shape, 1)
            update(a, jnp.where(tk <= tq, s, NEG_BIG), kn)
            o = acc_sc[a] / l_sc[a]
            o_ref[:, a * ts:(a + 1) * ts, :] = o.reshape(nh, ts, lora)


def _decode_call(page_table, q_lat, q_rope, ckv_all, kr_all, cache_ckv, cache_krope_t, ns, ts, row_block0, npg, sq):
    nh, ms, lora = q_lat.shape
    rope = q_rope.shape[2]
    n_pages = page_table.shape[1]
    pg = cache_ckv.shape[1]
    nsteps = n_pages // npg
    rows = nh * ts

    def page_map(s, j, pt, a, i):
        return (pt[(s * sq + a) * n_pages + j * npg + i], 0, 0)

    slots = [(a, i) for a in range(sq) for i in range(npg)]
    ck_specs = [pl.BlockSpec((None, pg, lora), functools.partial(page_map, a=a, i=i)) for a, i in slots]
    kr_specs = [pl.BlockSpec((None, rope, pg), functools.partial(page_map, a=a, i=i)) for a, i in slots]
    grid_spec = pltpu.PrefetchScalarGridSpec(
        num_scalar_prefetch=1,
        grid=(ns // sq, nsteps),
        in_specs=[pl.BlockSpec((nh, sq * ts, lora), lambda s, j, pt: (0, s, 0)),
                  pl.BlockSpec((nh, sq * ts, rope), lambda s, j, pt: (0, s, 0)),
                  pl.BlockSpec((sq * ts, lora), lambda s, j, pt: (row_block0 + s, 0)),
                  pl.BlockSpec((sq * ts, rope), lambda s, j, pt: (row_block0 + s, 0))] + ck_specs + kr_specs,
        out_specs=pl.BlockSpec((nh, sq * ts, lora), lambda s, j, pt: (0, s, 0)),
        scratch_shapes=[pltpu.VMEM((sq, rows, lora), BF16), pltpu.VMEM((sq, rows, rope), BF16),
                        pltpu.VMEM((sq, npg * pg, lora), BF16), pltpu.VMEM((sq, rope, npg * pg), BF16),
                        pltpu.VMEM((sq, rows, 1), F32), pltpu.VMEM((sq, rows, 1), F32),
                        pltpu.VMEM((sq, rows, lora), F32)],
    )
    n = sq * npg
    return pl.pallas_call(
        functools.partial(_decode_kernel, npg=npg, sq=sq, ts=ts, rope=rope),
        grid_spec=grid_spec,
        out_shape=jax.ShapeDtypeStruct((nh, ms, lora), F32),
        compiler_params=_cparams(("arbitrary", "arbitrary")),
        name="decode",
    )(page_table.reshape(-1), q_lat, q_rope, ckv_all, kr_all, *([cache_ckv] * n), *([cache_krope_t] * n))


def _vup_kernel(o_ref, w_ref, att_any, out_ref):
    del att_any
    out_ref[...] = _dot(o_ref[0].astype(BF16), w_ref[0]).astype(out_ref.dtype)


def _vup_call(o_lat, w_uv3, att_all, row_block0):
    nh, ms, lora = o_lat.shape
    vd = w_uv3.shape[2]
    return pl.pallas_call(
        _vup_kernel,
        grid=(nh,),
        in_specs=[pl.BlockSpec((1, ms, lora), lambda h: (h, 0, 0)),
                  pl.BlockSpec((1, lora, vd), lambda h: (h, 0, 0)),
                  pl.BlockSpec(memory_space=pl.ANY)],
        out_specs=pl.BlockSpec((ms, vd), lambda h: (row_block0, h)),
        out_shape=jax.ShapeDtypeStruct(att_all.shape, att_all.dtype),
        input_output_aliases={2: 0},
        compiler_params=_cparams(("arbitrary",)),
        name="vup",
    )(o_lat, w_uv3, att_all)


def _glu_kernel(g_ref, wa_ref, wb_ref, z_ref, o_ref):
    g = g_ref[...]
    ga = _dot(g, wa_ref[...])
    gb = _dot(g, wb_ref[...])
    o_ref[...] = (ga * _sigmoid(gb) * z_ref[...].astype(F32)).astype(o_ref.dtype)


def _glu_call(gy, w_glu, zazs, tm, tn):
    m, k = gy.shape
    n = w_glu.shape[1] // 2
    nn = n // tn
    zs_blk0 = (zazs.shape[1] - n) // tn
    return pl.pallas_call(
        _glu_kernel,
        grid=(m // tm, nn),
        in_specs=[pl.BlockSpec((tm, k), lambda i, j: (i, 0)),
                  pl.BlockSpec((k, tn), lambda i, j: (0, j)),
                  pl.BlockSpec((k, tn), lambda i, j: (0, nn + j)),
                  pl.BlockSpec((tm, tn), lambda i, j: (i, zs_blk0 + j))],
        out_specs=pl.BlockSpec((tm, tn), lambda i, j: (i, j)),
        out_shape=jax.ShapeDtypeStruct((m, n), BF16),
        compiler_params=_cparams(("arbitrary", "arbitrary")),
        name="glu",
    )(gy, w_glu, w_glu, zazs)


def _merge_kernel(ys_ref, att_ref, za_ref, wbs_ref, wo_ref, gs_ref, ga_ref, o_ref, v_sc):
    @pl.when(pl.program_id(1) == 0)
    def _gate():
        v_sc[...] = (att_ref[...].astype(F32) * za_ref[...].astype(F32)).astype(BF16)

    p_s = _dot(ys_ref[...], wbs_ref[...])
    p_a = _dot(v_sc[...], wo_ref[...])
    o_ref[...] = (gs_ref[...].astype(F32) * p_s + ga_ref[...].astype(F32) * p_a).astype(o_ref.dtype)


def _merge_call(ys2, att, zazs, w_bs, w_o, gates, tm, tn):
    m, ks = ys2.shape
    ka = att.shape[1]
    n = w_bs.shape[1]
    nn = n // tn
    return pl.pallas_call(
        _merge_kernel,
        grid=(m // tm, nn),
        in_specs=[pl.BlockSpec((tm, ks), lambda i, j: (i, 0)),
                  pl.BlockSpec((tm, ka), lambda i, j: (i, 0)),
                  pl.BlockSpec((tm, ka), lambda i, j: (i, 0)),
                  pl.BlockSpec((ks, tn), lambda i, j: (0, j)),
                  pl.BlockSpec((ka, tn), lambda i, j: (0, j)),
                  pl.BlockSpec((tm, tn), lambda i, j: (i, j)),
                  pl.BlockSpec((tm, tn), lambda i, j: (i, nn + j))],
        out_specs=pl.BlockSpec((tm, tn), lambda i, j: (i, j)),
        out_shape=jax.ShapeDtypeStruct((m, n), BF16),
        scratch_shapes=[pltpu.VMEM((tm, ka), BF16)],
        compiler_params=_cparams(("arbitrary", "arbitrary")),
        name="merge",
    )(ys2, att, zazs, w_bs, w_o, gates, gates)


def _final_kernel(mg_ref, w_ref, x_ref, gate_ref, g_ref, b_ref, o_ref, *, alpha):
    out = _dot(mg_ref[...], w_ref[...])
    x = x_ref[...]
    y = alpha * x + gate_ref[...] * out.reshape(x.shape)
    mu = jnp.mean(y, axis=-1, keepdims=True)
    yc = y - mu
    var = jnp.mean(yc * yc, axis=-1, keepdims=True)
    o_ref[...] = yc * lax.rsqrt(var + LN_EPS) * g_ref[...] + b_ref[...]


def _final_call(merged, w_out, x, mod3, ln_g, ln_b, alpha, x_blk, gate_blk, gate_map, merged_map, grid):
    d = w_out.shape[0]
    rows = x_blk[0] * x_blk[1]
    nd = len(grid)
    x_map = (lambda i, j: (i, j, 0)) if nd == 2 else (lambda i: (i, 0, 0))
    const2 = (lambda i, j: (0, 0)) if nd == 2 else (lambda i: (0, 0))
    const3 = (lambda i, j: (0, 0, 0)) if nd == 2 else (lambda i: (0, 0, 0))
    return pl.pallas_call(
        functools.partial(_final_kernel, alpha=alpha),
        grid=grid,
        in_specs=[pl.BlockSpec((rows, d), merged_map),
                  pl.BlockSpec((d, d), const2),
                  pl.BlockSpec(x_blk, x_map),
                  pl.BlockSpec(gate_blk, gate_map),
                  pl.BlockSpec((1, 1, d), const3),
                  pl.BlockSpec((1, 1, d), const3)],
        out_specs=pl.BlockSpec(x_blk, x_map),
        out_shape=jax.ShapeDtypeStruct(x.shape, F32),
        compiler_params=_cparams(("arbitrary",) * nd),
        name="final",
    )(merged, w_out, x, mod3, ln_g.reshape(1, 1, d), ln_b.reshape(1, 1, d))


def _rope_tables(pos, rope):
    freqs = ROPE_BASE ** (-jnp.arange(0, rope, 2, dtype=F32) / rope)
    ang = pos.astype(F32)[:, None] * freqs[None, :]
    cos, sin = jnp.cos(ang), jnp.sin(ang)
    return jnp.concatenate([cos, cos, sin, sin], axis=-1)


def _rot_cols(w, rope):
    half = rope // 2
    return jnp.concatenate([-w[..., half:], w[..., :half]], axis=-1)


def _block_diag(x, gblk):
    g, r, c = x.shape
    x4 = x.reshape(g // gblk, gblk, r, c)
    eye = jnp.eye(gblk, dtype=x.dtype)
    return jnp.einsum("bgrc,gh->bgrhc", x4, eye).reshape(g // gblk, gblk * r, gblk * c)


def kernel(x_prompt, x_sample, c_prompt, c_sample, cache_ckv, cache_krope, state_ssm_re, state_ssm_im, page_table, w_ada, b_ada, w_in, g_kv, w_uk, w_uv, w_o, lam_re, lam_im, log_dt, b_re, b_im, c_re, c_im, d_skip, w_glu, w_bs, w_out, ln_g, ln_b):
    nb, t, d = x_prompt.shape
    ns, ts, _ = x_sample.shape
    depth = w_in.shape[0]
    assert depth == 1, "single-layer step"
    lora, nh, nope = w_uk.shape[1:]
    vd = w_uv.shape[3]
    rope = cache_krope.shape[3]
    d_ssm = d_skip.shape[1]
    ngrp, nstate, gch = b_re.shape[1:]
    d_attn = nh * vd
    pg = cache_ckv.shape[2]
    past = page_table.shape[1] * pg
    mp, ms = nb * t, ns * ts
    m = mp + ms
    scale = float((nope + rope) ** -0.5) * LOG2E
    alpha = float((2 * depth) ** 0.25)

    tm = math.gcd(math.gcd(mp, ms), 1024)
    tmh = min(tm, 512)
    tn = 512
    assert rope * 2 == 128 and nope == 128 and vd == 128

    wi = w_in[0]
    o_u, o_zs, o_q = 0, d_ssm, 2 * d_ssm
    o_ckv = o_q + nh * (nope + rope)
    o_kr = o_ckv + lora
    o_za = o_kr + rope
    o_gs = o_za + d_attn
    o_ga = o_gs + d
    w_u = wi[:, o_u:o_zs].astype(BF16)
    w_z = jnp.concatenate([wi[:, o_za:o_gs], wi[:, o_zs:o_q]], axis=1).astype(BF16)
    w_g = wi[:, o_gs:].astype(BF16)
    wq = wi[:, o_q:o_ckv].reshape(d, nh, nope + rope)
    wq_r = wq[:, :, nope:]
    w_q3 = jnp.concatenate([wq, _rot_cols(wq_r, rope)], axis=-1).transpose(1, 0, 2).astype(BF16)
    wkr = wi[:, o_kr:o_za]
    w_ck = jnp.concatenate([wi[:, o_ckv:o_kr], wkr, _rot_cols(wkr, rope)], axis=1).astype(BF16)
    w_ukv = jnp.concatenate([w_uk[0], w_uv[0]], axis=-1).transpose(1, 0, 2).astype(BF16)
    w_ukt = w_uk[0].transpose(1, 2, 0).astype(BF16)
    w_uv3 = w_uv[0].transpose(1, 0, 2).astype(BF16)
    w_glu_b = w_glu[0].astype(BF16)
    w_bs_b = w_bs[0].astype(BF16)
    w_o_b = w_o[0].astype(BF16)
    w_out_b = w_out[0].astype(BF16)

    pos = jnp.concatenate([jnp.tile(jnp.arange(t), nb), jnp.tile(past + jnp.arange(ts), ns)])
    cs = _rope_tables(pos, rope)

    c_all = jnp.concatenate([c_sample, c_prompt], axis=0)
    pad = (-c_all.shape[0]) % 8
    c_all = jnp.concatenate([c_all, jnp.zeros((pad, d), F32)], axis=0)
    mod = _mod_call(c_all, w_ada[0], b_ada[0], tn)
    mod3 = mod.reshape(mod.shape[0], 1, 3 * d)

    sb = max(1, min(ns, tmh // ts))
    h_all = _hmod_prompt(x_prompt, mod3, ns, m, tmh)
    h_all = _hmod_sample(x_sample, mod3, h_all, mp // (sb * ts), sb)

    u_all = _proj_call(h_all, w_u, None, F32, tm, tn, "proj_u")
    zazs = _proj_call(h_all, w_z, "silu", BF16, tm, tn, "proj_z")
    gates = _proj_call(h_all, w_g, "sigmoid", BF16, tm, tn, "proj_g")
    q_hm = _qproj_call(h_all, w_q3, cs, tm, nope, rope, scale)
    ckv_all, kr_all, ckv_b, kr_b = _ckv_call(h_all, w_ck, g_kv[0], cs, tmh, lora, rope)

    gblk = SSM_GROUP_BLOCK
    a_re, a_im, bb_re, bb_im = _s5_prep_call(lam_re[0], lam_im[0], log_dt[0],
                                             b_re[0].transpose(0, 2, 1), b_im[0].transpose(0, 2, 1))
    bcat = jnp.concatenate([_block_diag(bb_re, gblk), _block_diag(bb_im, gblk)], axis=-1)
    b2 = jnp.concatenate([bcat, bcat], axis=1).astype(BF16)
    ccat = jnp.concatenate([_block_diag(c_re[0].transpose(0, 2, 1), gblk),
                            _block_diag(-c_im[0].transpose(0, 2, 1), gblk)], axis=1).astype(BF16)
    ngb = ngrp // gblk
    sw = gblk * nstate
    amat = jnp.stack([a_re.reshape(ngb, sw), a_im.reshape(ngb, sw)], axis=1)
    s5w = (b2, ccat, amat, d_skip)
    tc = min(t, S5_TIME_CHUNK)
    gy_p, sre_p, sim_p = _s5_prompt_call(u_all, s5w, nb, t, tc)
    ssb = min(ns, 32)
    gy_s, sre_s, sim_s = _s5_sample_call(u_all, s5w, state_ssm_re[0].reshape(ns, ngrp * nstate),
                                         state_ssm_im[0].reshape(ns, ngrp * nstate), ns, ts, mp // (ssb * ts), ssb)
    gy = jnp.concatenate([gy_p.reshape(mp, d_ssm), gy_s], axis=0)

    k_hm, v_hm = _kvup_call(ckv_b, kr_b, w_ukv, mp, tm, nope)
    tq = min(t, FLASH_BLOCK)
    att = _flash_call(q_hm, k_hm, v_hm, nb, t, m, tq)
    q_lat = _qlat_call(q_hm, w_ukt, ms, mp // ms)
    q_rope_s = q_hm[:, mp:, nope:nope + rope].astype(F32)
    sq = DECODE_SEQS_PER_STEP
    krope_t = jnp.swapaxes(cache_krope, 2, 3).reshape(cache_krope.shape[1], rope, pg)
    o_lat = _decode_call(page_table, q_lat, q_rope_s, ckv_all, kr_all, cache_ckv.reshape(cache_ckv.shape[1:]), krope_t,
                         ns, ts, mp // (sq * ts), PAGES_PER_STEP, sq)
    att = _vup_call(o_lat, w_uv3, att, mp // ms)

    ys2 = _glu_call(gy, w_glu_b, zazs, tm, tn)
    merged = _merge_call(ys2, att, zazs, w_bs_b, w_o_b, gates, tmh, tn)
    ntp = t // tmh
    y_p = _final_call(merged, w_out_b, x_prompt, mod3, ln_g[0], ln_b[0], alpha,
                      (1, tmh, d), (1, 1, d), lambda i, j: (ns + i, 0, 2), lambda i, j: (i * ntp + j, 0), (nb, ntp))
    y_s = _final_call(merged, w_out_b, x_sample, mod3, ln_g[0], ln_b[0], alpha,
                      (sb, ts, d), (sb, 1, d), lambda i: (i, 0, 2), lambda i: (mp // (sb * ts) + i, 0), (ns // sb,))

    st = lambda a, n: a.reshape(1, n, ngrp, nstate)
    return (y_p, y_s,
            ckv_all[:mp].reshape(1, nb, t, lora), kr_all[:mp].reshape(1, nb, t, rope), st(sre_p, nb), st(sim_p, nb),
            ckv_all[mp:].reshape(1, ns, ts, lora), kr_all[mp:].reshape(1, ns, ts, rope), st(sre_s, ns), st(sim_s, ns))
```

```python
import functools
import math

import jax
import jax.numpy as jnp
from jax import lax
from jax.experimental import pallas as pl
from jax.experimental.pallas import tpu as pltpu

F32 = jnp.float32
BF16 = jnp.bfloat16

LN_EPS = 1e-5
RMS_EPS = 1e-6
ROPE_BASE = 10000.0
NEG_BIG = -1e30
VMEM_LIMIT = 48 * 1024 * 1024
PAGES_PER_STEP = 16
DECODE_SEQS_PER_STEP = 2
LOG2E = 1.4426950408889634
SSM_GROUP_BLOCK = 8
LANES = 128
S5_TIME_CHUNK = 256
FLASH_BLOCK = 1024


def _cparams(sem):
    return pltpu.CompilerParams(dimension_semantics=sem, vmem_limit_bytes=VMEM_LIMIT)


def _sigmoid(x):
    return 1.0 / (1.0 + jnp.exp(-x))


def _silu(x):
    return x * _sigmoid(x)


def _dot(a, b):
    return jnp.dot(a, b, preferred_element_type=F32)


def _dot_nt(a, b):
    return lax.dot_general(a, b, (((1,), (1,)), ((), ())), preferred_element_type=F32)


def _mod_kernel(c_ref, w_ref, b_ref, o_ref):
    a = _silu(c_ref[...]).astype(BF16)
    o_ref[...] = _dot(a, w_ref[...].astype(BF16)) + b_ref[...]


def _mod_call(c_all, w_ada, b_ada, tn):
    r, d = c_all.shape
    n = w_ada.shape[1]
    return pl.pallas_call(
        _mod_kernel,
        grid=(n // tn,),
        in_specs=[pl.BlockSpec((r, d), lambda j: (0, 0)),
                  pl.BlockSpec((d, tn), lambda j: (0, j)),
                  pl.BlockSpec((1, tn), lambda j: (0, j))],
        out_specs=pl.BlockSpec((r, tn), lambda j: (0, j)),
        out_shape=jax.ShapeDtypeStruct((r, n), F32),
        compiler_params=_cparams(("arbitrary",)),
        name="mod",
    )(c_all, w_ada, b_ada.reshape(1, n))


def _hmod_kernel(x_ref, sh_ref, sc_ref, *rest):
    o_ref = rest[-1]
    h = x_ref[...] * (1.0 + sc_ref[...]) + sh_ref[...]
    o_ref[...] = h.reshape(o_ref.shape).astype(o_ref.dtype)


def _hmod_prompt_kernel(x_ref, sh_ref, sc_ref, o_ref, *, nb):
    @pl.when(pl.program_id(0) < nb)
    def _rows():
        _hmod_kernel(x_ref, sh_ref, sc_ref, o_ref)

    @pl.when(pl.program_id(0) == nb)
    def _tail():
        o_ref[...] = jnp.zeros_like(o_ref)


def _hmod_prompt(x, mod3, mod_row0, m_total, tm):
    b, t, d = x.shape
    nt = t // tm
    last = m_total // tm - 1
    assert b * nt + nt > last
    bc = lambda i: jnp.minimum(i, b - 1)
    return pl.pallas_call(
        functools.partial(_hmod_prompt_kernel, nb=b),
        grid=(b + 1, nt),
        in_specs=[pl.BlockSpec((1, tm, d), lambda i, j: (bc(i), jnp.where(i < b, j, nt - 1), 0)),
                  pl.BlockSpec((1, 1, d), lambda i, j: (mod_row0 + bc(i), 0, 0)),
                  pl.BlockSpec((1, 1, d), lambda i, j: (mod_row0 + bc(i), 0, 1))],
        out_specs=pl.BlockSpec((tm, d), lambda i, j: (jnp.minimum(i * nt + j, last), 0)),
        out_shape=jax.ShapeDtypeStruct((m_total, d), BF16),
        compiler_params=_cparams(("arbitrary", "arbitrary")),
        name="hmod_prompt",
    )(x, mod3, mod3)


def _hmod_sample(x, mod3, h_all, row_block0, sb):
    ns, ts, d = x.shape
    return pl.pallas_call(
        _hmod_kernel,
        grid=(ns // sb,),
        in_specs=[pl.BlockSpec((sb, ts, d), lambda i: (i, 0, 0)),
                  pl.BlockSpec((sb, 1, d), lambda i: (i, 0, 0)),
                  pl.BlockSpec((sb, 1, d), lambda i: (i, 0, 1)),
                  pl.BlockSpec(memory_space=pl.ANY)],
        out_specs=pl.BlockSpec((sb * ts, d), lambda i: (row_block0 + i, 0)),
        out_shape=jax.ShapeDtypeStruct(h_all.shape, h_all.dtype),
        input_output_aliases={3: 0},
        compiler_params=_cparams(("arbitrary",)),
        name="hmod_sample",
    )(x, mod3, mod3, h_all)


def _proj_kernel(h_ref, w_ref, o_ref, *, act):
    acc = _dot_nt(h_ref[...], w_ref[...])
    if act == "silu":
        acc = _silu(acc)
    elif act == "sigmoid":
        acc = _sigmoid(acc)
    o_ref[...] = acc.astype(o_ref.dtype)


def _proj_call(h, w, act, out_dtype, tm, tn, name):
    m, k = h.shape
    n = w.shape[0]
    return pl.pallas_call(
        functools.partial(_proj_kernel, act=act),
        grid=(m // tm, n // tn),
        in_specs=[pl.BlockSpec((tm, k), lambda i, j: (i, 0)),
                  pl.BlockSpec((tn, k), lambda i, j: (j, 0))],
        out_specs=pl.BlockSpec((tm, tn), lambda i, j: (i, j)),
        out_shape=jax.ShapeDtypeStruct((m, n), out_dtype),
        compiler_params=_cparams(("arbitrary", "arbitrary")),
        name=name,
    )(h, w)


def _qproj_kernel(h_ref, w_ref, cs_ref, o_ref, *, nope, rope, scale):
    acc = _dot_nt(h_ref[...], w_ref[0])
    t = acc[:, nope:] * cs_ref[...]
    r = t + pltpu.roll(t, rope, 1)
    lane = lax.broadcasted_iota(jnp.int32, r.shape, 1)
    r = jnp.where(lane < rope, r, 0.0)
    o_ref[0, :, :nope] = (acc[:, :nope] * scale).astype(o_ref.dtype)
    o_ref[0, :, nope:] = (r * scale).astype(o_ref.dtype)


def _qproj_call(h, w_q3, cs, tm, nope, rope, scale):
    m, k = h.shape
    nh, wd, _ = w_q3.shape
    return pl.pallas_call(
        functools.partial(_qproj_kernel, nope=nope, rope=rope, scale=scale),
        grid=(m // tm, nh),
        in_specs=[pl.BlockSpec((tm, k), lambda i, j: (i, 0)),
                  pl.BlockSpec((1, wd, k), lambda i, j: (j, 0, 0)),
                  pl.BlockSpec((tm, 2 * rope), lambda i, j: (i, 0))],
        out_specs=pl.BlockSpec((1, tm, wd), lambda i, j: (j, i, 0)),
        out_shape=jax.ShapeDtypeStruct((nh, m, wd), BF16),
        compiler_params=_cparams(("arbitrary", "arbitrary")),
        name="qproj",
    )(h, w_q3, cs)


def _ckv_kernel(h_ref, w_ref, g_ref, cs_ref, ckv_ref, kr_ref, ckvb_ref, krb_ref, *, lora, rope):
    acc = _dot_nt(h_ref[...], w_ref[...])
    c = acc[:, :lora]
    ms = jnp.mean(c * c, axis=-1, keepdims=True)
    ckv = c * lax.rsqrt(ms + RMS_EPS) * g_ref[...]
    ckv_ref[...] = ckv
    ckvb_ref[...] = ckv.astype(BF16)
    t = acc[:, lora:] * cs_ref[...]
    r = t + pltpu.roll(t, rope, 1)
    kr_ref[...] = r[:, :rope]
    lane = lax.broadcasted_iota(jnp.int32, r.shape, 1)
    krb_ref[...] = jnp.where(lane < rope, r, 0.0).astype(BF16)


def _ckv_call(h, w_ck, g_kv, cs, tm, lora, rope):
    m, k = h.shape
    wd = w_ck.shape[0]
    return pl.pallas_call(
        functools.partial(_ckv_kernel, lora=lora, rope=rope),
        grid=(m // tm,),
        in_specs=[pl.BlockSpec((tm, k), lambda i: (i, 0)),
                  pl.BlockSpec((wd, k), lambda i: (0, 0)),
                  pl.BlockSpec((1, lora), lambda i: (0, 0)),
                  pl.BlockSpec((tm, 2 * rope), lambda i: (i, 0))],
        out_specs=[pl.BlockSpec((tm, lora), lambda i: (i, 0)),
                   pl.BlockSpec((tm, rope), lambda i: (i, 0)),
                   pl.BlockSpec((tm, lora), lambda i: (i, 0)),
                   pl.BlockSpec((tm, 2 * rope), lambda i: (i, 0))],
        out_shape=[jax.ShapeDtypeStruct((m, lora), F32),
                   jax.ShapeDtypeStruct((m, rope), F32),
                   jax.ShapeDtypeStruct((m, lora), BF16),
                   jax.ShapeDtypeStruct((m, 2 * rope), BF16)],
        compiler_params=_cparams(("arbitrary",)),
        name="ckvproj",
    )(h, w_ck, g_kv.reshape(1, lora), cs)


def _s5_prep_kernel(lre_ref, lim_ref, ldt_ref, bre_ref, bim_ref, are_ref, aim_ref, bbre_ref, bbim_ref):
    lre = lre_ref[...]
    lim = lim_ref[...]
    dt = jnp.exp(ldt_ref[...])
    mag = jnp.exp(lre * dt)
    a_re = mag * jnp.cos(lim * dt)
    a_im = mag * jnp.sin(lim * dt)
    are_ref[...] = a_re
    aim_ref[...] = a_im
    den = lre * lre + lim * lim
    nr = a_re - 1.0
    c_re = (nr * lre + a_im * lim) / den
    c_im = (a_im * lre - nr * lim) / den
    bre = bre_ref[...]
    bim = bim_ref[...]
    bbre_ref[...] = c_re * bre - c_im * bim
    bbim_ref[...] = c_re * bim + c_im * bre


def _s5_prep_call(lam_re, lam_im, log_dt, b_re_t, b_im_t):
    g, p = lam_re.shape
    n = b_re_t.shape[1]
    a_re, a_im, bb_re, bb_im = pl.pallas_call(
        _s5_prep_kernel,
        out_shape=[jax.ShapeDtypeStruct((g, 1, p), F32), jax.ShapeDtypeStruct((g, 1, p), F32),
                   jax.ShapeDtypeStruct((g, n, p), F32), jax.ShapeDtypeStruct((g, n, p), F32)],
        name="s5prep",
    )(lam_re.reshape(g, 1, p), lam_im.reshape(g, 1, p), log_dt.reshape(g, 1, 1), b_re_t, b_im_t)
    return a_re.reshape(g, p), a_im.reshape(g, p), bb_re, bb_im


def _s5_kernel(*refs, n_u, tc, nseq, sw, has_h0):
    u_refs = refs[:n_u]
    i = n_u
    b2_ref, c_ref, a_ref, d_ref = refs[i:i + 4]
    i += 4
    if has_h0:
        h0re_ref, h0im_ref = refs[i:i + 2]
        i += 2
    gy_ref, sre_ref, sim_ref = refs[i:i + 3]
    i += 3
    bu_sc, hs_sc, y_sc, st_sc = refs[i:i + 4]
    assert tc % 8 == 0 and (nseq <= 8 or nseq % 8 == 0)

    step = pl.program_id(1)

    @pl.when(step == 0)
    def _init():
        if has_h0:
            st_sc[:, :sw] = h0re_ref[...]
            st_sc[:, sw:] = h0im_ref[...]
        else:
            st_sc[...] = jnp.zeros_like(st_sc)

    def u_of(q):
        return u_refs[q][...] if n_u > 1 else u_refs[0][q * tc:(q + 1) * tc, :]

    u = u_refs[0][...] if n_u == 1 else jnp.concatenate([r[...] for r in u_refs], axis=0)
    u_hi = u.astype(BF16)
    u_lo = (u - u_hi.astype(F32)).astype(BF16)
    bu = _dot(jnp.concatenate([u_hi, u_lo], axis=1), b2_ref[0])
    nj = sw // LANES
    for j in range(2 * nj):
        bu_sc[j] = bu[:, j * LANES:(j + 1) * LANES]

    for sg in range(0, nseq, 8):
        n8 = min(8, nseq - sg)
        base = sg * tc
        a_b = [jnp.broadcast_to(a_ref[0, r:r + 1, j * LANES:(j + 1) * LANES], (n8, LANES))
               for r in range(2) for j in range(nj)]

        def body(i8, carry, base=base, n8=n8, a_b=a_b):
            slab = base // (8 * n8) + i8
            h = list(carry)
            for k in range(8):
                rows = pl.ds(base + i8 * 8 + k, n8, stride=tc)
                new = [None] * (2 * nj)
                for j in range(nj):
                    new[j] = a_b[j] * h[j] - a_b[nj + j] * h[nj + j] + bu_sc[j, rows, :]
                    new[nj + j] = a_b[j] * h[nj + j] + a_b[nj + j] * h[j] + bu_sc[nj + j, rows, :]
                for j in range(2 * nj):
                    hs_sc[j, slab, k * n8:(k + 1) * n8, :] = new[j]
                h = new
            return tuple(h)

        init = tuple(st_sc[sg:sg + n8, j * LANES:(j + 1) * LANES] for j in range(2 * nj))
        fin = lax.fori_loop(0, tc // 8, body, init)
        for j in range(2 * nj):
            st_sc[sg:sg + n8, j * LANES:(j + 1) * LANES] = fin[j]

    hs = jnp.concatenate([hs_sc[j].reshape(nseq * tc, LANES) for j in range(2 * nj)], axis=1)
    yp = _dot(hs.astype(BF16), c_ref[0])
    nyb = yp.shape[1] // LANES
    for jb in range(nyb):
        y_sc[jb] = yp[:, jb * LANES:(jb + 1) * LANES]
    d = d_ref[...]
    for q in range(nseq):
        sg = (q // 8) * 8
        n8 = min(8, nseq - sg)
        rows = pl.ds(sg * tc + (q - sg), tc, stride=n8)
        y = jnp.concatenate([y_sc[jb, rows, :] for jb in range(nyb)], axis=1) + d * u_of(q)
        gy = (0.5 * y * (1.0 + lax.erf(y * (1.0 / math.sqrt(2.0))))).astype(gy_ref.dtype)
        if len(gy_ref.shape) == 3:
            gy_ref[q] = gy
        else:
            gy_ref[q * tc:(q + 1) * tc, :] = gy

    @pl.when(step == pl.num_programs(1) - 1)
    def _fin():
        sre_ref[...] = st_sc[:, :sw]
        sim_ref[...] = st_sc[:, sw:]


def _s5_scratch(nseq, tc, gk, sw):
    n8 = min(nseq, 8)
    nlb = 2 * sw // LANES
    return [pltpu.VMEM((nlb, nseq * tc, LANES), F32),
            pltpu.VMEM((nlb, nseq * tc // (8 * n8), 8 * n8, LANES), F32),
            pltpu.VMEM((gk // LANES, nseq * tc, LANES), F32),
            pltpu.VMEM((nseq, 2 * sw), F32)]


def _s5_weight_specs(gk, sw, idx):
    return [pl.BlockSpec((1, 2 * gk, 2 * sw), lambda *a: (idx(*a), 0, 0)),
            pl.BlockSpec((1, 2 * sw, gk), lambda *a: (idx(*a), 0, 0)),
            pl.BlockSpec((1, 2, sw), lambda *a: (idx(*a), 0, 0)),
            pl.BlockSpec((1, gk), lambda *a: (0, idx(*a)))]


def _s5_prompt_call(u_all, wts, nb, t, tc):
    b2, cmat, amat, dsk = wts
    ngb, gk2, sw2 = b2.shape
    gk, sw = gk2 // 2, sw2 // 2
    ntc = t // tc
    gb_of = lambda g, s: g
    u_specs = [pl.BlockSpec((tc, gk), functools.partial(lambda g, s, b: (b * ntc + s, g), b=b)) for b in range(nb)]
    kern = functools.partial(_s5_kernel, n_u=nb, tc=tc, nseq=nb, sw=sw, has_h0=False)
    return pl.pallas_call(
        kern,
        grid=(ngb, ntc),
        in_specs=u_specs + _s5_weight_specs(gk, sw, gb_of),
        out_specs=[pl.BlockSpec((nb, tc, gk), lambda g, s: (0, s, g)),
                   pl.BlockSpec((nb, sw), lambda g, s: (0, g)),
                   pl.BlockSpec((nb, sw), lambda g, s: (0, g))],
        out_shape=[jax.ShapeDtypeStruct((nb, t, ngb * gk), BF16),
                   jax.ShapeDtypeStruct((nb, ngb * sw), F32),
                   jax.ShapeDtypeStruct((nb, ngb * sw), F32)],
        scratch_shapes=_s5_scratch(nb, tc, gk, sw),
        compiler_params=_cparams(("arbitrary", "arbitrary")),
        name="s5_prompt",
    )(*([u_all] * nb), b2, cmat, amat, dsk)


def _s5_sample_call(u_all, wts, h0_re, h0_im, ns, ts, row_block0, sb):
    b2, cmat, amat, dsk = wts
    ngb, gk2, sw2 = b2.shape
    gk, sw = gk2 // 2, sw2 // 2
    gb_of = lambda g, s, z: g
    kern = functools.partial(_s5_kernel, n_u=1, tc=ts, nseq=sb, sw=sw, has_h0=True)
    return pl.pallas_call(
        kern,
        grid=(ngb, 1, ns // sb),
        in_specs=[pl.BlockSpec((sb * ts, gk), lambda g, s, z: (row_block0 + z, g))]
        + _s5_weight_specs(gk, sw, gb_of)
        + [pl.BlockSpec((sb, sw), lambda g, s, z: (z, g)), pl.BlockSpec((sb, sw), lambda g, s, z: (z, g))],
        out_specs=[pl.BlockSpec((sb * ts, gk), lambda g, s, z: (z, g)),
                   pl.BlockSpec((sb, sw), lambda g, s, z: (z, g)),
                   pl.BlockSpec((sb, sw), lambda g, s, z: (z, g))],
        out_shape=[jax.ShapeDtypeStruct((ns * ts, ngb * gk), BF16),
                   jax.ShapeDtypeStruct((ns, ngb * sw), F32),
                   jax.ShapeDtypeStruct((ns, ngb * sw), F32)],
        scratch_shapes=_s5_scratch(sb, ts, gk, sw),
        compiler_params=_cparams(("arbitrary", "arbitrary", "arbitrary")),
        name="s5_sample",
    )(u_all, b2, cmat, amat, dsk, h0_re, h0_im)


def _kvup_kernel(c_ref, kr_ref, w_ref, k_ref, v_ref, *, nope):
    acc = _dot(c_ref[...], w_ref[0])
    k_ref[0, :, :nope] = acc[:, :nope].astype(k_ref.dtype)
    k_ref[0, :, nope:] = kr_ref[...]
    v_ref[0] = acc[:, nope:].astype(v_ref.dtype)


def _kvup_call(ckv_b, kr_b, w_ukv, mp, tm, nope):
    lora = ckv_b.shape[1]
    nh, _, wd = w_ukv.shape
    vd = wd - nope
    rw = kr_b.shape[1]
    return pl.pallas_call(
        functools.partial(_kvup_kernel, nope=nope),
        grid=(mp // tm, nh),
        in_specs=[pl.BlockSpec((tm, lora), lambda i, j: (i, 0)),
                  pl.BlockSpec((tm, rw), lambda i, j: (i, 0)),
                  pl.BlockSpec((1, lora, wd), lambda i, j: (j, 0, 0))],
        out_specs=[pl.BlockSpec((1, tm, nope + rw), lambda i, j: (j, i, 0)),
                   pl.BlockSpec((1, tm, vd), lambda i, j: (j, i, 0))],
        out_shape=[jax.ShapeDtypeStruct((nh, mp, nope + rw), BF16),
                   jax.ShapeDtypeStruct((nh, mp, vd), BF16)],
        compiler_params=_cparams(("arbitrary", "arbitrary")),
        name="kvup",
    )(ckv_b, kr_b, w_ukv)


def _flash_kernel(q_ref, k_ref, v_ref, o_ref, m_sc, l_sc, acc_sc, *, tq):
    qi = pl.program_id(2)
    q = q_ref[0]
    m_sc[...] = jnp.full_like(m_sc, NEG_BIG)
    l_sc[...] = jnp.zeros_like(l_sc)
    acc_sc[...] = jnp.zeros_like(acc_sc)

    def block(ki, diagonal):
        start = pl.multiple_of(ki * tq, tq)
        s = _dot_nt(q, k_ref[0, pl.ds(start, tq), :])
        if diagonal:
            row = lax.broadcasted_iota(jnp.int32, s.shape, 0)
            col = lax.broadcasted_iota(jnp.int32, s.shape, 1)
            s = jnp.where(col <= row, s, NEG_BIG)
        m_prev = m_sc[...]
        m_new = jnp.maximum(m_prev, jnp.max(s, axis=-1, keepdims=True))
        alpha = jnp.exp2(m_prev - m_new)
        p = jnp.exp2(s - m_new)
        l_sc[...] = alpha * l_sc[...] + jnp.sum(p, axis=-1, keepdims=True)
        acc_sc[...] = alpha * acc_sc[...] + _dot(p.astype(BF16), v_ref[0, pl.ds(start, tq), :])
        m_sc[...] = m_new

    def below_diagonal(ki, carry):
        block(ki, False)
        return carry

    lax.fori_loop(0, qi, below_diagonal, 0)
    block(qi, True)
    o_ref[...] = (acc_sc[...] / l_sc[...]).astype(o_ref.dtype)


def _flash_outer_kernel(q_ref, k_ref, v_ref, o_ref, m_sc, l_sc, acc_sc, *, tq, nb):
    @pl.when(pl.program_id(0) < nb)
    def _rows():
        _flash_kernel(q_ref, k_ref, v_ref, o_ref, m_sc, l_sc, acc_sc, tq=tq)

    @pl.when(pl.program_id(0) == nb)
    def _tail():
        o_ref[...] = jnp.zeros_like(o_ref)


def _flash_call(q_hm, k_hm, v_hm, nb, t, m_total, tq):
    nh, _, qd = q_hm.shape
    vd = v_hm.shape[2]
    nq = t // tq
    last = m_total // tq - 1
    assert nb * nq + nq > last
    bc = lambda b: jnp.minimum(b, nb - 1)
    hc = lambda b, h: jnp.where(b < nb, h, nh - 1)
    return pl.pallas_call(
        functools.partial(_flash_outer_kernel, tq=tq, nb=nb),
        grid=(nb + 1, nh, nq),
        in_specs=[pl.BlockSpec((1, tq, qd), lambda b, h, i: (hc(b, h), bc(b) * nq + jnp.where(b < nb, i, nq - 1), 0)),
                  pl.BlockSpec((1, t, qd), lambda b, h, i: (hc(b, h), bc(b), 0)),
                  pl.BlockSpec((1, t, vd), lambda b, h, i: (hc(b, h), bc(b), 0))],
        out_specs=pl.BlockSpec((tq, vd), lambda b, h, i: (jnp.minimum(b * nq + i, last), h)),
        out_shape=jax.ShapeDtypeStruct((m_total, nh * vd), BF16),
        scratch_shapes=[pltpu.VMEM((tq, 1), F32), pltpu.VMEM((tq, 1), F32), pltpu.VMEM((tq, vd), F32)],
        compiler_params=_cparams(("arbitrary", "arbitrary", "arbitrary")),
        name="flash",
    )(q_hm, k_hm, v_hm)


def _qlat_kernel(q_ref, w_ref, o_ref):
    o_ref[0] = _dot(q_ref[0], w_ref[0])


def _qlat_call(q_hm, w_ukt, ms, row_block0):
    nh, nope, lora = w_ukt.shape
    return pl.pallas_call(
        _qlat_kernel,
        grid=(nh,),
        in_specs=[pl.BlockSpec((1, ms, nope), lambda h: (h, row_block0, 0)),
                  pl.BlockSpec((1, nope, lora), lambda h: (h, 0, 0))],
        out_specs=pl.BlockSpec((1, ms, lora), lambda h: (h, 0, 0)),
        out_shape=jax.ShapeDtypeStruct((nh, ms, lora), F32),
        compiler_params=_cparams(("arbitrary",)),
        name="qlat",
    )(q_hm, w_ukt)


def _decode_kernel(pt_ref, ql_ref, qr_ref, cn_ref, kn_ref, ck_hbm, kr_hbm, o_ref,
                   ck_buf, kr_buf, sem, qlb_sc, qrb_sc, kb_sc, rb_sc, m_sc, l_sc, acc_sc, *, npg, sq, ts, rope, n_pages):
    s_blk = pl.program_id(0)
    j = pl.program_id(1)
    nsteps = pl.num_programs(1)
    g = s_blk * nsteps + j
    slot = lax.rem(g, 2)
    nh, _, lora = ql_ref.shape
    rows = nh * ts
    pg = ck_buf.shape[2]

    def page_copies(sb, jj, sl):
        out = []
        for a in range(sq):
            for i in range(npg):
                page = pt_ref[(sb * sq + a) * n_pages + jj * npg + i]
                out.append(pltpu.make_async_copy(ck_hbm.at[page], ck_buf.at[sl, a * npg + i], sem.at[0, sl]))
                out.append(pltpu.make_async_copy(kr_hbm.at[page], kr_buf.at[sl, a * npg + i], sem.at[1, sl]))
        return out

    @pl.when(g == 0)
    def _prime():
        for c in page_copies(s_blk, j, slot):
            c.start()

    @pl.when(g + 1 < pl.num_programs(0) * nsteps)
    def _start_next():
        wrap = j == nsteps - 1
        for c in page_copies(jnp.where(wrap, s_blk + 1, s_blk), jnp.where(wrap, 0, j + 1), 1 - slot):
            c.start()

    for c in page_copies(s_blk, j, slot):
        c.wait()
    ck_refs = [ck_buf.at[slot, i] for i in range(sq * npg)]
    kr_refs = [kr_buf.at[slot, i] for i in range(sq * npg)]

    @pl.when(j == 0)
    def _init():
        for a in range(sq):
            qlb_sc[a] = ql_ref[:, a * ts:(a + 1) * ts, :].reshape(rows, lora).astype(BF16)
            qrb_sc[a] = qr_ref[:, a * ts:(a + 1) * ts, :].reshape(rows, rope).astype(BF16)
        m_sc[...] = jnp.full_like(m_sc, NEG_BIG)
        l_sc[...] = jnp.zeros_like(l_sc)
        acc_sc[...] = jnp.zeros_like(acc_sc)

    def update(a, s, kb):
        m_prev = m_sc[a]
        m_new = jnp.maximum(m_prev, jnp.max(s, axis=-1, keepdims=True))
        alpha = jnp.exp2(m_prev - m_new)
        p = jnp.exp2(s - m_new)
        l_sc[a] = alpha * l_sc[a] + jnp.sum(p, axis=-1, keepdims=True)
        acc_sc[a] = alpha * acc_sc[a] + _dot(p.astype(BF16), kb)
        m_sc[a] = m_new

    for a in range(sq):
        for i in range(npg):
            kb_sc[a, i * pg:(i + 1) * pg, :] = ck_refs[a * npg + i][...].astype(BF16)
            rb_sc[a, :, i * pg:(i + 1) * pg] = kr_refs[a * npg + i][...].astype(BF16)
    for a in range(sq):
        kb = kb_sc[a]
        update(a, _dot_nt(qlb_sc[a], kb) + _dot(qrb_sc[a], rb_sc[a]), kb)

    @pl.when(j == pl.num_programs(1) - 1)
    def _fin():
        for a in range(sq):
            kn = jnp.concatenate([cn_ref[a * ts:(a + 1) * ts, :], jnp.zeros((pg - ts, lora), F32)], axis=0).astype(BF16)
            rn = jnp.concatenate([kn_ref[a * ts:(a + 1) * ts, :], jnp.zeros((pg - ts, rope), F32)], axis=0).astype(BF16)
            s = _dot_nt(qlb_sc[a], kn) + _dot_nt(qrb_sc[a], rn)
            tq = lax.broadcasted_iota(jnp.int32, s.shape, 0) % ts
            tk = lax.broadcasted_iota(jnp.int32, s.shape, 1)
            update(a, jnp.where(tk <= tq, s, NEG_BIG), kn)
            o = acc_sc[a] / l_sc[a]
            o_ref[:, a * ts:(a + 1) * ts, :] = o.reshape(nh, ts, lora)


def _decode_call(page_table, q_lat, q_rope, ckv_all, kr_all, cache_ckv, cache_krope_t, ns, ts, row_block0, npg, sq):
    nh, ms, lora = q_lat.shape
    rope = q_rope.shape[2]
    n_pages = page_table.shape[1]
    pg = cache_ckv.shape[1]
    nsteps = n_pages // npg
    rows = nh * ts

    n = sq * npg
    grid_spec = pltpu.PrefetchScalarGridSpec(
        num_scalar_prefetch=1,
        grid=(ns // sq, nsteps),
        in_specs=[pl.BlockSpec((nh, sq * ts, lora), lambda s, j, pt: (0, s, 0)),
                  pl.BlockSpec((nh, sq * ts, rope), lambda s, j, pt: (0, s, 0)),
                  pl.BlockSpec((sq * ts, lora), lambda s, j, pt: (row_block0 + s, 0)),
                  pl.BlockSpec((sq * ts, rope), lambda s, j, pt: (row_block0 + s, 0)),
                  pl.BlockSpec(memory_space=pl.ANY),
                  pl.BlockSpec(memory_space=pl.ANY)],
        out_specs=pl.BlockSpec((nh, sq * ts, lora), lambda s, j, pt: (0, s, 0)),
        scratch_shapes=[pltpu.VMEM((2, n, pg, lora), F32), pltpu.VMEM((2, n, rope, pg), F32),
                        pltpu.SemaphoreType.DMA((2, 2)),
                        pltpu.VMEM((sq, rows, lora), BF16), pltpu.VMEM((sq, rows, rope), BF16),
                        pltpu.VMEM((sq, npg * pg, lora), BF16), pltpu.VMEM((sq, rope, npg * pg), BF16),
                        pltpu.VMEM((sq, rows, 1), F32), pltpu.VMEM((sq, rows, 1), F32),
                        pltpu.VMEM((sq, rows, lora), F32)],
    )
    return pl.pallas_call(
        functools.partial(_decode_kernel, npg=npg, sq=sq, ts=ts, rope=rope, n_pages=n_pages),
        grid_spec=grid_spec,
        out_shape=jax.ShapeDtypeStruct((nh, ms, lora), F32),
        compiler_params=_cparams(("arbitrary", "arbitrary")),
        name="decode",
    )(page_table.reshape(-1), q_lat, q_rope, ckv_all, kr_all, cache_ckv, cache_krope_t)


def _vup_kernel(o_ref, w_ref, att_any, out_ref):
    del att_any
    out_ref[...] = _dot(o_ref[0].astype(BF16), w_ref[0]).astype(out_ref.dtype)


def _vup_call(o_lat, w_uv3, att_all, row_block0):
    nh, ms, lora = o_lat.shape
    vd = w_uv3.shape[2]
    return pl.pallas_call(
        _vup_kernel,
        grid=(nh,),
        in_specs=[pl.BlockSpec((1, ms, lora), lambda h: (h, 0, 0)),
                  pl.BlockSpec((1, lora, vd), lambda h: (h, 0, 0)),
                  pl.BlockSpec(memory_space=pl.ANY)],
        out_specs=pl.BlockSpec((ms, vd), lambda h: (row_block0, h)),
        out_shape=jax.ShapeDtypeStruct(att_all.shape, att_all.dtype),
        input_output_aliases={2: 0},
        compiler_params=_cparams(("arbitrary",)),
        name="vup",
    )(o_lat, w_uv3, att_all)


def _glu_kernel(g_ref, wa_ref, wb_ref, z_ref, o_ref):
    g = g_ref[...]
    ga = _dot(g, wa_ref[...])
    gb = _dot(g, wb_ref[...])
    o_ref[...] = (ga * _sigmoid(gb) * z_ref[...].astype(F32)).astype(o_ref.dtype)


def _glu_call(gy, w_glu, zazs, tm, tn):
    m, k = gy.shape
    n = w_glu.shape[1] // 2
    nn = n // tn
    zs_blk0 = (zazs.shape[1] - n) // tn
    return pl.pallas_call(
        _glu_kernel,
        grid=(m // tm, nn),
        in_specs=[pl.BlockSpec((tm, k), lambda i, j: (i, 0)),
                  pl.BlockSpec((k, tn), lambda i, j: (0, j)),
                  pl.BlockSpec((k, tn), lambda i, j: (0, nn + j)),
                  pl.BlockSpec((tm, tn), lambda i, j: (i, zs_blk0 + j))],
        out_specs=pl.BlockSpec((tm, tn), lambda i, j: (i, j)),
        out_shape=jax.ShapeDtypeStruct((m, n), BF16),
        compiler_params=_cparams(("arbitrary", "arbitrary")),
        name="glu",
    )(gy, w_glu, w_glu, zazs)


def _merge_kernel(ys_ref, att_ref, za_ref, wbs_ref, wo_ref, gs_ref, ga_ref, o_ref, v_sc):
    @pl.when(pl.program_id(1) == 0)
    def _gate():
        v_sc[...] = (att_ref[...].astype(F32) * za_ref[...].astype(F32)).astype(BF16)

    p_s = _dot(ys_ref[...], wbs_ref[...])
    p_a = _dot(v_sc[...], wo_ref[...])
    o_ref[...] = (gs_ref[...].astype(F32) * p_s + ga_ref[...].astype(F32) * p_a).astype(o_ref.dtype)


def _merge_call(ys2, att, zazs, w_bs, w_o, gates, tm, tn):
    m, ks = ys2.shape
    ka = att.shape[1]
    n = w_bs.shape[1]
    nn = n // tn
    return pl.pallas_call(
        _merge_kernel,
        grid=(m // tm, nn),
        in_specs=[pl.BlockSpec((tm, ks), lambda i, j: (i, 0)),
                  pl.BlockSpec((tm, ka), lambda i, j: (i, 0)),
                  pl.BlockSpec((tm, ka), lambda i, j: (i, 0)),
                  pl.BlockSpec((ks, tn), lambda i, j: (0, j)),
                  pl.BlockSpec((ka, tn), lambda i, j: (0, j)),
                  pl.BlockSpec((tm, tn), lambda i, j: (i, j)),
                  pl.BlockSpec((tm, tn), lambda i, j: (i, nn + j))],
        out_specs=pl.BlockSpec((tm, tn), lambda i, j: (i, j)),
        out_shape=jax.ShapeDtypeStruct((m, n), BF16),
        scratch_shapes=[pltpu.VMEM((tm, ka), BF16)],
        compiler_params=_cparams(("arbitrary", "arbitrary")),
        name="merge",
    )(ys2, att, zazs, w_bs, w_o, gates, gates)


def _final_kernel(mg_ref, w_ref, x_ref, gate_ref, g_ref, b_ref, o_ref, *, alpha):
    out = _dot(mg_ref[...], w_ref[...])
    x = x_ref[...]
    y = alpha * x + gate_ref[...] * out.reshape(x.shape)
    mu = jnp.mean(y, axis=-1, keepdims=True)
    yc = y - mu
    var = jnp.mean(yc * yc, axis=-1, keepdims=True)
    o_ref[...] = yc * lax.rsqrt(var + LN_EPS) * g_ref[...] + b_ref[...]


def _final_call(merged, w_out, x, mod3, ln_g, ln_b, alpha, x_blk, gate_blk, gate_map, merged_map, grid):
    d = w_out.shape[0]
    rows = x_blk[0] * x_blk[1]
    nd = len(grid)
    x_map = (lambda i, j: (i, j, 0)) if nd == 2 else (lambda i: (i, 0, 0))
    const2 = (lambda i, j: (0, 0)) if nd == 2 else (lambda i: (0, 0))
    const3 = (lambda i, j: (0, 0, 0)) if nd == 2 else (lambda i: (0, 0, 0))
    return pl.pallas_call(
        functools.partial(_final_kernel, alpha=alpha),
        grid=grid,
        in_specs=[pl.BlockSpec((rows, d), merged_map),
                  pl.BlockSpec((d, d), const2),
                  pl.BlockSpec(x_blk, x_map),
                  pl.BlockSpec(gate_blk, gate_map),
                  pl.BlockSpec((1, 1, d), const3),
                  pl.BlockSpec((1, 1, d), const3)],
        out_specs=pl.BlockSpec(x_blk, x_map),
        out_shape=jax.ShapeDtypeStruct(x.shape, F32),
        compiler_params=_cparams(("arbitrary",) * nd),
        name="final",
    )(merged, w_out, x, mod3, ln_g.reshape(1, 1, d), ln_b.reshape(1, 1, d))


def _rope_tables(pos, rope):
    freqs = ROPE_BASE ** (-jnp.arange(0, rope, 2, dtype=F32) / rope)
    ang = pos.astype(F32)[:, None] * freqs[None, :]
    cos, sin = jnp.cos(ang), jnp.sin(ang)
    return jnp.concatenate([cos, cos, sin, sin], axis=-1)


def _rot_rows(wt, rope):
    half = rope // 2
    return jnp.concatenate([-wt[..., half:, :], wt[..., :half, :]], axis=-2)


def _block_diag(x, gblk):
    g, r, c = x.shape
    x4 = x.reshape(g // gblk, gblk, r, c)
    eye = jnp.eye(gblk, dtype=x.dtype)
    return jnp.einsum("bgrc,gh->bgrhc", x4, eye).reshape(g // gblk, gblk * r, gblk * c)


def kernel(x_prompt, x_sample, c_prompt, c_sample, cache_ckv, cache_krope, state_ssm_re, state_ssm_im, page_table, w_ada, b_ada, w_in, g_kv, w_uk, w_uv, w_o, lam_re, lam_im, log_dt, b_re, b_im, c_re, c_im, d_skip, w_glu, w_bs, w_out, ln_g, ln_b):
    nb, t, d = x_prompt.shape
    ns, ts, _ = x_sample.shape
    depth = w_in.shape[0]
    assert depth == 1, "single-layer step"
    lora, nh, nope = w_uk.shape[1:]
    vd = w_uv.shape[3]
    rope = cache_krope.shape[3]
    d_ssm = d_skip.shape[1]
    ngrp, nstate, gch = b_re.shape[1:]
    d_attn = nh * vd
    pg = cache_ckv.shape[2]
    past = page_table.shape[1] * pg
    mp, ms = nb * t, ns * ts
    m = mp + ms
    scale = float((nope + rope) ** -0.5) * LOG2E
    alpha = float((2 * depth) ** 0.25)

    tm = math.gcd(math.gcd(mp, ms), 1024)
    tmh = min(tm, 512)
    tn = 512
    assert rope * 2 == 128 and nope == 128 and vd == 128

    wt = jnp.swapaxes(w_in, 1, 2)[0]
    o_u, o_zs, o_q = 0, d_ssm, 2 * d_ssm
    o_ckv = o_q + nh * (nope + rope)
    o_kr = o_ckv + lora
    o_za = o_kr + rope
    o_gs = o_za + d_attn
    o_ga = o_gs + d
    w_u = wt[o_u:o_zs].astype(BF16)
    w_z = jnp.concatenate([wt[o_za:o_gs], wt[o_zs:o_q]], axis=0).astype(BF16)
    w_g = wt[o_gs:].astype(BF16)
    wq = wt[o_q:o_ckv].reshape(nh, nope + rope, d)
    w_q3 = jnp.concatenate([wq, _rot_rows(wq[:, nope:, :], rope)], axis=1).astype(BF16)
    wkr = wt[o_kr:o_za]
    w_ck = jnp.concatenate([wt[o_ckv:o_kr], wkr, _rot_rows(wkr, rope)], axis=0).astype(BF16)
    w_ukv = jnp.concatenate([w_uk[0], w_uv[0]], axis=-1).transpose(1, 0, 2).astype(BF16)
    w_ukt = w_uk[0].transpose(1, 2, 0).astype(BF16)
    w_uv3 = w_uv[0].transpose(1, 0, 2).astype(BF16)
    w_glu_b = w_glu[0].astype(BF16)
    w_bs_b = w_bs[0].astype(BF16)
    w_o_b = w_o[0].astype(BF16)
    w_out_b = w_out[0].astype(BF16)

    pos = jnp.concatenate([jnp.tile(jnp.arange(t), nb), jnp.tile(past + jnp.arange(ts), ns)])
    cs = _rope_tables(pos, rope)

    c_all = jnp.concatenate([c_sample, c_prompt], axis=0)
    pad = (-c_all.shape[0]) % 8
    c_all = jnp.concatenate([c_all, jnp.zeros((pad, d), F32)], axis=0)
    mod = _mod_call(c_all, w_ada[0], b_ada[0], tn)
    mod3 = mod.reshape(mod.shape[0], 1, 3 * d)

    sb = max(1, min(ns, tmh // ts))
    h_all = _hmod_prompt(x_prompt, mod3, ns, m, tmh)
    h_all = _hmod_sample(x_sample, mod3, h_all, mp // (sb * ts), sb)

    u_all = _proj_call(h_all, w_u, None, F32, tm, tn, "proj_u")
    zazs = _proj_call(h_all, w_z, "silu", BF16, tm, tn, "proj_z")
    gates = _proj_call(h_all, w_g, "sigmoid", BF16, tm, tn, "proj_g")
    q_hm = _qproj_call(h_all, w_q3, cs, tm, nope, rope, scale)
    ckv_all, kr_all, ckv_b, kr_b = _ckv_call(h_all, w_ck, g_kv[0], cs, tmh, lora, rope)

    gblk = SSM_GROUP_BLOCK
    a_re, a_im, bb_re, bb_im = _s5_prep_call(lam_re[0], lam_im[0], log_dt[0],
                                             b_re[0].transpose(0, 2, 1), b_im[0].transpose(0, 2, 1))
    bcat = jnp.concatenate([_block_diag(bb_re, gblk), _block_diag(bb_im, gblk)], axis=-1)
    b2 = jnp.concatenate([bcat, bcat], axis=1).astype(BF16)
    ccat = jnp.concatenate([_block_diag(c_re[0].transpose(0, 2, 1), gblk),
                            _block_diag(-c_im[0].transpose(0, 2, 1), gblk)], axis=1).astype(BF16)
    ngb = ngrp // gblk
    sw = gblk * nstate
    amat = jnp.stack([a_re.reshape(ngb, sw), a_im.reshape(ngb, sw)], axis=1)
    s5w = (b2, ccat, amat, d_skip)
    tc = min(t, S5_TIME_CHUNK)
    gy_p, sre_p, sim_p = _s5_prompt_call(u_all, s5w, nb, t, tc)
    ssb = min(ns, 32)
    gy_s, sre_s, sim_s = _s5_sample_call(u_all, s5w, state_ssm_re[0].reshape(ns, ngrp * nstate),
                                         state_ssm_im[0].reshape(ns, ngrp * nstate), ns, ts, mp // (ssb * ts), ssb)
    gy = jnp.concatenate([gy_p.reshape(mp, d_ssm), gy_s], axis=0)

    k_hm, v_hm = _kvup_call(ckv_b, kr_b, w_ukv, mp, tm, nope)
    tq = min(t, FLASH_BLOCK)
    att = _flash_call(q_hm, k_hm, v_hm, nb, t, m, tq)
    q_lat = _qlat_call(q_hm, w_ukt, ms, mp // ms)
    q_rope_s = q_hm[:, mp:, nope:nope + rope].astype(F32)
    sq = DECODE_SEQS_PER_STEP
    krope_t = jnp.swapaxes(cache_krope, 2, 3).reshape(cache_krope.shape[1], rope, pg)
    o_lat = _decode_call(page_table, q_lat, q_rope_s, ckv_all, kr_all, cache_ckv.reshape(cache_ckv.shape[1:]), krope_t,
                         ns, ts, mp // (sq * ts), PAGES_PER_STEP, sq)
    att = _vup_call(o_lat, w_uv3, att, mp // ms)

    ys2 = _glu_call(gy, w_glu_b, zazs, tm, tn)
    merged = _merge_call(ys2, att, zazs, w_bs_b, w_o_b, gates, tmh, tn)
    ntp = t // tmh
    y_p = _final_call(merged, w_out_b, x_prompt, mod3, ln_g[0], ln_b[0], alpha,
                      (1, tmh, d), (1, 1, d), lambda i, j: (ns + i, 0, 2), lambda i, j: (i * ntp + j, 0), (nb, ntp))
    y_s = _final_call(merged, w_out_b, x_sample, mod3, ln_g[0], ln_b[0], alpha,
                      (sb, ts, d), (sb, 1, d), lambda i: (i, 0, 2), lambda i: (mp // (sb * ts) + i, 0), (ns // sb,))

    st = lambda a, n: a.reshape(1, n, ngrp, nstate)
    return (y_p, y_s,
            ckv_all[:mp].reshape(1, nb, t, lora), kr_all[:mp].reshape(1, nb, t, rope), st(sre_p, nb), st(sim_p, nb),
            ckv_all[mp:].reshape(1, ns, ts, lora), kr_all[mp:].reshape(1, ns, ts, rope), st(sre_s, ns), st(sim_s, ns))
```

```python
import functools
import math

import jax
import jax.numpy as jnp
from jax import lax
from jax.experimental import pallas as pl
from jax.experimental.pallas import tpu as pltpu

F32 = jnp.float32
BF16 = jnp.bfloat16

LN_EPS = 1e-5
RMS_EPS = 1e-6
ROPE_BASE = 10000.0
NEG_BIG = -1e30
VMEM_LIMIT = 48 * 1024 * 1024
PAGES_PER_STEP = 16
DECODE_SEQS_PER_STEP = 2
DECODE_SLOTS = 3
LOG2E = 1.4426950408889634
SSM_GROUP_BLOCK = 8
LANES = 128
S5_TIME_CHUNK = 256
FLASH_BLOCK = 1024


def _cparams(sem):
    return pltpu.CompilerParams(dimension_semantics=sem, vmem_limit_bytes=VMEM_LIMIT)


def _sigmoid(x):
    return 1.0 / (1.0 + jnp.exp(-x))


def _silu(x):
    return x * _sigmoid(x)


def _dot(a, b):
    return jnp.dot(a, b, preferred_element_type=F32)


def _dot_nt(a, b):
    return lax.dot_general(a, b, (((1,), (1,)), ((), ())), preferred_element_type=F32)


def _mod_kernel(c_ref, w_ref, b_ref, o_ref):
    a = _silu(c_ref[...]).astype(BF16)
    o_ref[...] = _dot(a, w_ref[...].astype(BF16)) + b_ref[...]


def _mod_call(c_all, w_ada, b_ada, tn):
    r, d = c_all.shape
    n = w_ada.shape[1]
    return pl.pallas_call(
        _mod_kernel,
        grid=(n // tn,),
        in_specs=[pl.BlockSpec((r, d), lambda j: (0, 0)),
                  pl.BlockSpec((d, tn), lambda j: (0, j)),
                  pl.BlockSpec((1, tn), lambda j: (0, j))],
        out_specs=pl.BlockSpec((r, tn), lambda j: (0, j)),
        out_shape=jax.ShapeDtypeStruct((r, n), F32),
        compiler_params=_cparams(("arbitrary",)),
        name="mod",
    )(c_all, w_ada, b_ada.reshape(1, n))


def _hmod_kernel(x_ref, sh_ref, sc_ref, *rest):
    o_ref = rest[-1]
    h = x_ref[...] * (1.0 + sc_ref[...]) + sh_ref[...]
    o_ref[...] = h.reshape(o_ref.shape).astype(o_ref.dtype)


def _hmod_prompt_kernel(x_ref, sh_ref, sc_ref, o_ref, *, nb):
    @pl.when(pl.program_id(0) < nb)
    def _rows():
        _hmod_kernel(x_ref, sh_ref, sc_ref, o_ref)

    @pl.when(pl.program_id(0) == nb)
    def _tail():
        o_ref[...] = jnp.zeros_like(o_ref)


def _hmod_prompt(x, mod3, mod_row0, m_total, tm):
    b, t, d = x.shape
    nt = t // tm
    last = m_total // tm - 1
    assert b * nt + nt > last
    bc = lambda i: jnp.minimum(i, b - 1)
    return pl.pallas_call(
        functools.partial(_hmod_prompt_kernel, nb=b),
        grid=(b + 1, nt),
        in_specs=[pl.BlockSpec((1, tm, d), lambda i, j: (bc(i), jnp.where(i < b, j, nt - 1), 0)),
                  pl.BlockSpec((1, 1, d), lambda i, j: (mod_row0 + bc(i), 0, 0)),
                  pl.BlockSpec((1, 1, d), lambda i, j: (mod_row0 + bc(i), 0, 1))],
        out_specs=pl.BlockSpec((tm, d), lambda i, j: (jnp.minimum(i * nt + j, last), 0)),
        out_shape=jax.ShapeDtypeStruct((m_total, d), BF16),
        compiler_params=_cparams(("arbitrary", "arbitrary")),
        name="hmod_prompt",
    )(x, mod3, mod3)


def _hmod_sample(x, mod3, h_all, row_block0, sb):
    ns, ts, d = x.shape
    return pl.pallas_call(
        _hmod_kernel,
        grid=(ns // sb,),
        in_specs=[pl.BlockSpec((sb, ts, d), lambda i: (i, 0, 0)),
                  pl.BlockSpec((sb, 1, d), lambda i: (i, 0, 0)),
                  pl.BlockSpec((sb, 1, d), lambda i: (i, 0, 1)),
                  pl.BlockSpec(memory_space=pl.ANY)],
        out_specs=pl.BlockSpec((sb * ts, d), lambda i: (row_block0 + i, 0)),
        out_shape=jax.ShapeDtypeStruct(h_all.shape, h_all.dtype),
        input_output_aliases={3: 0},
        compiler_params=_cparams(("arbitrary",)),
        name="hmod_sample",
    )(x, mod3, mod3, h_all)


def _proj_kernel(h_ref, w_ref, o_ref, *, act):
    acc = _dot_nt(h_ref[...], w_ref[...])
    if act == "silu":
        acc = _silu(acc)
    elif act == "sigmoid":
        acc = _sigmoid(acc)
    o_ref[...] = acc.astype(o_ref.dtype)


def _proj_call(h, w, act, out_dtype, tm, tn, name):
    m, k = h.shape
    n = w.shape[0]
    return pl.pallas_call(
        functools.partial(_proj_kernel, act=act),
        grid=(m // tm, n // tn),
        in_specs=[pl.BlockSpec((tm, k), lambda i, j: (i, 0)),
                  pl.BlockSpec((tn, k), lambda i, j: (j, 0))],
        out_specs=pl.BlockSpec((tm, tn), lambda i, j: (i, j)),
        out_shape=jax.ShapeDtypeStruct((m, n), out_dtype),
        compiler_params=_cparams(("arbitrary", "arbitrary")),
        name=name,
    )(h, w)


def _qproj_kernel(h_ref, w_ref, cs_ref, o_ref, *, nope, rope, scale, hb):
    h = h_ref[...]
    cs = cs_ref[...]
    for hh in range(hb):
        acc = _dot_nt(h, w_ref[hh])
        t = acc[:, nope:] * cs
        r = t + pltpu.roll(t, rope, 1)
        lane = lax.broadcasted_iota(jnp.int32, r.shape, 1)
        r = jnp.where(lane < rope, r, 0.0)
        o_ref[hh, :, :nope] = (acc[:, :nope] * scale).astype(o_ref.dtype)
        o_ref[hh, :, nope:] = (r * scale).astype(o_ref.dtype)


def _qproj_call(h, w_q3, cs, tm, nope, rope, scale, hb=2):
    m, k = h.shape
    nh, wd, _ = w_q3.shape
    return pl.pallas_call(
        functools.partial(_qproj_kernel, nope=nope, rope=rope, scale=scale, hb=hb),
        grid=(m // tm, nh // hb),
        in_specs=[pl.BlockSpec((tm, k), lambda i, j: (i, 0)),
                  pl.BlockSpec((hb, wd, k), lambda i, j: (j, 0, 0)),
                  pl.BlockSpec((tm, 2 * rope), lambda i, j: (i, 0))],
        out_specs=pl.BlockSpec((hb, tm, wd), lambda i, j: (j, i, 0)),
        out_shape=jax.ShapeDtypeStruct((nh, m, wd), BF16),
        compiler_params=_cparams(("arbitrary", "arbitrary")),
        name="qproj",
    )(h, w_q3, cs)


def _ckv_kernel(h_ref, w_ref, g_ref, cs_ref, ckv_ref, kr_ref, ckvb_ref, krb_ref, *, lora, rope):
    acc = _dot_nt(h_ref[...], w_ref[...])
    c = acc[:, :lora]
    ms = jnp.mean(c * c, axis=-1, keepdims=True)
    ckv = c * lax.rsqrt(ms + RMS_EPS) * g_ref[...]
    ckv_ref[...] = ckv
    ckvb_ref[...] = ckv.astype(BF16)
    t = acc[:, lora:] * cs_ref[...]
    r = t + pltpu.roll(t, rope, 1)
    kr_ref[...] = r[:, :rope]
    lane = lax.broadcasted_iota(jnp.int32, r.shape, 1)
    krb_ref[...] = jnp.where(lane < rope, r, 0.0).astype(BF16)


def _ckv_call(h, w_ck, g_kv, cs, tm, lora, rope):
    m, k = h.shape
    wd = w_ck.shape[0]
    return pl.pallas_call(
        functools.partial(_ckv_kernel, lora=lora, rope=rope),
        grid=(m // tm,),
        in_specs=[pl.BlockSpec((tm, k), lambda i: (i, 0)),
                  pl.BlockSpec((wd, k), lambda i: (0, 0)),
                  pl.BlockSpec((1, lora), lambda i: (0, 0)),
                  pl.BlockSpec((tm, 2 * rope), lambda i: (i, 0))],
        out_specs=[pl.BlockSpec((tm, lora), lambda i: (i, 0)),
                   pl.BlockSpec((tm, rope), lambda i: (i, 0)),
                   pl.BlockSpec((tm, lora), lambda i: (i, 0)),
                   pl.BlockSpec((tm, 2 * rope), lambda i: (i, 0))],
        out_shape=[jax.ShapeDtypeStruct((m, lora), F32),
                   jax.ShapeDtypeStruct((m, rope), F32),
                   jax.ShapeDtypeStruct((m, lora), BF16),
                   jax.ShapeDtypeStruct((m, 2 * rope), BF16)],
        compiler_params=_cparams(("arbitrary",)),
        name="ckvproj",
    )(h, w_ck, g_kv.reshape(1, lora), cs)


def _s5_prep_kernel(lre_ref, lim_ref, ldt_ref, bre_ref, bim_ref, are_ref, aim_ref, bbre_ref, bbim_ref):
    lre = lre_ref[...]
    lim = lim_ref[...]
    dt = jnp.exp(ldt_ref[...])
    mag = jnp.exp(lre * dt)
    a_re = mag * jnp.cos(lim * dt)
    a_im = mag * jnp.sin(lim * dt)
    are_ref[...] = a_re
    aim_ref[...] = a_im
    den = lre * lre + lim * lim
    nr = a_re - 1.0
    c_re = (nr * lre + a_im * lim) / den
    c_im = (a_im * lre - nr * lim) / den
    bre = bre_ref[...]
    bim = bim_ref[...]
    bbre_ref[...] = c_re * bre - c_im * bim
    bbim_ref[...] = c_re * bim + c_im * bre


def _s5_prep_call(lam_re, lam_im, log_dt, b_re_t, b_im_t):
    g, p = lam_re.shape
    n = b_re_t.shape[1]
    a_re, a_im, bb_re, bb_im = pl.pallas_call(
        _s5_prep_kernel,
        out_shape=[jax.ShapeDtypeStruct((g, 1, p), F32), jax.ShapeDtypeStruct((g, 1, p), F32),
                   jax.ShapeDtypeStruct((g, n, p), F32), jax.ShapeDtypeStruct((g, n, p), F32)],
        name="s5prep",
    )(lam_re.reshape(g, 1, p), lam_im.reshape(g, 1, p), log_dt.reshape(g, 1, 1), b_re_t, b_im_t)
    return a_re.reshape(g, p), a_im.reshape(g, p), bb_re, bb_im


def _s5_kernel(*refs, n_u, tc, nseq, sw, has_h0):
    u_refs = refs[:n_u]
    i = n_u
    b2_ref, c_ref, a_ref, d_ref = refs[i:i + 4]
    i += 4
    if has_h0:
        h0re_ref, h0im_ref = refs[i:i + 2]
        i += 2
    gy_ref, sre_ref, sim_ref = refs[i:i + 3]
    i += 3
    bu_sc, hs_sc, y_sc, st_sc = refs[i:i + 4]
    assert tc % 8 == 0 and (nseq <= 8 or nseq % 8 == 0)

    step = pl.program_id(1)

    @pl.when(step == 0)
    def _init():
        if has_h0:
            st_sc[:, :sw] = h0re_ref[...]
            st_sc[:, sw:] = h0im_ref[...]
        else:
            st_sc[...] = jnp.zeros_like(st_sc)

    def u_of(q):
        return u_refs[q][...] if n_u > 1 else u_refs[0][q * tc:(q + 1) * tc, :]

    u = u_refs[0][...] if n_u == 1 else jnp.concatenate([r[...] for r in u_refs], axis=0)
    u_hi = u.astype(BF16)
    u_lo = (u - u_hi.astype(F32)).astype(BF16)
    bu = _dot(jnp.concatenate([u_hi, u_lo], axis=1), b2_ref[0])
    nj = sw // LANES
    for j in range(2 * nj):
        bu_sc[j] = bu[:, j * LANES:(j + 1) * LANES]

    for sg in range(0, nseq, 8):
        n8 = min(8, nseq - sg)
        base = sg * tc
        a_b = [jnp.broadcast_to(a_ref[0, r:r + 1, j * LANES:(j + 1) * LANES], (n8, LANES))
               for r in range(2) for j in range(nj)]

        def body(i8, carry, base=base, n8=n8, a_b=a_b):
            slab = base // (8 * n8) + i8
            h = list(carry)
            for k in range(8):
                rows = pl.ds(base + i8 * 8 + k, n8, stride=tc)
                new = [None] * (2 * nj)
                for j in range(nj):
                    new[j] = a_b[j] * h[j] - a_b[nj + j] * h[nj + j] + bu_sc[j, rows, :]
                    new[nj + j] = a_b[j] * h[nj + j] + a_b[nj + j] * h[j] + bu_sc[nj + j, rows, :]
                for j in range(2 * nj):
                    hs_sc[j, slab, k * n8:(k + 1) * n8, :] = new[j]
                h = new
            return tuple(h)

        init = tuple(st_sc[sg:sg + n8, j * LANES:(j + 1) * LANES] for j in range(2 * nj))
        fin = lax.fori_loop(0, tc // 8, body, init)
        for j in range(2 * nj):
            st_sc[sg:sg + n8, j * LANES:(j + 1) * LANES] = fin[j]

    hs = jnp.concatenate([hs_sc[j].reshape(nseq * tc, LANES) for j in range(2 * nj)], axis=1)
    yp = _dot(hs.astype(BF16), c_ref[0])
    nyb = yp.shape[1] // LANES
    for jb in range(nyb):
        y_sc[jb] = yp[:, jb * LANES:(jb + 1) * LANES]
    d = d_ref[...]
    for q in range(nseq):
        sg = (q // 8) * 8
        n8 = min(8, nseq - sg)
        rows = pl.ds(sg * tc + (q - sg), tc, stride=n8)
        y = jnp.concatenate([y_sc[jb, rows, :] for jb in range(nyb)], axis=1) + d * u_of(q)
        gy = (0.5 * y * (1.0 + lax.erf(y * (1.0 / math.sqrt(2.0))))).astype(gy_ref.dtype)
        if len(gy_ref.shape) == 3:
            gy_ref[q] = gy
        else:
            gy_ref[q * tc:(q + 1) * tc, :] = gy

    @pl.when(step == pl.num_programs(1) - 1)
    def _fin():
        sre_ref[...] = st_sc[:, :sw]
        sim_ref[...] = st_sc[:, sw:]


def _s5_scratch(nseq, tc, gk, sw):
    n8 = min(nseq, 8)
    nlb = 2 * sw // LANES
    return [pltpu.VMEM((nlb, nseq * tc, LANES), F32),
            pltpu.VMEM((nlb, nseq * tc // (8 * n8), 8 * n8, LANES), F32),
            pltpu.VMEM((gk // LANES, nseq * tc, LANES), F32),
            pltpu.VMEM((nseq, 2 * sw), F32)]


def _s5_weight_specs(gk, sw, idx):
    return [pl.BlockSpec((1, 2 * gk, 2 * sw), lambda *a: (idx(*a), 0, 0)),
            pl.BlockSpec((1, 2 * sw, gk), lambda *a: (idx(*a), 0, 0)),
            pl.BlockSpec((1, 2, sw), lambda *a: (idx(*a), 0, 0)),
            pl.BlockSpec((1, gk), lambda *a: (0, idx(*a)))]


def _s5_prompt_call(u_all, wts, nb, t, tc):
    b2, cmat, amat, dsk = wts
    ngb, gk2, sw2 = b2.shape
    gk, sw = gk2 // 2, sw2 // 2
    ntc = t // tc
    gb_of = lambda g, s: g
    u_specs = [pl.BlockSpec((tc, gk), functools.partial(lambda g, s, b: (b * ntc + s, g), b=b)) for b in range(nb)]
    kern = functools.partial(_s5_kernel, n_u=nb, tc=tc, nseq=nb, sw=sw, has_h0=False)
    return pl.pallas_call(
        kern,
        grid=(ngb, ntc),
        in_specs=u_specs + _s5_weight_specs(gk, sw, gb_of),
        out_specs=[pl.BlockSpec((nb, tc, gk), lambda g, s: (0, s, g)),
                   pl.BlockSpec((nb, sw), lambda g, s: (0, g)),
                   pl.BlockSpec((nb, sw), lambda g, s: (0, g))],
        out_shape=[jax.ShapeDtypeStruct((nb, t, ngb * gk), BF16),
                   jax.ShapeDtypeStruct((nb, ngb * sw), F32),
                   jax.ShapeDtypeStruct((nb, ngb * sw), F32)],
        scratch_shapes=_s5_scratch(nb, tc, gk, sw),
        compiler_params=_cparams(("arbitrary", "arbitrary")),
        name="s5_prompt",
    )(*([u_all] * nb), b2, cmat, amat, dsk)


def _s5_sample_call(u_all, wts, h0_re, h0_im, ns, ts, row_block0, sb):
    b2, cmat, amat, dsk = wts
    ngb, gk2, sw2 = b2.shape
    gk, sw = gk2 // 2, sw2 // 2
    gb_of = lambda g, s, z: g
    kern = functools.partial(_s5_kernel, n_u=1, tc=ts, nseq=sb, sw=sw, has_h0=True)
    return pl.pallas_call(
        kern,
        grid=(ngb, 1, ns // sb),
        in_specs=[pl.BlockSpec((sb * ts, gk), lambda g, s, z: (row_block0 + z, g))]
        + _s5_weight_specs(gk, sw, gb_of)
        + [pl.BlockSpec((sb, sw), lambda g, s, z: (z, g)), pl.BlockSpec((sb, sw), lambda g, s, z: (z, g))],
        out_specs=[pl.BlockSpec((sb * ts, gk), lambda g, s, z: (z, g)),
                   pl.BlockSpec((sb, sw), lambda g, s, z: (z, g)),
                   pl.BlockSpec((sb, sw), lambda g, s, z: (z, g))],
        out_shape=[jax.ShapeDtypeStruct((ns * ts, ngb * gk), BF16),
                   jax.ShapeDtypeStruct((ns, ngb * sw), F32),
                   jax.ShapeDtypeStruct((ns, ngb * sw), F32)],
        scratch_shapes=_s5_scratch(sb, ts, gk, sw),
        compiler_params=_cparams(("arbitrary", "arbitrary", "arbitrary")),
        name="s5_sample",
    )(u_all, b2, cmat, amat, dsk, h0_re, h0_im)


def _kvup_kernel(c_ref, kr_ref, wk_ref, wvt_ref, k_ref, vt_ref, *, nope, hb):
    c = c_ref[...]
    for h in range(hb):
        k_ref[h, :, :nope] = _dot(c, wk_ref[h]).astype(k_ref.dtype)
        k_ref[h, :, nope:] = kr_ref[...]
        vt_ref[h] = _dot_nt(wvt_ref[h], c).astype(vt_ref.dtype)


def _kvup_call(ckv_b, kr_b, w_uk3, w_uvt, mp, tm, hb):
    lora = ckv_b.shape[1]
    nh, _, nope = w_uk3.shape
    vd = w_uvt.shape[1]
    rw = kr_b.shape[1]
    return pl.pallas_call(
        functools.partial(_kvup_kernel, nope=nope, hb=hb),
        grid=(mp // tm, nh // hb),
        in_specs=[pl.BlockSpec((tm, lora), lambda i, j: (i, 0)),
                  pl.BlockSpec((tm, rw), lambda i, j: (i, 0)),
                  pl.BlockSpec((hb, lora, nope), lambda i, j: (j, 0, 0)),
                  pl.BlockSpec((hb, vd, lora), lambda i, j: (j, 0, 0))],
        out_specs=[pl.BlockSpec((hb, tm, nope + rw), lambda i, j: (j, i, 0)),
                   pl.BlockSpec((hb, vd, tm), lambda i, j: (j, 0, i))],
        out_shape=[jax.ShapeDtypeStruct((nh, mp, nope + rw), BF16),
                   jax.ShapeDtypeStruct((nh, vd, mp), BF16)],
        compiler_params=_cparams(("arbitrary", "arbitrary")),
        name="kvup",
    )(ckv_b, kr_b, w_uk3, w_uvt)


def _flash_kernel(q_ref, k_ref, vt_ref, o_ref, m_sc, l_sc, acc_sc, *, tq):
    qi = pl.program_id(2)
    q = q_ref[0]
    m_sc[...] = jnp.full_like(m_sc, NEG_BIG)
    l_sc[...] = jnp.zeros_like(l_sc)
    acc_sc[...] = jnp.zeros_like(acc_sc)

    def block(ki, diagonal):
        start = pl.multiple_of(ki * tq, tq)
        s = _dot_nt(k_ref[0, pl.ds(start, tq), :], q)
        if diagonal:
            key = lax.broadcasted_iota(jnp.int32, s.shape, 0)
            qry = lax.broadcasted_iota(jnp.int32, s.shape, 1)
            s = jnp.where(key <= qry, s, NEG_BIG)
        m_prev = m_sc[...]
        m_new = jnp.maximum(m_prev, jnp.max(s, axis=0, keepdims=True))
        alpha = jnp.exp2(m_prev - m_new)
        p = jnp.exp2(s - m_new)
        l_sc[...] = alpha * l_sc[...] + jnp.sum(p, axis=0, keepdims=True)
        acc_sc[...] = alpha * acc_sc[...] + _dot(vt_ref[0, :, pl.ds(start, tq)], p.astype(BF16))
        m_sc[...] = m_new

    def below_diagonal(ki, carry):
        block(ki, False)
        return carry

    lax.fori_loop(0, qi, below_diagonal, 0)
    block(qi, True)
    o_ref[...] = (acc_sc[...] / l_sc[...]).T.astype(o_ref.dtype)


def _flash_outer_kernel(q_ref, k_ref, vt_ref, o_ref, m_sc, l_sc, acc_sc, *, tq, nb):
    @pl.when(pl.program_id(0) < nb)
    def _rows():
        _flash_kernel(q_ref, k_ref, vt_ref, o_ref, m_sc, l_sc, acc_sc, tq=tq)

    @pl.when(pl.program_id(0) == nb)
    def _tail():
        o_ref[...] = jnp.zeros_like(o_ref)


def _flash_call(q_hm, k_hm, vt_hm, nb, t, m_total, tq):
    nh, _, qd = q_hm.shape
    vd = vt_hm.shape[1]
    nq = t // tq
    last = m_total // tq - 1
    assert nb * nq + nq > last
    bc = lambda b: jnp.minimum(b, nb - 1)
    hc = lambda b, h: jnp.where(b < nb, h, nh - 1)
    return pl.pallas_call(
        functools.partial(_flash_outer_kernel, tq=tq, nb=nb),
        grid=(nb + 1, nh, nq),
        in_specs=[pl.BlockSpec((1, tq, qd), lambda b, h, i: (hc(b, h), bc(b) * nq + jnp.where(b < nb, i, nq - 1), 0)),
                  pl.BlockSpec((1, t, qd), lambda b, h, i: (hc(b, h), bc(b), 0)),
                  pl.BlockSpec((1, vd, t), lambda b, h, i: (hc(b, h), 0, bc(b)))],
        out_specs=pl.BlockSpec((tq, vd), lambda b, h, i: (jnp.minimum(b * nq + i, last), h)),
        out_shape=jax.ShapeDtypeStruct((m_total, nh * vd), BF16),
        scratch_shapes=[pltpu.VMEM((1, tq), F32), pltpu.VMEM((1, tq), F32), pltpu.VMEM((vd, tq), F32)],
        compiler_params=_cparams(("arbitrary", "arbitrary", "arbitrary")),
        name="flash",
    )(q_hm, k_hm, vt_hm)


def _qlat_kernel(q_ref, w_ref, o_ref):
    o_ref[0] = _dot(q_ref[0], w_ref[0])


def _qlat_call(q_hm, w_ukt, ms, row_block0):
    nh, nope, lora = w_ukt.shape
    return pl.pallas_call(
        _qlat_kernel,
        grid=(nh,),
        in_specs=[pl.BlockSpec((1, ms, nope), lambda h: (h, row_block0, 0)),
                  pl.BlockSpec((1, nope, lora), lambda h: (h, 0, 0))],
        out_specs=pl.BlockSpec((1, ms, lora), lambda h: (h, 0, 0)),
        out_shape=jax.ShapeDtypeStruct((nh, ms, lora), F32),
        compiler_params=_cparams(("arbitrary",)),
        name="qlat",
    )(q_hm, w_ukt)


def _decode_kernel(pt_ref, ql_ref, qr_ref, cn_ref, kn_ref, ck_hbm, kr_hbm, o_ref,
                   ck_buf, kr_buf, sem, qlb_sc, qrb_sc, kb_sc, rb_sc, m_sc, l_sc, acc_sc, *, npg, sq, ts, rope, n_pages):
    s_blk = pl.program_id(0)
    j = pl.program_id(1)
    nsteps = pl.num_programs(1)
    g = s_blk * nsteps + j
    total = pl.num_programs(0) * nsteps
    nslot = ck_buf.shape[0]
    ahead = nslot - 1
    slot = lax.rem(g, nslot)
    nh, _, lora = ql_ref.shape
    rows = nh * ts
    pg = ck_buf.shape[2]

    def page_copies(step):
        sb = lax.div(step, nsteps)
        jj = lax.rem(step, nsteps)
        sl = lax.rem(step, nslot)
        out = []
        for a in range(sq):
            for i in range(npg):
                page = pt_ref[(sb * sq + a) * n_pages + jj * npg + i]
                out.append(pltpu.make_async_copy(ck_hbm.at[page], ck_buf.at[sl, a * npg + i], sem.at[0, sl]))
                out.append(pltpu.make_async_copy(kr_hbm.at[page], kr_buf.at[sl, a * npg + i], sem.at[1, sl]))
        return out

    for k in range(ahead):
        @pl.when(jnp.logical_and(g == 0, k < total))
        def _prime(k=k):
            for c in page_copies(k):
                c.start()

    @pl.when(g + ahead < total)
    def _start_ahead():
        for c in page_copies(g + ahead):
            c.start()

    for c in page_copies(g):
        c.wait()
    ck_refs = [ck_buf.at[slot, i] for i in range(sq * npg)]
    kr_refs = [kr_buf.at[slot, i] for i in range(sq * npg)]

    @pl.when(j == 0)
    def _init():
        for a in range(sq):
            qlb_sc[a] = ql_ref[:, a * ts:(a + 1) * ts, :].reshape(rows, lora).astype(BF16)
            qrb_sc[a] = qr_ref[:, a * ts:(a + 1) * ts, :].reshape(rows, rope).astype(BF16)
        m_sc[...] = jnp.full_like(m_sc, NEG_BIG)
        l_sc[...] = jnp.zeros_like(l_sc)
        acc_sc[...] = jnp.zeros_like(acc_sc)

    def update(a, s, kb):
        m_prev = m_sc[a]
        m_new = jnp.maximum(m_prev, jnp.max(s, axis=-1, keepdims=True))
        alpha = jnp.exp2(m_prev - m_new)
        p = jnp.exp2(s - m_new)
        l_sc[a] = alpha * l_sc[a] + jnp.sum(p, axis=-1, keepdims=True)
        acc_sc[a] = alpha * acc_sc[a] + _dot(p.astype(BF16), kb)
        m_sc[a] = m_new

    for a in range(sq):
        for i in range(npg):
            kb_sc[a, i * pg:(i + 1) * pg, :] = ck_refs[a * npg + i][...].astype(BF16)
            rb_sc[a, :, i * pg:(i + 1) * pg] = kr_refs[a * npg + i][...].astype(BF16)
    for a in range(sq):
        kb = kb_sc[a]
        update(a, _dot_nt(qlb_sc[a], kb) + _dot(qrb_sc[a], rb_sc[a]), kb)

    @pl.when(j == pl.num_programs(1) - 1)
    def _fin():
        for a in range(sq):
            kn = jnp.concatenate([cn_ref[a * ts:(a + 1) * ts, :], jnp.zeros((pg - ts, lora), F32)], axis=0).astype(BF16)
            rn = jnp.concatenate([kn_ref[a * ts:(a + 1) * ts, :], jnp.zeros((pg - ts, rope), F32)], axis=0).astype(BF16)
            s = _dot_nt(qlb_sc[a], kn) + _dot_nt(qrb_sc[a], rn)
            tq = lax.broadcasted_iota(jnp.int32, s.shape, 0) % ts
            tk = lax.broadcasted_iota(jnp.int32, s.shape, 1)
            update(a, jnp.where(tk <= tq, s, NEG_BIG), kn)
            o = acc_sc[a] / l_sc[a]
            o_ref[:, a * ts:(a + 1) * ts, :] = o.reshape(nh, ts, lora)


def _decode_call(page_table, q_lat, q_rope, ckv_all, kr_all, cache_ckv, cache_krope_t, ns, ts, row_block0, npg, sq):
    nh, ms, lora = q_lat.shape
    rope = q_rope.shape[2]
    n_pages = page_table.shape[1]
    pg = cache_ckv.shape[1]
    nsteps = n_pages // npg
    rows = nh * ts

    n = sq * npg
    grid_spec = pltpu.PrefetchScalarGridSpec(
        num_scalar_prefetch=1,
        grid=(ns // sq, nsteps),
        in_specs=[pl.BlockSpec((nh, sq * ts, lora), lambda s, j, pt: (0, s, 0)),
                  pl.BlockSpec((nh, sq * ts, rope), lambda s, j, pt: (0, s, 0)),
                  pl.BlockSpec((sq * ts, lora), lambda s, j, pt: (row_block0 + s, 0)),
                  pl.BlockSpec((sq * ts, rope), lambda s, j, pt: (row_block0 + s, 0)),
                  pl.BlockSpec(memory_space=pl.ANY),
                  pl.BlockSpec(memory_space=pl.ANY)],
        out_specs=pl.BlockSpec((nh, sq * ts, lora), lambda s, j, pt: (0, s, 0)),
        scratch_shapes=[pltpu.VMEM((DECODE_SLOTS, n, pg, lora), F32), pltpu.VMEM((DECODE_SLOTS, n, rope, pg), F32),
                        pltpu.SemaphoreType.DMA((2, DECODE_SLOTS)),
                        pltpu.VMEM((sq, rows, lora), BF16), pltpu.VMEM((sq, rows, rope), BF16),
                        pltpu.VMEM((sq, npg * pg, lora), BF16), pltpu.VMEM((sq, rope, npg * pg), BF16),
                        pltpu.VMEM((sq, rows, 1), F32), pltpu.VMEM((sq, rows, 1), F32),
                        pltpu.VMEM((sq, rows, lora), F32)],
    )
    return pl.pallas_call(
        functools.partial(_decode_kernel, npg=npg, sq=sq, ts=ts, rope=rope, n_pages=n_pages),
        grid_spec=grid_spec,
        out_shape=jax.ShapeDtypeStruct((nh, ms, lora), F32),
        compiler_params=_cparams(("arbitrary", "arbitrary")),
        name="decode",
    )(page_table.reshape(-1), q_lat, q_rope, ckv_all, kr_all, cache_ckv, cache_krope_t)


def _vup_kernel(o_ref, w_ref, att_any, out_ref):
    del att_any
    out_ref[...] = _dot(o_ref[0].astype(BF16), w_ref[0]).astype(out_ref.dtype)


def _vup_call(o_lat, w_uv3, att_all, row_block0):
    nh, ms, lora = o_lat.shape
    vd = w_uv3.shape[2]
    return pl.pallas_call(
        _vup_kernel,
        grid=(nh,),
        in_specs=[pl.BlockSpec((1, ms, lora), lambda h: (h, 0, 0)),
                  pl.BlockSpec((1, lora, vd), lambda h: (h, 0, 0)),
                  pl.BlockSpec(memory_space=pl.ANY)],
        out_specs=pl.BlockSpec((ms, vd), lambda h: (row_block0, h)),
        out_shape=jax.ShapeDtypeStruct(att_all.shape, att_all.dtype),
        input_output_aliases={2: 0},
        compiler_params=_cparams(("arbitrary",)),
        name="vup",
    )(o_lat, w_uv3, att_all)


def _glu_kernel(g_ref, wa_ref, wb_ref, z_ref, o_ref):
    g = g_ref[...]
    ga = _dot(g, wa_ref[...])
    gb = _dot(g, wb_ref[...])
    o_ref[...] = (ga * _sigmoid(gb) * z_ref[...].astype(F32)).astype(o_ref.dtype)


def _glu_call(gy, w_glu, zazs, tm, tn):
    m, k = gy.shape
    n = w_glu.shape[1] // 2
    nn = n // tn
    zs_blk0 = (zazs.shape[1] - n) // tn
    return pl.pallas_call(
        _glu_kernel,
        grid=(m // tm, nn),
        in_specs=[pl.BlockSpec((tm, k), lambda i, j: (i, 0)),
                  pl.BlockSpec((k, tn), lambda i, j: (0, j)),
                  pl.BlockSpec((k, tn), lambda i, j: (0, nn + j)),
                  pl.BlockSpec((tm, tn), lambda i, j: (i, zs_blk0 + j))],
        out_specs=pl.BlockSpec((tm, tn), lambda i, j: (i, j)),
        out_shape=jax.ShapeDtypeStruct((m, n), BF16),
        compiler_params=_cparams(("arbitrary", "arbitrary")),
        name="glu",
    )(gy, w_glu, w_glu, zazs)


def _merge_kernel(ys_ref, att_ref, za_ref, wbs_ref, wo_ref, gs_ref, ga_ref, o_ref, v_sc):
    @pl.when(pl.program_id(1) == 0)
    def _gate():
        v_sc[...] = (att_ref[...].astype(F32) * za_ref[...].astype(F32)).astype(BF16)

    p_s = _dot(ys_ref[...], wbs_ref[...])
    p_a = _dot(v_sc[...], wo_ref[...])
    o_ref[...] = (gs_ref[...].astype(F32) * p_s + ga_ref[...].astype(F32) * p_a).astype(o_ref.dtype)


def _merge_call(ys2, att, zazs, w_bs, w_o, gates, tm, tn):
    m, ks = ys2.shape
    ka = att.shape[1]
    n = w_bs.shape[1]
    nn = n // tn
    return pl.pallas_call(
        _merge_kernel,
        grid=(m // tm, nn),
        in_specs=[pl.BlockSpec((tm, ks), lambda i, j: (i, 0)),
                  pl.BlockSpec((tm, ka), lambda i, j: (i, 0)),
                  pl.BlockSpec((tm, ka), lambda i, j: (i, 0)),
                  pl.BlockSpec((ks, tn), lambda i, j: (0, j)),
                  pl.BlockSpec((ka, tn), lambda i, j: (0, j)),
                  pl.BlockSpec((tm, tn), lambda i, j: (i, j)),
                  pl.BlockSpec((tm, tn), lambda i, j: (i, nn + j))],
        out_specs=pl.BlockSpec((tm, tn), lambda i, j: (i, j)),
        out_shape=jax.ShapeDtypeStruct((m, n), BF16),
        scratch_shapes=[pltpu.VMEM((tm, ka), BF16)],
        compiler_params=_cparams(("arbitrary", "arbitrary")),
        name="merge",
    )(ys2, att, zazs, w_bs, w_o, gates, gates)


def _final_kernel(mg_ref, w_ref, x_ref, gate_ref, g_ref, b_ref, o_ref, *, alpha):
    out = _dot(mg_ref[...], w_ref[...])
    x = x_ref[...]
    y = alpha * x + gate_ref[...] * out.reshape(x.shape)
    mu = jnp.mean(y, axis=-1, keepdims=True)
    yc = y - mu
    var = jnp.mean(yc * yc, axis=-1, keepdims=True)
    o_ref[...] = yc * lax.rsqrt(var + LN_EPS) * g_ref[...] + b_ref[...]


def _final_call(merged, w_out, x, mod3, ln_g, ln_b, alpha, x_blk, gate_blk, gate_map, merged_map, grid):
    d = w_out.shape[0]
    rows = x_blk[0] * x_blk[1]
    nd = len(grid)
    x_map = (lambda i, j: (i, j, 0)) if nd == 2 else (lambda i: (i, 0, 0))
    const2 = (lambda i, j: (0, 0)) if nd == 2 else (lambda i: (0, 0))
    const3 = (lambda i, j: (0, 0, 0)) if nd == 2 else (lambda i: (0, 0, 0))
    return pl.pallas_call(
        functools.partial(_final_kernel, alpha=alpha),
        grid=grid,
        in_specs=[pl.BlockSpec((rows, d), merged_map),
                  pl.BlockSpec((d, d), const2),
                  pl.BlockSpec(x_blk, x_map),
                  pl.BlockSpec(gate_blk, gate_map),
                  pl.BlockSpec((1, 1, d), const3),
                  pl.BlockSpec((1, 1, d), const3)],
        out_specs=pl.BlockSpec(x_blk, x_map),
        out_shape=jax.ShapeDtypeStruct(x.shape, F32),
        compiler_params=_cparams(("arbitrary",) * nd),
        name="final",
    )(merged, w_out, x, mod3, ln_g.reshape(1, 1, d), ln_b.reshape(1, 1, d))


def _rope_tables(pos, rope):
    freqs = ROPE_BASE ** (-jnp.arange(0, rope, 2, dtype=F32) / rope)
    ang = pos.astype(F32)[:, None] * freqs[None, :]
    cos, sin = jnp.cos(ang), jnp.sin(ang)
    return jnp.concatenate([cos, cos, sin, sin], axis=-1)


def _rot_rows(wt, rope):
    half = rope // 2
    return jnp.concatenate([-wt[..., half:, :], wt[..., :half, :]], axis=-2)


def _block_diag(x, gblk):
    g, r, c = x.shape
    x4 = x.reshape(g // gblk, gblk, r, c)
    eye = jnp.eye(gblk, dtype=x.dtype)
    return jnp.einsum("bgrc,gh->bgrhc", x4, eye).reshape(g // gblk, gblk * r, gblk * c)


def kernel(x_prompt, x_sample, c_prompt, c_sample, cache_ckv, cache_krope, state_ssm_re, state_ssm_im, page_table, w_ada, b_ada, w_in, g_kv, w_uk, w_uv, w_o, lam_re, lam_im, log_dt, b_re, b_im, c_re, c_im, d_skip, w_glu, w_bs, w_out, ln_g, ln_b):
    nb, t, d = x_prompt.shape
    ns, ts, _ = x_sample.shape
    depth = w_in.shape[0]
    assert depth == 1, "single-layer step"
    lora, nh, nope = w_uk.shape[1:]
    vd = w_uv.shape[3]
    rope = cache_krope.shape[3]
    d_ssm = d_skip.shape[1]
    ngrp, nstate, gch = b_re.shape[1:]
    d_attn = nh * vd
    pg = cache_ckv.shape[2]
    past = page_table.shape[1] * pg
    mp, ms = nb * t, ns * ts
    m = mp + ms
    scale = float((nope + rope) ** -0.5) * LOG2E
    alpha = float((2 * depth) ** 0.25)

    tm = math.gcd(math.gcd(mp, ms), 1024)
    tmh = min(tm, 512)
    tn = 512
    assert rope * 2 == 128 and nope == 128 and vd == 128

    wt = jnp.swapaxes(w_in, 1, 2)[0]
    o_u, o_zs, o_q = 0, d_ssm, 2 * d_ssm
    o_ckv = o_q + nh * (nope + rope)
    o_kr = o_ckv + lora
    o_za = o_kr + rope
    o_gs = o_za + d_attn
    o_ga = o_gs + d
    w_u = wt[o_u:o_zs].astype(BF16)
    w_z = jnp.concatenate([wt[o_za:o_gs], wt[o_zs:o_q]], axis=0).astype(BF16)
    w_g = wt[o_gs:].astype(BF16)
    wq = wt[o_q:o_ckv].reshape(nh, nope + rope, d)
    w_q3 = jnp.concatenate([wq, _rot_rows(wq[:, nope:, :], rope)], axis=1).astype(BF16)
    wkr = wt[o_kr:o_za]
    w_ck = jnp.concatenate([wt[o_ckv:o_kr], wkr, _rot_rows(wkr, rope)], axis=0).astype(BF16)
    w_uk3 = w_uk[0].transpose(1, 0, 2).astype(BF16)
    w_uvt = w_uv[0].transpose(1, 2, 0).astype(BF16)
    w_ukt = w_uk[0].transpose(1, 2, 0).astype(BF16)
    w_uv3 = w_uv[0].transpose(1, 0, 2).astype(BF16)
    w_glu_b = w_glu[0].astype(BF16)
    w_bs_b = w_bs[0].astype(BF16)
    w_o_b = w_o[0].astype(BF16)
    w_out_b = w_out[0].astype(BF16)

    pos = jnp.concatenate([jnp.tile(jnp.arange(t), nb), jnp.tile(past + jnp.arange(ts), ns)])
    cs = _rope_tables(pos, rope)

    c_all = jnp.concatenate([c_sample, c_prompt], axis=0)
    pad = (-c_all.shape[0]) % 8
    c_all = jnp.concatenate([c_all, jnp.zeros((pad, d), F32)], axis=0)
    mod = _mod_call(c_all, w_ada[0], b_ada[0], tn)
    mod3 = mod.reshape(mod.shape[0], 1, 3 * d)

    sb = max(1, min(ns, tmh // ts))
    h_all = _hmod_prompt(x_prompt, mod3, ns, m, tmh)
    h_all = _hmod_sample(x_sample, mod3, h_all, mp // (sb * ts), sb)

    u_all = _proj_call(h_all, w_u, None, F32, tm, tn, "proj_u")
    zazs = _proj_call(h_all, w_z, "silu", BF16, tm, tn, "proj_z")
    gates = _proj_call(h_all, w_g, "sigmoid", BF16, tm, tn, "proj_g")
    q_hm = _qproj_call(h_all, w_q3, cs, tm, nope, rope, scale)
    ckv_all, kr_all, ckv_b, kr_b = _ckv_call(h_all, w_ck, g_kv[0], cs, tmh, lora, rope)

    gblk = SSM_GROUP_BLOCK
    a_re, a_im, bb_re, bb_im = _s5_prep_call(lam_re[0], lam_im[0], log_dt[0],
                                             b_re[0].transpose(0, 2, 1), b_im[0].transpose(0, 2, 1))
    bcat = jnp.concatenate([_block_diag(bb_re, gblk), _block_diag(bb_im, gblk)], axis=-1)
    b2 = jnp.concatenate([bcat, bcat], axis=1).astype(BF16)
    ccat = jnp.concatenate([_block_diag(c_re[0].transpose(0, 2, 1), gblk),
                            _block_diag(-c_im[0].transpose(0, 2, 1), gblk)], axis=1).astype(BF16)
    ngb = ngrp // gblk
    sw = gblk * nstate
    amat = jnp.stack([a_re.reshape(ngb, sw), a_im.reshape(ngb, sw)], axis=1)
    s5w = (b2, ccat, amat, d_skip)
    tc = min(t, S5_TIME_CHUNK)
    gy_p, sre_p, sim_p = _s5_prompt_call(u_all, s5w, nb, t, tc)
    ssb = min(ns, 32)
    gy_s, sre_s, sim_s = _s5_sample_call(u_all, s5w, state_ssm_re[0].reshape(ns, ngrp * nstate),
                                         state_ssm_im[0].reshape(ns, ngrp * nstate), ns, ts, mp // (ssb * ts), ssb)
    gy = jnp.concatenate([gy_p.reshape(mp, d_ssm), gy_s], axis=0)

    k_hm, vt_hm = _kvup_call(ckv_b, kr_b, w_uk3, w_uvt, mp, tm, 2)
    tq = min(t, FLASH_BLOCK)
    att = _flash_call(q_hm, k_hm, vt_hm, nb, t, m, tq)
    q_lat = _qlat_call(q_hm, w_ukt, ms, mp // ms)
    q_rope_s = q_hm[:, mp:, nope:nope + rope].astype(F32)
    sq = DECODE_SEQS_PER_STEP
    krope_t = jnp.swapaxes(cache_krope, 2, 3).reshape(cache_krope.shape[1], rope, pg)
    o_lat = _decode_call(page_table, q_lat, q_rope_s, ckv_all, kr_all, cache_ckv.reshape(cache_ckv.shape[1:]), krope_t,
                         ns, ts, mp // (sq * ts), PAGES_PER_STEP, sq)
    att = _vup_call(o_lat, w_uv3, att, mp // ms)

    ys2 = _glu_call(gy, w_glu_b, zazs, tm, tn)
    merged = _merge_call(ys2, att, zazs, w_bs_b, w_o_b, gates, tmh, tn)
    ntp = t // tmh
    y_p = _final_call(merged, w_out_b, x_prompt, mod3, ln_g[0], ln_b[0], alpha,
                      (1, tmh, d), (1, 1, d), lambda i, j: (ns + i, 0, 2), lambda i, j: (i * ntp + j, 0), (nb, ntp))
    y_s = _final_call(merged, w_out_b, x_sample, mod3, ln_g[0], ln_b[0], alpha,
                      (sb, ts, d), (sb, 1, d), lambda i: (i, 0, 2), lambda i: (mp // (sb * ts) + i, 0), (ns // sb,))

    st = lambda a, n: a.reshape(1, n, ngrp, nstate)
    return (y_p, y_s,
            ckv_all[:mp].reshape(1, nb, t, lora), kr_all[:mp].reshape(1, nb, t, rope), st(sre_p, nb), st(sim_p, nb),
            ckv_all[mp:].reshape(1, ns, ts, lora), kr_all[mp:].reshape(1, ns, ts, rope), st(sre_s, ns), st(sim_s, ns))
```

```python
import functools
import math

import jax
import jax.numpy as jnp
from jax import lax
from jax.experimental import pallas as pl
from jax.experimental.pallas import tpu as pltpu

F32 = jnp.float32
BF16 = jnp.bfloat16

LN_EPS = 1e-5
RMS_EPS = 1e-6
ROPE_BASE = 10000.0
NEG_BIG = -1e30
VMEM_LIMIT = 48 * 1024 * 1024
PAGES_PER_STEP = 16
DECODE_SEQS_PER_STEP = 2
DECODE_SLOTS = 3
LOG2E = 1.4426950408889634
SSM_GROUP_BLOCK = 8
LANES = 128
S5_TIME_CHUNK = 256
FLASH_BLOCK = 1024


def _cparams(sem):
    return pltpu.CompilerParams(dimension_semantics=sem, vmem_limit_bytes=VMEM_LIMIT)


def _sigmoid(x):
    return 1.0 / (1.0 + jnp.exp(-x))


def _silu(x):
    return x * _sigmoid(x)


def _dot(a, b):
    return jnp.dot(a, b, preferred_element_type=F32)


def _dot_nt(a, b):
    return lax.dot_general(a, b, (((1,), (1,)), ((), ())), preferred_element_type=F32)


def _mod_kernel(c_ref, w_ref, b_ref, o_ref):
    a = _silu(c_ref[...]).astype(BF16)
    o_ref[...] = _dot(a, w_ref[...].astype(BF16)) + b_ref[...]


def _mod_call(c_all, w_ada, b_ada, tn):
    r, d = c_all.shape
    n = w_ada.shape[1]
    return pl.pallas_call(
        _mod_kernel,
        grid=(n // tn,),
        in_specs=[pl.BlockSpec((r, d), lambda j: (0, 0)),
                  pl.BlockSpec((d, tn), lambda j: (0, j)),
                  pl.BlockSpec((1, tn), lambda j: (0, j))],
        out_specs=pl.BlockSpec((r, tn), lambda j: (0, j)),
        out_shape=jax.ShapeDtypeStruct((r, n), F32),
        compiler_params=_cparams(("arbitrary",)),
        name="mod",
    )(c_all, w_ada, b_ada.reshape(1, n))


def _hmod_kernel(x_ref, sh_ref, sc_ref, *rest):
    o_ref = rest[-1]
    h = x_ref[...] * (1.0 + sc_ref[...]) + sh_ref[...]
    o_ref[...] = h.reshape(o_ref.shape).astype(o_ref.dtype)


def _hmod_prompt_kernel(x_ref, sh_ref, sc_ref, o_ref, *, nb):
    @pl.when(pl.program_id(0) < nb)
    def _rows():
        _hmod_kernel(x_ref, sh_ref, sc_ref, o_ref)

    @pl.when(pl.program_id(0) == nb)
    def _tail():
        o_ref[...] = jnp.zeros_like(o_ref)


def _hmod_prompt(x, mod3, mod_row0, m_total, tm):
    b, t, d = x.shape
    nt = t // tm
    last = m_total // tm - 1
    assert b * nt + nt > last
    bc = lambda i: jnp.minimum(i, b - 1)
    return pl.pallas_call(
        functools.partial(_hmod_prompt_kernel, nb=b),
        grid=(b + 1, nt),
        in_specs=[pl.BlockSpec((1, tm, d), lambda i, j: (bc(i), jnp.where(i < b, j, nt - 1), 0)),
                  pl.BlockSpec((1, 1, d), lambda i, j: (mod_row0 + bc(i), 0, 0)),
                  pl.BlockSpec((1, 1, d), lambda i, j: (mod_row0 + bc(i), 0, 1))],
        out_specs=pl.BlockSpec((tm, d), lambda i, j: (jnp.minimum(i * nt + j, last), 0)),
        out_shape=jax.ShapeDtypeStruct((m_total, d), BF16),
        compiler_params=_cparams(("arbitrary", "arbitrary")),
        name="hmod_prompt",
    )(x, mod3, mod3)


def _hmod_sample(x, mod3, h_all, row_block0, sb):
    ns, ts, d = x.shape
    return pl.pallas_call(
        _hmod_kernel,
        grid=(ns // sb,),
        in_specs=[pl.BlockSpec((sb, ts, d), lambda i: (i, 0, 0)),
                  pl.BlockSpec((sb, 1, d), lambda i: (i, 0, 0)),
                  pl.BlockSpec((sb, 1, d), lambda i: (i, 0, 1)),
                  pl.BlockSpec(memory_space=pl.ANY)],
        out_specs=pl.BlockSpec((sb * ts, d), lambda i: (row_block0 + i, 0)),
        out_shape=jax.ShapeDtypeStruct(h_all.shape, h_all.dtype),
        input_output_aliases={3: 0},
        compiler_params=_cparams(("arbitrary",)),
        name="hmod_sample",
    )(x, mod3, mod3, h_all)


def _proj_kernel(h_ref, w_ref, o_ref, *, act):
    acc = _dot_nt(h_ref[...], w_ref[...])
    if act == "silu":
        acc = _silu(acc)
    elif act == "sigmoid":
        acc = _sigmoid(acc)
    o_ref[...] = acc.astype(o_ref.dtype)


def _proj_call(h, w, act, out_dtype, tm, tn, name):
    m, k = h.shape
    n = w.shape[0]
    return pl.pallas_call(
        functools.partial(_proj_kernel, act=act),
        grid=(m // tm, n // tn),
        in_specs=[pl.BlockSpec((tm, k), lambda i, j: (i, 0)),
                  pl.BlockSpec((tn, k), lambda i, j: (j, 0))],
        out_specs=pl.BlockSpec((tm, tn), lambda i, j: (i, j)),
        out_shape=jax.ShapeDtypeStruct((m, n), out_dtype),
        compiler_params=_cparams(("arbitrary", "arbitrary")),
        name=name,
    )(h, w)


def _qproj_kernel(h_ref, w_ref, cs_ref, o_ref, *, nope, rope, scale, hb):
    h = h_ref[...]
    cs = cs_ref[...]
    for hh in range(hb):
        acc = _dot_nt(h, w_ref[hh])
        t = acc[:, nope:] * cs
        r = t + pltpu.roll(t, rope, 1)
        lane = lax.broadcasted_iota(jnp.int32, r.shape, 1)
        r = jnp.where(lane < rope, r, 0.0)
        o_ref[hh, :, :nope] = (acc[:, :nope] * scale).astype(o_ref.dtype)
        o_ref[hh, :, nope:] = (r * scale).astype(o_ref.dtype)


def _qproj_call(h, w_q3, cs, tm, nope, rope, scale, hb=2):
    m, k = h.shape
    nh, wd, _ = w_q3.shape
    return pl.pallas_call(
        functools.partial(_qproj_kernel, nope=nope, rope=rope, scale=scale, hb=hb),
        grid=(m // tm, nh // hb),
        in_specs=[pl.BlockSpec((tm, k), lambda i, j: (i, 0)),
                  pl.BlockSpec((hb, wd, k), lambda i, j: (j, 0, 0)),
                  pl.BlockSpec((tm, 2 * rope), lambda i, j: (i, 0))],
        out_specs=pl.BlockSpec((hb, tm, wd), lambda i, j: (j, i, 0)),
        out_shape=jax.ShapeDtypeStruct((nh, m, wd), BF16),
        compiler_params=_cparams(("arbitrary", "arbitrary")),
        name="qproj",
    )(h, w_q3, cs)


def _ckv_kernel(h_ref, w_ref, g_ref, cs_ref, ckv_ref, kr_ref, ckvb_ref, krb_ref, *, lora, rope):
    acc = _dot_nt(h_ref[...], w_ref[...])
    c = acc[:, :lora]
    ms = jnp.mean(c * c, axis=-1, keepdims=True)
    ckv = c * lax.rsqrt(ms + RMS_EPS) * g_ref[...]
    ckv_ref[...] = ckv
    ckvb_ref[...] = ckv.astype(BF16)
    t = acc[:, lora:] * cs_ref[...]
    r = t + pltpu.roll(t, rope, 1)
    kr_ref[...] = r[:, :rope]
    lane = lax.broadcasted_iota(jnp.int32, r.shape, 1)
    krb_ref[...] = jnp.where(lane < rope, r, 0.0).astype(BF16)


def _ckv_call(h, w_ck, g_kv, cs, tm, lora, rope):
    m, k = h.shape
    wd = w_ck.shape[0]
    return pl.pallas_call(
        functools.partial(_ckv_kernel, lora=lora, rope=rope),
        grid=(m // tm,),
        in_specs=[pl.BlockSpec((tm, k), lambda i: (i, 0)),
                  pl.BlockSpec((wd, k), lambda i: (0, 0)),
                  pl.BlockSpec((1, lora), lambda i: (0, 0)),
                  pl.BlockSpec((tm, 2 * rope), lambda i: (i, 0))],
        out_specs=[pl.BlockSpec((tm, lora), lambda i: (i, 0)),
                   pl.BlockSpec((tm, rope), lambda i: (i, 0)),
                   pl.BlockSpec((tm, lora), lambda i: (i, 0)),
                   pl.BlockSpec((tm, 2 * rope), lambda i: (i, 0))],
        out_shape=[jax.ShapeDtypeStruct((m, lora), F32),
                   jax.ShapeDtypeStruct((m, rope), F32),
                   jax.ShapeDtypeStruct((m, lora), BF16),
                   jax.ShapeDtypeStruct((m, 2 * rope), BF16)],
        compiler_params=_cparams(("arbitrary",)),
        name="ckvproj",
    )(h, w_ck, g_kv.reshape(1, lora), cs)


def _s5_prep_kernel(lre_ref, lim_ref, ldt_ref, bre_ref, bim_ref, are_ref, aim_ref, bbre_ref, bbim_ref):
    lre = lre_ref[...]
    lim = lim_ref[...]
    dt = jnp.exp(ldt_ref[...])
    mag = jnp.exp(lre * dt)
    a_re = mag * jnp.cos(lim * dt)
    a_im = mag * jnp.sin(lim * dt)
    are_ref[...] = a_re
    aim_ref[...] = a_im
    den = lre * lre + lim * lim
    nr = a_re - 1.0
    c_re = (nr * lre + a_im * lim) / den
    c_im = (a_im * lre - nr * lim) / den
    bre = bre_ref[...]
    bim = bim_ref[...]
    bbre_ref[...] = c_re * bre - c_im * bim
    bbim_ref[...] = c_re * bim + c_im * bre


def _s5_prep_call(lam_re, lam_im, log_dt, b_re_t, b_im_t):
    g, p = lam_re.shape
    n = b_re_t.shape[1]
    a_re, a_im, bb_re, bb_im = pl.pallas_call(
        _s5_prep_kernel,
        out_shape=[jax.ShapeDtypeStruct((g, 1, p), F32), jax.ShapeDtypeStruct((g, 1, p), F32),
                   jax.ShapeDtypeStruct((g, n, p), F32), jax.ShapeDtypeStruct((g, n, p), F32)],
        name="s5prep",
    )(lam_re.reshape(g, 1, p), lam_im.reshape(g, 1, p), log_dt.reshape(g, 1, 1), b_re_t, b_im_t)
    return a_re.reshape(g, p), a_im.reshape(g, p), bb_re, bb_im


def _s5_kernel(*refs, n_u, tc, nreal, pair, sw, has_h0):
    u_refs = refs[:n_u]
    i = n_u
    b2_ref, c_ref, a_ref, d_ref = refs[i:i + 4]
    i += 4
    if has_h0:
        h0re_ref, h0im_ref = refs[i:i + 2]
        i += 2
    gy_ref, sre_ref, sim_ref = refs[i:i + 3]
    i += 3
    us_sc, ut_sc, bu_sc, y_sc, st_sc = refs[i:i + 5]
    nseq = nreal * pair
    gk = d_ref.shape[1] // pair
    assert tc % 8 == 0 and nseq % 8 == 0 and (pair == 1 or n_u == nreal)

    step = pl.program_id(1)

    @pl.when(step == 0)
    def _init():
        if has_h0:
            st_sc[:, :sw] = h0re_ref[...]
            st_sc[:, sw:] = h0im_ref[...]
        else:
            st_sc[...] = jnp.zeros_like(st_sc)

    def u_of(v):
        b, p = v % nreal, v // nreal
        return u_refs[b][:, p * gk:(p + 1) * gk] if n_u > 1 else u_refs[0][b * tc:(b + 1) * tc, :]

    nkb = gk // LANES
    nub = pair * nkb
    for v in range(nseq):
        p, uv = v // nreal, u_of(v)
        for kb in range(nub):
            own = kb // nkb == p
            us_sc[kb, v * tc:(v + 1) * tc, :] = (uv[:, (kb - p * nkb) * LANES:(kb - p * nkb + 1) * LANES] if own
                                                 else jnp.zeros((tc, LANES), F32))
    for sg in range(0, nseq, 8):
        def permute(t, c, base=sg * tc):
            dst = pl.multiple_of(base + t * 8, 8)
            for kb in range(nub):
                ut_sc[kb, pl.ds(dst, 8), :] = us_sc[kb, pl.ds(base + t, 8, stride=tc), :]
            return c

        lax.fori_loop(0, tc, permute, 0, unroll=8)
    up = jnp.concatenate([ut_sc[kb] for kb in range(nub)], axis=1)
    if pair == 1:
        u_hi = up.astype(BF16)
        u_lo = (up - u_hi.astype(F32)).astype(BF16)
        lhs = jnp.concatenate([u_hi, u_lo], axis=1)
    else:
        lhs = up.astype(BF16)
    bu = _dot(lhs, b2_ref[0])
    nj = sw // LANES
    for j in range(2 * nj):
        bu_sc[j] = bu[:, j * LANES:(j + 1) * LANES]

    for sg in range(0, nseq, 8):
        n8 = 8
        base = sg * tc
        def a_tile(r, j, sg=sg, n8=n8):
            ps = [v // nreal for v in range(sg, sg + n8)]
            runs = [(p, ps.count(p)) for p in sorted(set(ps))]
            parts = [jnp.broadcast_to(a_ref[0, 2 * p + r:2 * p + r + 1, j * LANES:(j + 1) * LANES], (cnt, LANES))
                     for p, cnt in runs]
            return parts[0] if len(parts) == 1 else jnp.concatenate(parts, axis=0)

        a_b = [a_tile(r, j) for r in range(2) for j in range(nj)]

        def body(i8, carry, base=base, a_b=a_b):
            h = list(carry)
            for k in range(8):
                rows = pl.ds(pl.multiple_of(base + (i8 * 8 + k) * 8, 8), 8)
                new = [None] * (2 * nj)
                for j in range(nj):
                    new[j] = a_b[j] * h[j] - a_b[nj + j] * h[nj + j] + bu_sc[j, rows, :]
                    new[nj + j] = a_b[j] * h[nj + j] + a_b[nj + j] * h[j] + bu_sc[nj + j, rows, :]
                for j in range(2 * nj):
                    bu_sc[j, rows, :] = new[j]
                h = new
            return tuple(h)

        init = tuple(st_sc[sg:sg + n8, j * LANES:(j + 1) * LANES] for j in range(2 * nj))
        fin = lax.fori_loop(0, tc // 8, body, init)
        for j in range(2 * nj):
            st_sc[sg:sg + n8, j * LANES:(j + 1) * LANES] = fin[j]

    hs = jnp.concatenate([bu_sc[j] for j in range(2 * nj)], axis=1)
    yp = _dot(hs.astype(BF16), c_ref[0])
    for jb in range(nub):
        y_sc[jb] = yp[:, jb * LANES:(jb + 1) * LANES]
    for v in range(nseq):
        b, p = v % nreal, v // nreal
        sg = (v // 8) * 8
        rows = pl.ds(sg * tc + (v - sg), tc, stride=8)
        y = jnp.concatenate([y_sc[p * nkb + jb, rows, :] for jb in range(nkb)], axis=1)
        y = y + d_ref[:, p * gk:(p + 1) * gk] * u_of(v)
        gy = (0.5 * y * (1.0 + lax.erf(y * (1.0 / math.sqrt(2.0))))).astype(gy_ref.dtype)
        if len(gy_ref.shape) == 3:
            gy_ref[b, :, p * gk:(p + 1) * gk] = gy
        else:
            gy_ref[b * tc:(b + 1) * tc, :] = gy

    @pl.when(step == pl.num_programs(1) - 1)
    def _fin():
        for p in range(pair):
            sre_ref[:, p * sw:(p + 1) * sw] = st_sc[p * nreal:(p + 1) * nreal, :sw]
            sim_ref[:, p * sw:(p + 1) * sw] = st_sc[p * nreal:(p + 1) * nreal, sw:]


def _s5_scratch(nseq, tc, gk, sw):
    rows = nseq * tc
    return [pltpu.VMEM((gk // LANES, rows, LANES), F32),
            pltpu.VMEM((gk // LANES, rows, LANES), F32),
            pltpu.VMEM((2 * sw // LANES, rows, LANES), F32),
            pltpu.VMEM((gk // LANES, rows, LANES), F32),
            pltpu.VMEM((nseq, 2 * sw), F32)]


def _s5_weight_specs(kin, gk, sw, pair, idx):
    return [pl.BlockSpec((1, kin, 2 * sw), lambda *a: (idx(*a), 0, 0)),
            pl.BlockSpec((1, 2 * sw, pair * gk), lambda *a: (idx(*a), 0, 0)),
            pl.BlockSpec((1, 2 * pair, sw), lambda *a: (idx(*a), 0, 0)),
            pl.BlockSpec((1, pair * gk), lambda *a: (0, idx(*a)))]


def _s5_prompt_call(u_all, wts, nb, t, tc, pair):
    bw, cmat, amat, dsk = wts
    ngp, kin, sw2 = bw.shape
    sw = sw2 // 2
    gk = cmat.shape[2] // pair
    ntc = t // tc
    gb_of = lambda g, s: g
    u_specs = [pl.BlockSpec((tc, pair * gk), functools.partial(lambda g, s, b: (b * ntc + s, g), b=b))
               for b in range(nb)]
    kern = functools.partial(_s5_kernel, n_u=nb, tc=tc, nreal=nb, pair=pair, sw=sw, has_h0=False)
    return pl.pallas_call(
        kern,
        grid=(ngp, ntc),
        in_specs=u_specs + _s5_weight_specs(kin, gk, sw, pair, gb_of),
        out_specs=[pl.BlockSpec((nb, tc, pair * gk), lambda g, s: (0, s, g)),
                   pl.BlockSpec((nb, pair * sw), lambda g, s: (0, g)),
                   pl.BlockSpec((nb, pair * sw), lambda g, s: (0, g))],
        out_shape=[jax.ShapeDtypeStruct((nb, t, ngp * pair * gk), BF16),
                   jax.ShapeDtypeStruct((nb, ngp * pair * sw), F32),
                   jax.ShapeDtypeStruct((nb, ngp * pair * sw), F32)],
        scratch_shapes=_s5_scratch(nb * pair, tc, pair * gk, sw),
        compiler_params=_cparams(("arbitrary", "arbitrary")),
        name="s5_prompt",
    )(*([u_all] * nb), bw, cmat, amat, dsk)


def _s5_sample_call(u_all, wts, h0_re, h0_im, ns, ts, row_block0, sb):
    b2, cmat, amat, dsk = wts
    ngb, gk2, sw2 = b2.shape
    gk, sw = gk2 // 2, sw2 // 2
    gb_of = lambda g, s, z: g
    kern = functools.partial(_s5_kernel, n_u=1, tc=ts, nreal=sb, pair=1, sw=sw, has_h0=True)
    return pl.pallas_call(
        kern,
        grid=(ngb, 1, ns // sb),
        in_specs=[pl.BlockSpec((sb * ts, gk), lambda g, s, z: (row_block0 + z, g))]
        + _s5_weight_specs(gk2, gk, sw, 1, gb_of)
        + [pl.BlockSpec((sb, sw), lambda g, s, z: (z, g)), pl.BlockSpec((sb, sw), lambda g, s, z: (z, g))],
        out_specs=[pl.BlockSpec((sb * ts, gk), lambda g, s, z: (z, g)),
                   pl.BlockSpec((sb, sw), lambda g, s, z: (z, g)),
                   pl.BlockSpec((sb, sw), lambda g, s, z: (z, g))],
        out_shape=[jax.ShapeDtypeStruct((ns * ts, ngb * gk), BF16),
                   jax.ShapeDtypeStruct((ns, ngb * sw), F32),
                   jax.ShapeDtypeStruct((ns, ngb * sw), F32)],
        scratch_shapes=_s5_scratch(sb, ts, gk, sw),
        compiler_params=_cparams(("arbitrary", "arbitrary", "arbitrary")),
        name="s5_sample",
    )(u_all, b2, cmat, amat, dsk, h0_re, h0_im)


def _kvup_kernel(c_ref, kr_ref, wk_ref, wvt_ref, k_ref, vt_ref, *, nope, hb):
    c = c_ref[...]
    for h in range(hb):
        k_ref[h, :, :nope] = _dot(c, wk_ref[h]).astype(k_ref.dtype)
        k_ref[h, :, nope:] = kr_ref[...]
        vt_ref[h] = _dot_nt(wvt_ref[h], c).astype(vt_ref.dtype)


def _kvup_call(ckv_b, kr_b, w_uk3, w_uvt, mp, tm, hb):
    lora = ckv_b.shape[1]
    nh, _, nope = w_uk3.shape
    vd = w_uvt.shape[1]
    rw = kr_b.shape[1]
    return pl.pallas_call(
        functools.partial(_kvup_kernel, nope=nope, hb=hb),
        grid=(mp // tm, nh // hb),
        in_specs=[pl.BlockSpec((tm, lora), lambda i, j: (i, 0)),
                  pl.BlockSpec((tm, rw), lambda i, j: (i, 0)),
                  pl.BlockSpec((hb, lora, nope), lambda i, j: (j, 0, 0)),
                  pl.BlockSpec((hb, vd, lora), lambda i, j: (j, 0, 0))],
        out_specs=[pl.BlockSpec((hb, tm, nope + rw), lambda i, j: (j, i, 0)),
                   pl.BlockSpec((hb, vd, tm), lambda i, j: (j, 0, i))],
        out_shape=[jax.ShapeDtypeStruct((nh, mp, nope + rw), BF16),
                   jax.ShapeDtypeStruct((nh, vd, mp), BF16)],
        compiler_params=_cparams(("arbitrary", "arbitrary")),
        name="kvup",
    )(ckv_b, kr_b, w_uk3, w_uvt)


def _flash_kernel(q_ref, k_ref, vt_ref, o_ref, m_sc, l_sc, acc_sc, *, tq):
    qi = pl.program_id(2)
    q = q_ref[0]
    m_sc[...] = jnp.full_like(m_sc, NEG_BIG)
    l_sc[...] = jnp.zeros_like(l_sc)
    acc_sc[...] = jnp.zeros_like(acc_sc)

    def block(ki, diagonal):
        start = pl.multiple_of(ki * tq, tq)
        s = _dot_nt(k_ref[0, pl.ds(start, tq), :], q)
        if diagonal:
            key = lax.broadcasted_iota(jnp.int32, s.shape, 0)
            qry = lax.broadcasted_iota(jnp.int32, s.shape, 1)
            s = jnp.where(key <= qry, s, NEG_BIG)
        m_prev = m_sc[...]
        m_new = jnp.maximum(m_prev, jnp.max(s, axis=0, keepdims=True))
        alpha = jnp.exp2(m_prev - m_new)
        p = jnp.exp2(s - m_new)
        l_sc[...] = alpha * l_sc[...] + jnp.sum(p, axis=0, keepdims=True)
        acc_sc[...] = alpha * acc_sc[...] + _dot(vt_ref[0, :, pl.ds(start, tq)], p.astype(BF16))
        m_sc[...] = m_new

    def below_diagonal(ki, carry):
        block(ki, False)
        return carry

    lax.fori_loop(0, qi, below_diagonal, 0)
    block(qi, True)
    o_ref[...] = (acc_sc[...] / l_sc[...]).T.astype(o_ref.dtype)


def _flash_outer_kernel(q_ref, k_ref, vt_ref, o_ref, m_sc, l_sc, acc_sc, *, tq, nb):
    @pl.when(pl.program_id(0) < nb)
    def _rows():
        _flash_kernel(q_ref, k_ref, vt_ref, o_ref, m_sc, l_sc, acc_sc, tq=tq)

    @pl.when(pl.program_id(0) == nb)
    def _tail():
        o_ref[...] = jnp.zeros_like(o_ref)


def _flash_call(q_hm, k_hm, vt_hm, nb, t, m_total, tq):
    nh, _, qd = q_hm.shape
    vd = vt_hm.shape[1]
    nq = t // tq
    last = m_total // tq - 1
    assert nb * nq + nq > last
    bc = lambda b: jnp.minimum(b, nb - 1)
    hc = lambda b, h: jnp.where(b < nb, h, nh - 1)
    return pl.pallas_call(
        functools.partial(_flash_outer_kernel, tq=tq, nb=nb),
        grid=(nb + 1, nh, nq),
        in_specs=[pl.BlockSpec((1, tq, qd), lambda b, h, i: (hc(b, h), bc(b) * nq + jnp.where(b < nb, i, nq - 1), 0)),
                  pl.BlockSpec((1, t, qd), lambda b, h, i: (hc(b, h), bc(b), 0)),
                  pl.BlockSpec((1, vd, t), lambda b, h, i: (hc(b, h), 0, bc(b)))],
        out_specs=pl.BlockSpec((tq, vd), lambda b, h, i: (jnp.minimum(b * nq + i, last), h)),
        out_shape=jax.ShapeDtypeStruct((m_total, nh * vd), BF16),
        scratch_shapes=[pltpu.VMEM((1, tq), F32), pltpu.VMEM((1, tq), F32), pltpu.VMEM((vd, tq), F32)],
        compiler_params=_cparams(("arbitrary", "arbitrary", "arbitrary")),
        name="flash",
    )(q_hm, k_hm, vt_hm)


def _qlat_kernel(q_ref, w_ref, o_ref):
    o_ref[0] = _dot(q_ref[0], w_ref[0])


def _qlat_call(q_hm, w_ukt, ms, row_block0):
    nh, nope, lora = w_ukt.shape
    return pl.pallas_call(
        _qlat_kernel,
        grid=(nh,),
        in_specs=[pl.BlockSpec((1, ms, nope), lambda h: (h, row_block0, 0)),
                  pl.BlockSpec((1, nope, lora), lambda h: (h, 0, 0))],
        out_specs=pl.BlockSpec((1, ms, lora), lambda h: (h, 0, 0)),
        out_shape=jax.ShapeDtypeStruct((nh, ms, lora), F32),
        compiler_params=_cparams(("arbitrary",)),
        name="qlat",
    )(q_hm, w_ukt)


def _decode_kernel(pt_ref, ql_ref, qr_ref, cn_ref, kn_ref, ck_hbm, kr_hbm, o_ref,
                   ck_buf, kr_buf, sem, qlb_sc, qrb_sc, kb_sc, rb_sc, m_sc, l_sc, acc_sc, *, npg, sq, ts, rope, n_pages):
    s_blk = pl.program_id(0)
    j = pl.program_id(1)
    nsteps = pl.num_programs(1)
    g = s_blk * nsteps + j
    total = pl.num_programs(0) * nsteps
    nslot = ck_buf.shape[0]
    ahead = nslot - 1
    slot = lax.rem(g, nslot)
    nh, _, lora = ql_ref.shape
    rows = nh * ts
    pg = ck_buf.shape[2]

    def page_copies(step):
        sb = lax.div(step, nsteps)
        jj = lax.rem(step, nsteps)
        sl = lax.rem(step, nslot)
        out = []
        for a in range(sq):
            for i in range(npg):
                page = pt_ref[(sb * sq + a) * n_pages + jj * npg + i]
                out.append(pltpu.make_async_copy(ck_hbm.at[page], ck_buf.at[sl, a * npg + i], sem.at[0, sl]))
                out.append(pltpu.make_async_copy(kr_hbm.at[page], kr_buf.at[sl, a * npg + i], sem.at[1, sl]))
        return out

    for k in range(ahead):
        @pl.when(jnp.logical_and(g == 0, k < total))
        def _prime(k=k):
            for c in page_copies(k):
                c.start()

    @pl.when(g + ahead < total)
    def _start_ahead():
        for c in page_copies(g + ahead):
            c.start()

    for c in page_copies(g):
        c.wait()
    ck_refs = [ck_buf.at[slot, i] for i in range(sq * npg)]
    kr_refs = [kr_buf.at[slot, i] for i in range(sq * npg)]

    @pl.when(j == 0)
    def _init():
        for a in range(sq):
            qlb_sc[a] = ql_ref[:, a * ts:(a + 1) * ts, :].reshape(rows, lora).astype(BF16)
            qrb_sc[a] = qr_ref[:, a * ts:(a + 1) * ts, :].reshape(rows, rope).astype(BF16)
        m_sc[...] = jnp.full_like(m_sc, NEG_BIG)
        l_sc[...] = jnp.zeros_like(l_sc)
        acc_sc[...] = jnp.zeros_like(acc_sc)

    def update(a, s, kb):
        m_prev = m_sc[a]
        m_new = jnp.maximum(m_prev, jnp.max(s, axis=-1, keepdims=True))
        alpha = jnp.exp2(m_prev - m_new)
        p = jnp.exp2(s - m_new)
        l_sc[a] = alpha * l_sc[a] + jnp.sum(p, axis=-1, keepdims=True)
        acc_sc[a] = alpha * acc_sc[a] + _dot(p.astype(BF16), kb)
        m_sc[a] = m_new

    for a in range(sq):
        for i in range(npg):
            kb_sc[a, i * pg:(i + 1) * pg, :] = ck_refs[a * npg + i][...].astype(BF16)
            rb_sc[a, :, i * pg:(i + 1) * pg] = kr_refs[a * npg + i][...].astype(BF16)
    for a in range(sq):
        kb = kb_sc[a]
        update(a, _dot_nt(qlb_sc[a], kb) + _dot(qrb_sc[a], rb_sc[a]), kb)

    @pl.when(j == pl.num_programs(1) - 1)
    def _fin():
        for a in range(sq):
            kn = jnp.concatenate([cn_ref[a * ts:(a + 1) * ts, :], jnp.zeros((pg - ts, lora), F32)], axis=0).astype(BF16)
            rn = jnp.concatenate([kn_ref[a * ts:(a + 1) * ts, :], jnp.zeros((pg - ts, rope), F32)], axis=0).astype(BF16)
            s = _dot_nt(qlb_sc[a], kn) + _dot_nt(qrb_sc[a], rn)
            tq = lax.broadcasted_iota(jnp.int32, s.shape, 0) % ts
            tk = lax.broadcasted_iota(jnp.int32, s.shape, 1)
            update(a, jnp.where(tk <= tq, s, NEG_BIG), kn)
            o = acc_sc[a] / l_sc[a]
            o_ref[:, a * ts:(a + 1) * ts, :] = o.reshape(nh, ts, lora)


def _decode_call(page_table, q_lat, q_rope, ckv_all, kr_all, cache_ckv, cache_krope_t, ns, ts, row_block0, npg, sq):
    nh, ms, lora = q_lat.shape
    rope = q_rope.shape[2]
    n_pages = page_table.shape[1]
    pg = cache_ckv.shape[1]
    nsteps = n_pages // npg
    rows = nh * ts

    n = sq * npg
    grid_spec = pltpu.PrefetchScalarGridSpec(
        num_scalar_prefetch=1,
        grid=(ns // sq, nsteps),
        in_specs=[pl.BlockSpec((nh, sq * ts, lora), lambda s, j, pt: (0, s, 0)),
                  pl.BlockSpec((nh, sq * ts, rope), lambda s, j, pt: (0, s, 0)),
                  pl.BlockSpec((sq * ts, lora), lambda s, j, pt: (row_block0 + s, 0)),
                  pl.BlockSpec((sq * ts, rope), lambda s, j, pt: (row_block0 + s, 0)),
                  pl.BlockSpec(memory_space=pl.ANY),
                  pl.BlockSpec(memory_space=pl.ANY)],
        out_specs=pl.BlockSpec((nh, sq * ts, lora), lambda s, j, pt: (0, s, 0)),
        scratch_shapes=[pltpu.VMEM((DECODE_SLOTS, n, pg, lora), F32), pltpu.VMEM((DECODE_SLOTS, n, rope, pg), F32),
                        pltpu.SemaphoreType.DMA((2, DECODE_SLOTS)),
                        pltpu.VMEM((sq, rows, lora), BF16), pltpu.VMEM((sq, rows, rope), BF16),
                        pltpu.VMEM((sq, npg * pg, lora), BF16), pltpu.VMEM((sq, rope, npg * pg), BF16),
                        pltpu.VMEM((sq, rows, 1), F32), pltpu.VMEM((sq, rows, 1), F32),
                        pltpu.VMEM((sq, rows, lora), F32)],
    )
    return pl.pallas_call(
        functools.partial(_decode_kernel, npg=npg, sq=sq, ts=ts, rope=rope, n_pages=n_pages),
        grid_spec=grid_spec,
        out_shape=jax.ShapeDtypeStruct((nh, ms, lora), F32),
        compiler_params=_cparams(("arbitrary", "arbitrary")),
        name="decode",
    )(page_table.reshape(-1), q_lat, q_rope, ckv_all, kr_all, cache_ckv, cache_krope_t)


def _vup_kernel(o_ref, w_ref, att_any, out_ref):
    del att_any
    out_ref[...] = _dot(o_ref[0].astype(BF16), w_ref[0]).astype(out_ref.dtype)


def _vup_call(o_lat, w_uv3, att_all, row_block0):
    nh, ms, lora = o_lat.shape
    vd = w_uv3.shape[2]
    return pl.pallas_call(
        _vup_kernel,
        grid=(nh,),
        in_specs=[pl.BlockSpec((1, ms, lora), lambda h: (h, 0, 0)),
                  pl.BlockSpec((1, lora, vd), lambda h: (h, 0, 0)),
                  pl.BlockSpec(memory_space=pl.ANY)],
        out_specs=pl.BlockSpec((ms, vd), lambda h: (row_block0, h)),
        out_shape=jax.ShapeDtypeStruct(att_all.shape, att_all.dtype),
        input_output_aliases={2: 0},
        compiler_params=_cparams(("arbitrary",)),
        name="vup",
    )(o_lat, w_uv3, att_all)


def _glu_kernel(g_ref, wa_ref, wb_ref, z_ref, o_ref):
    g = g_ref[...]
    ga = _dot(g, wa_ref[...])
    gb = _dot(g, wb_ref[...])
    o_ref[...] = (ga * _sigmoid(gb) * z_ref[...].astype(F32)).astype(o_ref.dtype)


def _glu_call(gy, w_glu, zazs, tm, tn):
    m, k = gy.shape
    n = w_glu.shape[1] // 2
    nn = n // tn
    zs_blk0 = (zazs.shape[1] - n) // tn
    return pl.pallas_call(
        _glu_kernel,
        grid=(m // tm, nn),
        in_specs=[pl.BlockSpec((tm, k), lambda i, j: (i, 0)),
                  pl.BlockSpec((k, tn), lambda i, j: (0, j)),
                  pl.BlockSpec((k, tn), lambda i, j: (0, nn + j)),
                  pl.BlockSpec((tm, tn), lambda i, j: (i, zs_blk0 + j))],
        out_specs=pl.BlockSpec((tm, tn), lambda i, j: (i, j)),
        out_shape=jax.ShapeDtypeStruct((m, n), BF16),
        compiler_params=_cparams(("arbitrary", "arbitrary")),
        name="glu",
    )(gy, w_glu, w_glu, zazs)


def _merge_kernel(ys_ref, att_ref, za_ref, wbs_ref, wo_ref, gs_ref, ga_ref, o_ref, v_sc):
    @pl.when(pl.program_id(1) == 0)
    def _gate():
        v_sc[...] = (att_ref[...].astype(F32) * za_ref[...].astype(F32)).astype(BF16)

    p_s = _dot(ys_ref[...], wbs_ref[...])
    p_a = _dot(v_sc[...], wo_ref[...])
    o_ref[...] = (gs_ref[...].astype(F32) * p_s + ga_ref[...].astype(F32) * p_a).astype(o_ref.dtype)


def _merge_call(ys2, att, zazs, w_bs, w_o, gates, tm, tn):
    m, ks = ys2.shape
    ka = att.shape[1]
    n = w_bs.shape[1]
    nn = n // tn
    return pl.pallas_call(
        _merge_kernel,
        grid=(m // tm, nn),
        in_specs=[pl.BlockSpec((tm, ks), lambda i, j: (i, 0)),
                  pl.BlockSpec((tm, ka), lambda i, j: (i, 0)),
                  pl.BlockSpec((tm, ka), lambda i, j: (i, 0)),
                  pl.BlockSpec((ks, tn), lambda i, j: (0, j)),
                  pl.BlockSpec((ka, tn), lambda i, j: (0, j)),
                  pl.BlockSpec((tm, tn), lambda i, j: (i, j)),
                  pl.BlockSpec((tm, tn), lambda i, j: (i, nn + j))],
        out_specs=pl.BlockSpec((tm, tn), lambda i, j: (i, j)),
        out_shape=jax.ShapeDtypeStruct((m, n), BF16),
        scratch_shapes=[pltpu.VMEM((tm, ka), BF16)],
        compiler_params=_cparams(("arbitrary", "arbitrary")),
        name="merge",
    )(ys2, att, zazs, w_bs, w_o, gates, gates)


def _final_kernel(mg_ref, w_ref, x_ref, gate_ref, g_ref, b_ref, o_ref, *, alpha):
    out = _dot(mg_ref[...], w_ref[...])
    x = x_ref[...]
    y = alpha * x + gate_ref[...] * out.reshape(x.shape)
    mu = jnp.mean(y, axis=-1, keepdims=True)
    yc = y - mu
    var = jnp.mean(yc * yc, axis=-1, keepdims=True)
    o_ref[...] = yc * lax.rsqrt(var + LN_EPS) * g_ref[...] + b_ref[...]


def _final_call(merged, w_out, x, mod3, ln_g, ln_b, alpha, x_blk, gate_blk, gate_map, merged_map, grid):
    d = w_out.shape[0]
    rows = x_blk[0] * x_blk[1]
    nd = len(grid)
    x_map = (lambda i, j: (i, j, 0)) if nd == 2 else (lambda i: (i, 0, 0))
    const2 = (lambda i, j: (0, 0)) if nd == 2 else (lambda i: (0, 0))
    const3 = (lambda i, j: (0, 0, 0)) if nd == 2 else (lambda i: (0, 0, 0))
    return pl.pallas_call(
        functools.partial(_final_kernel, alpha=alpha),
        grid=grid,
        in_specs=[pl.BlockSpec((rows, d), merged_map),
                  pl.BlockSpec((d, d), const2),
                  pl.BlockSpec(x_blk, x_map),
                  pl.BlockSpec(gate_blk, gate_map),
                  pl.BlockSpec((1, 1, d), const3),
                  pl.BlockSpec((1, 1, d), const3)],
        out_specs=pl.BlockSpec(x_blk, x_map),
        out_shape=jax.ShapeDtypeStruct(x.shape, F32),
        compiler_params=_cparams(("arbitrary",) * nd),
        name="final",
    )(merged, w_out, x, mod3, ln_g.reshape(1, 1, d), ln_b.reshape(1, 1, d))


def _rope_tables(pos, rope):
    freqs = ROPE_BASE ** (-jnp.arange(0, rope, 2, dtype=F32) / rope)
    ang = pos.astype(F32)[:, None] * freqs[None, :]
    cos, sin = jnp.cos(ang), jnp.sin(ang)
    return jnp.concatenate([cos, cos, sin, sin], axis=-1)


def _rot_rows(wt, rope):
    half = rope // 2
    return jnp.concatenate([-wt[..., half:, :], wt[..., :half, :]], axis=-2)


def _block_diag(x, gblk):
    g, r, c = x.shape
    x4 = x.reshape(g // gblk, gblk, r, c)
    eye = jnp.eye(gblk, dtype=x.dtype)
    return jnp.einsum("bgrc,gh->bgrhc", x4, eye).reshape(g // gblk, gblk * r, gblk * c)


def kernel(x_prompt, x_sample, c_prompt, c_sample, cache_ckv, cache_krope, state_ssm_re, state_ssm_im, page_table, w_ada, b_ada, w_in, g_kv, w_uk, w_uv, w_o, lam_re, lam_im, log_dt, b_re, b_im, c_re, c_im, d_skip, w_glu, w_bs, w_out, ln_g, ln_b):
    nb, t, d = x_prompt.shape
    ns, ts, _ = x_sample.shape
    depth = w_in.shape[0]
    assert depth == 1, "single-layer step"
    lora, nh, nope = w_uk.shape[1:]
    vd = w_uv.shape[3]
    rope = cache_krope.shape[3]
    d_ssm = d_skip.shape[1]
    ngrp, nstate, gch = b_re.shape[1:]
    d_attn = nh * vd
    pg = cache_ckv.shape[2]
    past = page_table.shape[1] * pg
    mp, ms = nb * t, ns * ts
    m = mp + ms
    scale = float((nope + rope) ** -0.5) * LOG2E
    alpha = float((2 * depth) ** 0.25)

    tm = math.gcd(math.gcd(mp, ms), 1024)
    tmh = min(tm, 512)
    tn = 512
    assert rope * 2 == 128 and nope == 128 and vd == 128

    wt = jnp.swapaxes(w_in, 1, 2)[0]
    o_u, o_zs, o_q = 0, d_ssm, 2 * d_ssm
    o_ckv = o_q + nh * (nope + rope)
    o_kr = o_ckv + lora
    o_za = o_kr + rope
    o_gs = o_za + d_attn
    o_ga = o_gs + d
    w_u = wt[o_u:o_zs].astype(BF16)
    w_z = jnp.concatenate([wt[o_za:o_gs], wt[o_zs:o_q]], axis=0).astype(BF16)
    w_g = wt[o_gs:].astype(BF16)
    wq = wt[o_q:o_ckv].reshape(nh, nope + rope, d)
    w_q3 = jnp.concatenate([wq, _rot_rows(wq[:, nope:, :], rope)], axis=1).astype(BF16)
    wkr = wt[o_kr:o_za]
    w_ck = jnp.concatenate([wt[o_ckv:o_kr], wkr, _rot_rows(wkr, rope)], axis=0).astype(BF16)
    w_uk3 = w_uk[0].transpose(1, 0, 2).astype(BF16)
    w_uvt = w_uv[0].transpose(1, 2, 0).astype(BF16)
    w_ukt = w_uk[0].transpose(1, 2, 0).astype(BF16)
    w_uv3 = w_uv[0].transpose(1, 0, 2).astype(BF16)
    w_glu_b = w_glu[0].astype(BF16)
    w_bs_b = w_bs[0].astype(BF16)
    w_o_b = w_o[0].astype(BF16)
    w_out_b = w_out[0].astype(BF16)

    cs = jnp.concatenate([jnp.tile(_rope_tables(jnp.arange(t), rope), (nb, 1)),
                          jnp.tile(_rope_tables(past + jnp.arange(ts), rope), (ns, 1))], axis=0)

    c_all = jnp.concatenate([c_sample, c_prompt], axis=0)
    pad = (-c_all.shape[0]) % 8
    c_all = jnp.concatenate([c_all, jnp.zeros((pad, d), F32)], axis=0)
    mod = _mod_call(c_all, w_ada[0], b_ada[0], tn)
    mod3 = mod.reshape(mod.shape[0], 1, 3 * d)

    sb = max(1, min(ns, tmh // ts))
    h_all = _hmod_prompt(x_prompt, mod3, ns, m, tmh)
    h_all = _hmod_sample(x_sample, mod3, h_all, mp // (sb * ts), sb)

    u_all = _proj_call(h_all, w_u, None, F32, tm, tn, "proj_u")
    zazs = _proj_call(h_all, w_z, "silu", BF16, tm, tn, "proj_z")
    gates = _proj_call(h_all, w_g, "sigmoid", BF16, tm, tn, "proj_g")
    q_hm = _qproj_call(h_all, w_q3, cs, tm, nope, rope, scale)
    ckv_all, kr_all, ckv_b, kr_b = _ckv_call(h_all, w_ck, g_kv[0], cs, tmh, lora, rope)

    gblk = SSM_GROUP_BLOCK
    a_re, a_im, bb_re, bb_im = _s5_prep_call(lam_re[0], lam_im[0], log_dt[0],
                                             b_re[0].transpose(0, 2, 1), b_im[0].transpose(0, 2, 1))
    bcat = jnp.concatenate([_block_diag(bb_re, gblk), _block_diag(bb_im, gblk)], axis=-1)
    b2 = jnp.concatenate([bcat, bcat], axis=1).astype(BF16)
    ccat = jnp.concatenate([_block_diag(c_re[0].transpose(0, 2, 1), gblk),
                            _block_diag(-c_im[0].transpose(0, 2, 1), gblk)], axis=1).astype(BF16)
    ngb = ngrp // gblk
    sw = gblk * nstate
    amat = jnp.stack([a_re.reshape(ngb, sw), a_im.reshape(ngb, sw)], axis=1)
    s5w = (b2, ccat, amat, d_skip)
    tc = min(t, S5_TIME_CHUNK)
    pair = 2 if (2 * nb <= 8 and ngb % 2 == 0) else 1
    gkb = gblk * gch
    s5w_p = s5w if pair == 1 else (
        bcat.astype(BF16).reshape(ngb // 2, 2 * gkb, 2 * sw),
        ccat.reshape(ngb // 2, 2, 2 * sw, gkb).transpose(0, 2, 1, 3).reshape(ngb // 2, 2 * sw, 2 * gkb),
        amat.reshape(ngb // 2, 4, sw), d_skip)
    gy_p, sre_p, sim_p = _s5_prompt_call(u_all, s5w_p, nb, t, tc, pair)
    ssb = min(ns, 32)
    gy_s, sre_s, sim_s = _s5_sample_call(u_all, s5w, state_ssm_re[0].reshape(ns, ngrp * nstate),
                                         state_ssm_im[0].reshape(ns, ngrp * nstate), ns, ts, mp // (ssb * ts), ssb)
    gy = jnp.concatenate([gy_p.reshape(mp, d_ssm), gy_s], axis=0)

    k_hm, vt_hm = _kvup_call(ckv_b, kr_b, w_uk3, w_uvt, mp, tm, 2)
    tq = min(t, FLASH_BLOCK)
    att = _flash_call(q_hm, k_hm, vt_hm, nb, t, m, tq)
    q_lat = _qlat_call(q_hm, w_ukt, ms, mp // ms)
    q_rope_s = q_hm[:, mp:, nope:nope + rope].astype(F32)
    sq = DECODE_SEQS_PER_STEP
    krope_t = jnp.swapaxes(cache_krope, 2, 3).reshape(cache_krope.shape[1], rope, pg)
    o_lat = _decode_call(page_table, q_lat, q_rope_s, ckv_all, kr_all, cache_ckv.reshape(cache_ckv.shape[1:]), krope_t,
                         ns, ts, mp // (sq * ts), PAGES_PER_STEP, sq)
    att = _vup_call(o_lat, w_uv3, att, mp // ms)

    ys2 = _glu_call(gy, w_glu_b, zazs, tm, tn)
    merged = _merge_call(ys2, att, zazs, w_bs_b, w_o_b, gates, tm, tn)
    ntp = t // tmh
    y_p = _final_call(merged, w_out_b, x_prompt, mod3, ln_g[0], ln_b[0], alpha,
                      (1, tmh, d), (1, 1, d), lambda i, j: (ns + i, 0, 2), lambda i, j: (i * ntp + j, 0), (nb, ntp))
    y_s = _final_call(merged, w_out_b, x_sample, mod3, ln_g[0], ln_b[0], alpha,
                      (sb, ts, d), (sb, 1, d), lambda i: (i, 0, 2), lambda i: (mp // (sb * ts) + i, 0), (ns // sb,))

    st = lambda a, n: a.reshape(1, n, ngrp, nstate)
    return (y_p, y_s,
            ckv_all[:mp].reshape(1, nb, t, lora), kr_all[:mp].reshape(1, nb, t, rope), st(sre_p, nb), st(sim_p, nb),
            ckv_all[mp:].reshape(1, ns, ts, lora), kr_all[mp:].reshape(1, ns, ts, rope), st(sre_s, ns), st(sim_s, ns))
```

```python
import functools
import math

import jax
import jax.numpy as jnp
from jax import lax
from jax.experimental import pallas as pl
from jax.experimental.pallas import tpu as pltpu

F32 = jnp.float32
BF16 = jnp.bfloat16

LN_EPS = 1e-5
RMS_EPS = 1e-6
ROPE_BASE = 10000.0
NEG_BIG = -1e30
VMEM_LIMIT = 48 * 1024 * 1024
PAGES_PER_STEP = 16
DECODE_SEQS_PER_STEP = 2
DECODE_SLOTS = 3
LOG2E = 1.4426950408889634
SSM_GROUP_BLOCK = 8
LANES = 128
S5_TIME_CHUNK = 256
FLASH_BLOCK = 1024


def _cparams(sem):
    return pltpu.CompilerParams(dimension_semantics=sem, vmem_limit_bytes=VMEM_LIMIT)


def _sigmoid(x):
    return 1.0 / (1.0 + jnp.exp(-x))


def _silu(x):
    return x * _sigmoid(x)


def _dot(a, b):
    return jnp.dot(a, b, preferred_element_type=F32)


def _dot_nt(a, b):
    return lax.dot_general(a, b, (((1,), (1,)), ((), ())), preferred_element_type=F32)


def _mod_kernel(c_ref, w_ref, b_ref, o_ref):
    a = _silu(c_ref[...]).astype(BF16)
    o_ref[...] = _dot(a, w_ref[...].astype(BF16)) + b_ref[...]


def _mod_call(c_all, w_ada, b_ada, tn):
    r, d = c_all.shape
    n = w_ada.shape[1]
    return pl.pallas_call(
        _mod_kernel,
        grid=(n // tn,),
        in_specs=[pl.BlockSpec((r, d), lambda j: (0, 0)),
                  pl.BlockSpec((d, tn), lambda j: (0, j)),
                  pl.BlockSpec((1, tn), lambda j: (0, j))],
        out_specs=pl.BlockSpec((r, tn), lambda j: (0, j)),
        out_shape=jax.ShapeDtypeStruct((r, n), F32),
        compiler_params=_cparams(("arbitrary",)),
        name="mod",
    )(c_all, w_ada, b_ada.reshape(1, n))


def _hmod_kernel(x_ref, sh_ref, sc_ref, *rest):
    o_ref = rest[-1]
    h = x_ref[...] * (1.0 + sc_ref[...]) + sh_ref[...]
    o_ref[...] = h.reshape(o_ref.shape).astype(o_ref.dtype)


def _hmod_prompt_kernel(x_ref, sh_ref, sc_ref, o_ref, *, nb):
    @pl.when(pl.program_id(0) < nb)
    def _rows():
        _hmod_kernel(x_ref, sh_ref, sc_ref, o_ref)

    @pl.when(pl.program_id(0) == nb)
    def _tail():
        o_ref[...] = jnp.zeros_like(o_ref)


def _hmod_prompt(x, mod3, mod_row0, m_total, tm):
    b, t, d = x.shape
    nt = t // tm
    last = m_total // tm - 1
    assert b * nt + nt > last
    bc = lambda i: jnp.minimum(i, b - 1)
    return pl.pallas_call(
        functools.partial(_hmod_prompt_kernel, nb=b),
        grid=(b + 1, nt),
        in_specs=[pl.BlockSpec((1, tm, d), lambda i, j: (bc(i), jnp.where(i < b, j, nt - 1), 0)),
                  pl.BlockSpec((1, 1, d), lambda i, j: (mod_row0 + bc(i), 0, 0)),
                  pl.BlockSpec((1, 1, d), lambda i, j: (mod_row0 + bc(i), 0, 1))],
        out_specs=pl.BlockSpec((tm, d), lambda i, j: (jnp.minimum(i * nt + j, last), 0)),
        out_shape=jax.ShapeDtypeStruct((m_total, d), BF16),
        compiler_params=_cparams(("arbitrary", "arbitrary")),
        name="hmod_prompt",
    )(x, mod3, mod3)


def _hmod_sample(x, mod3, h_all, row_block0, sb):
    ns, ts, d = x.shape
    return pl.pallas_call(
        _hmod_kernel,
        grid=(ns // sb,),
        in_specs=[pl.BlockSpec((sb, ts, d), lambda i: (i, 0, 0)),
                  pl.BlockSpec((sb, 1, d), lambda i: (i, 0, 0)),
                  pl.BlockSpec((sb, 1, d), lambda i: (i, 0, 1)),
                  pl.BlockSpec(memory_space=pl.ANY)],
        out_specs=pl.BlockSpec((sb * ts, d), lambda i: (row_block0 + i, 0)),
        out_shape=jax.ShapeDtypeStruct(h_all.shape, h_all.dtype),
        input_output_aliases={3: 0},
        compiler_params=_cparams(("arbitrary",)),
        name="hmod_sample",
    )(x, mod3, mod3, h_all)


def _proj_kernel(h_ref, w_ref, o_ref, *, act):
    acc = _dot_nt(h_ref[...], w_ref[...])
    if act == "silu":
        acc = _silu(acc)
    elif act == "sigmoid":
        acc = _sigmoid(acc)
    o_ref[...] = acc.astype(o_ref.dtype)


def _proj_call(h, w, act, out_dtype, tm, tn, name):
    m, k = h.shape
    n = w.shape[0]
    return pl.pallas_call(
        functools.partial(_proj_kernel, act=act),
        grid=(m // tm, n // tn),
        in_specs=[pl.BlockSpec((tm, k), lambda i, j: (i, 0)),
                  pl.BlockSpec((tn, k), lambda i, j: (j, 0))],
        out_specs=pl.BlockSpec((tm, tn), lambda i, j: (i, j)),
        out_shape=jax.ShapeDtypeStruct((m, n), out_dtype),
        compiler_params=_cparams(("arbitrary", "arbitrary")),
        name=name,
    )(h, w)


def _qproj_kernel(h_ref, w_ref, cs_ref, o_ref, *, nope, rope, scale, hb):
    h = h_ref[...]
    cs = cs_ref[...]
    for hh in range(hb):
        acc = _dot_nt(h, w_ref[hh])
        t = acc[:, nope:] * cs
        r = t + pltpu.roll(t, rope, 1)
        lane = lax.broadcasted_iota(jnp.int32, r.shape, 1)
        r = jnp.where(lane < rope, r, 0.0)
        o_ref[hh, :, :nope] = (acc[:, :nope] * scale).astype(o_ref.dtype)
        o_ref[hh, :, nope:] = (r * scale).astype(o_ref.dtype)


def _qproj_call(h, w_q3, cs, tm, nope, rope, scale, hb=2):
    m, k = h.shape
    nh, wd, _ = w_q3.shape
    return pl.pallas_call(
        functools.partial(_qproj_kernel, nope=nope, rope=rope, scale=scale, hb=hb),
        grid=(m // tm, nh // hb),
        in_specs=[pl.BlockSpec((tm, k), lambda i, j: (i, 0)),
                  pl.BlockSpec((hb, wd, k), lambda i, j: (j, 0, 0)),
                  pl.BlockSpec((tm, 2 * rope), lambda i, j: (i, 0))],
        out_specs=pl.BlockSpec((hb, tm, wd), lambda i, j: (j, i, 0)),
        out_shape=jax.ShapeDtypeStruct((nh, m, wd), BF16),
        compiler_params=_cparams(("arbitrary", "arbitrary")),
        name="qproj",
    )(h, w_q3, cs)


def _ckv_kernel(h_ref, w_ref, g_ref, cs_ref, ckv_ref, kr_ref, ckvb_ref, krb_ref, *, lora, rope):
    acc = _dot_nt(h_ref[...], w_ref[...])
    c = acc[:, :lora]
    ms = jnp.mean(c * c, axis=-1, keepdims=True)
    ckv = c * lax.rsqrt(ms + RMS_EPS) * g_ref[...]
    ckv_ref[...] = ckv
    ckvb_ref[...] = ckv.astype(BF16)
    t = acc[:, lora:] * cs_ref[...]
    r = t + pltpu.roll(t, rope, 1)
    kr_ref[...] = r[:, :rope]
    lane = lax.broadcasted_iota(jnp.int32, r.shape, 1)
    krb_ref[...] = jnp.where(lane < rope, r, 0.0).astype(BF16)


def _ckv_call(h, w_ck, g_kv, cs, tm, lora, rope):
    m, k = h.shape
    wd = w_ck.shape[0]
    return pl.pallas_call(
        functools.partial(_ckv_kernel, lora=lora, rope=rope),
        grid=(m // tm,),
        in_specs=[pl.BlockSpec((tm, k), lambda i: (i, 0)),
                  pl.BlockSpec((wd, k), lambda i: (0, 0)),
                  pl.BlockSpec((1, lora), lambda i: (0, 0)),
                  pl.BlockSpec((tm, 2 * rope), lambda i: (i, 0))],
        out_specs=[pl.BlockSpec((tm, lora), lambda i: (i, 0)),
                   pl.BlockSpec((tm, rope), lambda i: (i, 0)),
                   pl.BlockSpec((tm, lora), lambda i: (i, 0)),
                   pl.BlockSpec((tm, 2 * rope), lambda i: (i, 0))],
        out_shape=[jax.ShapeDtypeStruct((m, lora), F32),
                   jax.ShapeDtypeStruct((m, rope), F32),
                   jax.ShapeDtypeStruct((m, lora), BF16),
                   jax.ShapeDtypeStruct((m, 2 * rope), BF16)],
        compiler_params=_cparams(("arbitrary",)),
        name="ckvproj",
    )(h, w_ck, g_kv.reshape(1, lora), cs)


def _s5_prep_kernel(lre_ref, lim_ref, ldt_ref, bre_ref, bim_ref, are_ref, aim_ref, bbre_ref, bbim_ref):
    lre = lre_ref[...]
    lim = lim_ref[...]
    dt = jnp.exp(ldt_ref[...])
    mag = jnp.exp(lre * dt)
    a_re = mag * jnp.cos(lim * dt)
    a_im = mag * jnp.sin(lim * dt)
    are_ref[...] = a_re
    aim_ref[...] = a_im
    den = lre * lre + lim * lim
    nr = a_re - 1.0
    c_re = (nr * lre + a_im * lim) / den
    c_im = (a_im * lre - nr * lim) / den
    bre = bre_ref[...]
    bim = bim_ref[...]
    bbre_ref[...] = c_re * bre - c_im * bim
    bbim_ref[...] = c_re * bim + c_im * bre


def _s5_prep_call(lam_re, lam_im, log_dt, b_re_t, b_im_t):
    g, p = lam_re.shape
    n = b_re_t.shape[1]
    a_re, a_im, bb_re, bb_im = pl.pallas_call(
        _s5_prep_kernel,
        out_shape=[jax.ShapeDtypeStruct((g, 1, p), F32), jax.ShapeDtypeStruct((g, 1, p), F32),
                   jax.ShapeDtypeStruct((g, n, p), F32), jax.ShapeDtypeStruct((g, n, p), F32)],
        name="s5prep",
    )(lam_re.reshape(g, 1, p), lam_im.reshape(g, 1, p), log_dt.reshape(g, 1, 1), b_re_t, b_im_t)
    return a_re.reshape(g, p), a_im.reshape(g, p), bb_re, bb_im


def _s5_kernel(*refs, n_u, tc, nreal, pair, sw, has_h0):
    u_refs = refs[:n_u]
    i = n_u
    b2_ref, c_ref, a_ref, d_ref = refs[i:i + 4]
    i += 4
    if has_h0:
        h0re_ref, h0im_ref = refs[i:i + 2]
        i += 2
    gy_ref, sre_ref, sim_ref = refs[i:i + 3]
    i += 3
    us_sc, ut_sc, bu_sc, y_sc, st_sc = refs[i:i + 5]
    nseq = nreal * pair
    gk = d_ref.shape[1] // pair
    assert tc % 8 == 0 and nseq % 8 == 0 and (pair == 1 or n_u == nreal)

    step = pl.program_id(1)

    @pl.when(step == 0)
    def _init():
        if has_h0:
            st_sc[:, :sw] = h0re_ref[...]
            st_sc[:, sw:] = h0im_ref[...]
        else:
            st_sc[...] = jnp.zeros_like(st_sc)

    def u_of(v):
        b, p = v % nreal, v // nreal
        return u_refs[b][:, p * gk:(p + 1) * gk] if n_u > 1 else u_refs[0][b * tc:(b + 1) * tc, :]

    nkb = gk // LANES
    nub = pair * nkb
    for v in range(nseq):
        p, uv = v // nreal, u_of(v)
        for kb in range(nub):
            own = kb // nkb == p
            us_sc[kb, v * tc:(v + 1) * tc, :] = (uv[:, (kb - p * nkb) * LANES:(kb - p * nkb + 1) * LANES] if own
                                                 else jnp.zeros((tc, LANES), F32))
    for sg in range(0, nseq, 8):
        def permute(t, c, base=sg * tc):
            dst = pl.multiple_of(base + t * 8, 8)
            for kb in range(nub):
                ut_sc[kb, pl.ds(dst, 8), :] = us_sc[kb, pl.ds(base + t, 8, stride=tc), :]
            return c

        lax.fori_loop(0, tc, permute, 0, unroll=8)
    up = jnp.concatenate([ut_sc[kb] for kb in range(nub)], axis=1)
    if pair == 1:
        u_hi = up.astype(BF16)
        u_lo = (up - u_hi.astype(F32)).astype(BF16)
        lhs = jnp.concatenate([u_hi, u_lo], axis=1)
    else:
        lhs = up.astype(BF16)
    bu = _dot(lhs, b2_ref[0])
    nj = sw // LANES
    for j in range(2 * nj):
        bu_sc[j] = bu[:, j * LANES:(j + 1) * LANES]

    for sg in range(0, nseq, 8):
        n8 = 8
        base = sg * tc
        def a_tile(r, j, sg=sg, n8=n8):
            ps = [v // nreal for v in range(sg, sg + n8)]
            runs = [(p, ps.count(p)) for p in sorted(set(ps))]
            parts = [jnp.broadcast_to(a_ref[0, 2 * p + r:2 * p + r + 1, j * LANES:(j + 1) * LANES], (cnt, LANES))
                     for p, cnt in runs]
            return parts[0] if len(parts) == 1 else jnp.concatenate(parts, axis=0)

        a_b = [a_tile(r, j) for r in range(2) for j in range(nj)]

        def body(i8, carry, base=base, a_b=a_b):
            h = list(carry)
            for k in range(8):
                rows = pl.ds(pl.multiple_of(base + (i8 * 8 + k) * 8, 8), 8)
                new = [None] * (2 * nj)
                for j in range(nj):
                    new[j] = a_b[j] * h[j] - a_b[nj + j] * h[nj + j] + bu_sc[j, rows, :]
                    new[nj + j] = a_b[j] * h[nj + j] + a_b[nj + j] * h[j] + bu_sc[nj + j, rows, :]
                for j in range(2 * nj):
                    bu_sc[j, rows, :] = new[j]
                h = new
            return tuple(h)

        init = tuple(st_sc[sg:sg + n8, j * LANES:(j + 1) * LANES] for j in range(2 * nj))
        fin = lax.fori_loop(0, tc // 8, body, init)
        for j in range(2 * nj):
            st_sc[sg:sg + n8, j * LANES:(j + 1) * LANES] = fin[j]

    hs = jnp.concatenate([bu_sc[j] for j in range(2 * nj)], axis=1)
    yp = _dot(hs.astype(BF16), c_ref[0])
    for jb in range(nub):
        y_sc[jb] = yp[:, jb * LANES:(jb + 1) * LANES]
    for v in range(nseq):
        b, p = v % nreal, v // nreal
        sg = (v // 8) * 8
        rows = pl.ds(sg * tc + (v - sg), tc, stride=8)
        y = jnp.concatenate([y_sc[p * nkb + jb, rows, :] for jb in range(nkb)], axis=1)
        y = y + d_ref[:, p * gk:(p + 1) * gk] * u_of(v)
        gy = (0.5 * y * (1.0 + lax.erf(y * (1.0 / math.sqrt(2.0))))).astype(gy_ref.dtype)
        if len(gy_ref.shape) == 3:
            gy_ref[b, :, p * gk:(p + 1) * gk] = gy
        else:
            gy_ref[b * tc:(b + 1) * tc, :] = gy

    @pl.when(step == pl.num_programs(1) - 1)
    def _fin():
        for p in range(pair):
            sre_ref[:, p * sw:(p + 1) * sw] = st_sc[p * nreal:(p + 1) * nreal, :sw]
            sim_ref[:, p * sw:(p + 1) * sw] = st_sc[p * nreal:(p + 1) * nreal, sw:]


def _s5_scratch(nseq, tc, gk, sw):
    rows = nseq * tc
    return [pltpu.VMEM((gk // LANES, rows, LANES), F32),
            pltpu.VMEM((gk // LANES, rows, LANES), F32),
            pltpu.VMEM((2 * sw // LANES, rows, LANES), F32),
            pltpu.VMEM((gk // LANES, rows, LANES), F32),
            pltpu.VMEM((nseq, 2 * sw), F32)]


def _s5_weight_specs(kin, gk, sw, pair, idx):
    return [pl.BlockSpec((1, kin, 2 * sw), lambda *a: (idx(*a), 0, 0)),
            pl.BlockSpec((1, 2 * sw, pair * gk), lambda *a: (idx(*a), 0, 0)),
            pl.BlockSpec((1, 2 * pair, sw), lambda *a: (idx(*a), 0, 0)),
            pl.BlockSpec((1, pair * gk), lambda *a: (0, idx(*a)))]


def _s5_prompt_call(u_all, wts, nb, t, tc, pair):
    bw, cmat, amat, dsk = wts
    ngp, kin, sw2 = bw.shape
    sw = sw2 // 2
    gk = cmat.shape[2] // pair
    ntc = t // tc
    gb_of = lambda g, s: g
    u_specs = [pl.BlockSpec((tc, pair * gk), functools.partial(lambda g, s, b: (b * ntc + s, g), b=b))
               for b in range(nb)]
    kern = functools.partial(_s5_kernel, n_u=nb, tc=tc, nreal=nb, pair=pair, sw=sw, has_h0=False)
    return pl.pallas_call(
        kern,
        grid=(ngp, ntc),
        in_specs=u_specs + _s5_weight_specs(kin, gk, sw, pair, gb_of),
        out_specs=[pl.BlockSpec((nb, tc, pair * gk), lambda g, s: (0, s, g)),
                   pl.BlockSpec((nb, pair * sw), lambda g, s: (0, g)),
                   pl.BlockSpec((nb, pair * sw), lambda g, s: (0, g))],
        out_shape=[jax.ShapeDtypeStruct((nb, t, ngp * pair * gk), BF16),
                   jax.ShapeDtypeStruct((nb, ngp * pair * sw), F32),
                   jax.ShapeDtypeStruct((nb, ngp * pair * sw), F32)],
        scratch_shapes=_s5_scratch(nb * pair, tc, pair * gk, sw),
        compiler_params=_cparams(("arbitrary", "arbitrary")),
        name="s5_prompt",
    )(*([u_all] * nb), bw, cmat, amat, dsk)


def _s5_sample_call(u_all, wts, h0_re, h0_im, ns, ts, row_block0, sb):
    b2, cmat, amat, dsk = wts
    ngb, gk2, sw2 = b2.shape
    gk, sw = gk2 // 2, sw2 // 2
    gb_of = lambda g, s, z: g
    kern = functools.partial(_s5_kernel, n_u=1, tc=ts, nreal=sb, pair=1, sw=sw, has_h0=True)
    return pl.pallas_call(
        kern,
        grid=(ngb, 1, ns // sb),
        in_specs=[pl.BlockSpec((sb * ts, gk), lambda g, s, z: (row_block0 + z, g))]
        + _s5_weight_specs(gk2, gk, sw, 1, gb_of)
        + [pl.BlockSpec((sb, sw), lambda g, s, z: (z, g)), pl.BlockSpec((sb, sw), lambda g, s, z: (z, g))],
        out_specs=[pl.BlockSpec((sb * ts, gk), lambda g, s, z: (z, g)),
                   pl.BlockSpec((sb, sw), lambda g, s, z: (z, g)),
                   pl.BlockSpec((sb, sw), lambda g, s, z: (z, g))],
        out_shape=[jax.ShapeDtypeStruct((ns * ts, ngb * gk), BF16),
                   jax.ShapeDtypeStruct((ns, ngb * sw), F32),
                   jax.ShapeDtypeStruct((ns, ngb * sw), F32)],
        scratch_shapes=_s5_scratch(sb, ts, gk, sw),
        compiler_params=_cparams(("arbitrary", "arbitrary", "arbitrary")),
        name="s5_sample",
    )(u_all, b2, cmat, amat, dsk, h0_re, h0_im)


def _flash_kernel(q_ref, c_ref, kr_ref, wk_ref, wvt_ref, o_ref, k_sc, vt_sc, m_sc, l_sc, acc_sc, *, tq, nope):
    qi = pl.program_id(2)

    @pl.when(qi == 0)
    def _keys_values():
        c = c_ref[...]
        k_sc[:, :nope] = _dot(c, wk_ref[0]).astype(k_sc.dtype)
        k_sc[:, nope:] = kr_ref[...]
        vt_sc[...] = _dot_nt(wvt_ref[0], c).astype(vt_sc.dtype)

    q = q_ref[0]
    m_sc[...] = jnp.full_like(m_sc, NEG_BIG)
    l_sc[...] = jnp.zeros_like(l_sc)
    acc_sc[...] = jnp.zeros_like(acc_sc)

    def block(ki, diagonal):
        start = pl.multiple_of(ki * tq, tq)
        s = _dot_nt(k_sc[pl.ds(start, tq), :], q)
        if diagonal:
            key = lax.broadcasted_iota(jnp.int32, s.shape, 0)
            qry = lax.broadcasted_iota(jnp.int32, s.shape, 1)
            s = jnp.where(key <= qry, s, NEG_BIG)
        m_prev = m_sc[...]
        m_new = jnp.maximum(m_prev, jnp.max(s, axis=0, keepdims=True))
        alpha = jnp.exp2(m_prev - m_new)
        p = jnp.exp2(s - m_new)
        l_sc[...] = alpha * l_sc[...] + jnp.sum(p, axis=0, keepdims=True)
        acc_sc[...] = alpha * acc_sc[...] + _dot(vt_sc[:, pl.ds(start, tq)], p.astype(BF16))
        m_sc[...] = m_new

    def below_diagonal(ki, carry):
        block(ki, False)
        return carry

    lax.fori_loop(0, qi, below_diagonal, 0)
    block(qi, True)
    o_ref[...] = (acc_sc[...] / l_sc[...]).T.astype(o_ref.dtype)


def _flash_outer_kernel(*refs, tq, nb, nope):
    o_ref = refs[5]

    @pl.when(pl.program_id(0) < nb)
    def _rows():
        _flash_kernel(*refs, tq=tq, nope=nope)

    @pl.when(pl.program_id(0) == nb)
    def _tail():
        o_ref[...] = jnp.zeros_like(o_ref)


def _flash_call(q_hm, ckv_b, kr_b, w_uk3, w_uvt, nb, t, m_total, tq):
    nh, _, qd = q_hm.shape
    lora, nope = w_uk3.shape[1:]
    vd = w_uvt.shape[1]
    rw = kr_b.shape[1]
    assert nope + rw == qd
    nq = t // tq
    last = m_total // tq - 1
    assert nb * nq + nq > last
    bc = lambda b: jnp.minimum(b, nb - 1)
    hc = lambda b, h: jnp.where(b < nb, h, nh - 1)
    return pl.pallas_call(
        functools.partial(_flash_outer_kernel, tq=tq, nb=nb, nope=nope),
        grid=(nb + 1, nh, nq),
        in_specs=[pl.BlockSpec((1, tq, qd), lambda b, h, i: (hc(b, h), bc(b) * nq + jnp.where(b < nb, i, nq - 1), 0)),
                  pl.BlockSpec((t, lora), lambda b, h, i: (bc(b), 0)),
                  pl.BlockSpec((t, rw), lambda b, h, i: (bc(b), 0)),
                  pl.BlockSpec((1, lora, nope), lambda b, h, i: (hc(b, h), 0, 0)),
                  pl.BlockSpec((1, vd, lora), lambda b, h, i: (hc(b, h), 0, 0))],
        out_specs=pl.BlockSpec((tq, vd), lambda b, h, i: (jnp.minimum(b * nq + i, last), h)),
        out_shape=jax.ShapeDtypeStruct((m_total, nh * vd), BF16),
        scratch_shapes=[pltpu.VMEM((t, qd), BF16), pltpu.VMEM((vd, t), BF16),
                        pltpu.VMEM((1, tq), F32), pltpu.VMEM((1, tq), F32), pltpu.VMEM((vd, tq), F32)],
        compiler_params=_cparams(("arbitrary", "arbitrary", "arbitrary")),
        name="flash",
    )(q_hm, ckv_b, kr_b, w_uk3, w_uvt)


def _qlat_kernel(q_ref, w_ref, o_ref):
    o_ref[0] = _dot(q_ref[0], w_ref[0])


def _qlat_call(q_hm, w_ukt, ms, row_block0):
    nh, nope, lora = w_ukt.shape
    return pl.pallas_call(
        _qlat_kernel,
        grid=(nh,),
        in_specs=[pl.BlockSpec((1, ms, nope), lambda h: (h, row_block0, 0)),
                  pl.BlockSpec((1, nope, lora), lambda h: (h, 0, 0))],
        out_specs=pl.BlockSpec((1, ms, lora), lambda h: (h, 0, 0)),
        out_shape=jax.ShapeDtypeStruct((nh, ms, lora), F32),
        compiler_params=_cparams(("arbitrary",)),
        name="qlat",
    )(q_hm, w_ukt)


def _decode_kernel(pt_ref, ql_ref, qr_ref, cn_ref, kn_ref, ck_hbm, kr_hbm, o_ref,
                   ck_buf, kr_buf, sem, qlb_sc, qrb_sc, kb_sc, rb_sc, m_sc, l_sc, acc_sc, *, npg, sq, ts, rope, n_pages):
    s_blk = pl.program_id(0)
    j = pl.program_id(1)
    nsteps = pl.num_programs(1)
    g = s_blk * nsteps + j
    total = pl.num_programs(0) * nsteps
    nslot = ck_buf.shape[0]
    ahead = nslot - 1
    slot = lax.rem(g, nslot)
    nh, _, lora = ql_ref.shape
    rows = nh * ts
    pg = ck_buf.shape[2]

    def page_copies(step):
        sb = lax.div(step, nsteps)
        jj = lax.rem(step, nsteps)
        sl = lax.rem(step, nslot)
        out = []
        for a in range(sq):
            for i in range(npg):
                page = pt_ref[(sb * sq + a) * n_pages + jj * npg + i]
                out.append(pltpu.make_async_copy(ck_hbm.at[page], ck_buf.at[sl, a * npg + i], sem.at[0, sl]))
                out.append(pltpu.make_async_copy(kr_hbm.at[page], kr_buf.at[sl, a * npg + i], sem.at[1, sl]))
        return out

    for k in range(ahead):
        @pl.when(jnp.logical_and(g == 0, k < total))
        def _prime(k=k):
            for c in page_copies(k):
                c.start()

    @pl.when(g + ahead < total)
    def _start_ahead():
        for c in page_copies(g + ahead):
            c.start()

    for c in page_copies(g):
        c.wait()
    ck_refs = [ck_buf.at[slot, i] for i in range(sq * npg)]
    kr_refs = [kr_buf.at[slot, i] for i in range(sq * npg)]

    @pl.when(j == 0)
    def _init():
        for a in range(sq):
            qlb_sc[a] = ql_ref[:, a * ts:(a + 1) * ts, :].reshape(rows, lora).astype(BF16)
            qrb_sc[a] = qr_ref[:, a * ts:(a + 1) * ts, :].reshape(rows, rope).astype(BF16)
        m_sc[...] = jnp.full_like(m_sc, NEG_BIG)
        l_sc[...] = jnp.zeros_like(l_sc)
        acc_sc[...] = jnp.zeros_like(acc_sc)

    def update(a, s, kb):
        m_prev = m_sc[a]
        m_new = jnp.maximum(m_prev, jnp.max(s, axis=-1, keepdims=True))
        alpha = jnp.exp2(m_prev - m_new)
        p = jnp.exp2(s - m_new)
        l_sc[a] = alpha * l_sc[a] + jnp.sum(p, axis=-1, keepdims=True)
        acc_sc[a] = alpha * acc_sc[a] + _dot(p.astype(BF16), kb)
        m_sc[a] = m_new

    for a in range(sq):
        for i in range(npg):
            kb_sc[a, i * pg:(i + 1) * pg, :] = ck_refs[a * npg + i][...].astype(BF16)
            rb_sc[a, :, i * pg:(i + 1) * pg] = kr_refs[a * npg + i][...].astype(BF16)
    for a in range(sq):
        kb = kb_sc[a]
        update(a, _dot_nt(qlb_sc[a], kb) + _dot(qrb_sc[a], rb_sc[a]), kb)

    @pl.when(j == pl.num_programs(1) - 1)
    def _fin():
        for a in range(sq):
            kn = jnp.concatenate([cn_ref[a * ts:(a + 1) * ts, :], jnp.zeros((pg - ts, lora), F32)], axis=0).astype(BF16)
            rn = jnp.concatenate([kn_ref[a * ts:(a + 1) * ts, :], jnp.zeros((pg - ts, rope), F32)], axis=0).astype(BF16)
            s = _dot_nt(qlb_sc[a], kn) + _dot_nt(qrb_sc[a], rn)
            tq = lax.broadcasted_iota(jnp.int32, s.shape, 0) % ts
            tk = lax.broadcasted_iota(jnp.int32, s.shape, 1)
            update(a, jnp.where(tk <= tq, s, NEG_BIG), kn)
            o = acc_sc[a] / l_sc[a]
            o_ref[:, a * ts:(a + 1) * ts, :] = o.reshape(nh, ts, lora)


def _decode_call(page_table, q_lat, q_rope, ckv_all, kr_all, cache_ckv, cache_krope_t, ns, ts, row_block0, npg, sq):
    nh, ms, lora = q_lat.shape
    rope = q_rope.shape[2]
    n_pages = page_table.shape[1]
    pg = cache_ckv.shape[1]
    nsteps = n_pages // npg
    rows = nh * ts

    n = sq * npg
    grid_spec = pltpu.PrefetchScalarGridSpec(
        num_scalar_prefetch=1,
        grid=(ns // sq, nsteps),
        in_specs=[pl.BlockSpec((nh, sq * ts, lora), lambda s, j, pt: (0, s, 0)),
                  pl.BlockSpec((nh, sq * ts, rope), lambda s, j, pt: (0, s, 0)),
                  pl.BlockSpec((sq * ts, lora), lambda s, j, pt: (row_block0 + s, 0)),
                  pl.BlockSpec((sq * ts, rope), lambda s, j, pt: (row_block0 + s, 0)),
                  pl.BlockSpec(memory_space=pl.ANY),
                  pl.BlockSpec(memory_space=pl.ANY)],
        out_specs=pl.BlockSpec((nh, sq * ts, lora), lambda s, j, pt: (0, s, 0)),
        scratch_shapes=[pltpu.VMEM((DECODE_SLOTS, n, pg, lora), F32), pltpu.VMEM((DECODE_SLOTS, n, rope, pg), F32),
                        pltpu.SemaphoreType.DMA((2, DECODE_SLOTS)),
                        pltpu.VMEM((sq, rows, lora), BF16), pltpu.VMEM((sq, rows, rope), BF16),
                        pltpu.VMEM((sq, npg * pg, lora), BF16), pltpu.VMEM((sq, rope, npg * pg), BF16),
                        pltpu.VMEM((sq, rows, 1), F32), pltpu.VMEM((sq, rows, 1), F32),
                        pltpu.VMEM((sq, rows, lora), F32)],
    )
    return pl.pallas_call(
        functools.partial(_decode_kernel, npg=npg, sq=sq, ts=ts, rope=rope, n_pages=n_pages),
        grid_spec=grid_spec,
        out_shape=jax.ShapeDtypeStruct((nh, ms, lora), F32),
        compiler_params=_cparams(("arbitrary", "arbitrary")),
        name="decode",
    )(page_table.reshape(-1), q_lat, q_rope, ckv_all, kr_all, cache_ckv, cache_krope_t)


def _vup_kernel(o_ref, w_ref, att_any, out_ref):
    del att_any
    out_ref[...] = _dot(o_ref[0].astype(BF16), w_ref[0]).astype(out_ref.dtype)


def _vup_call(o_lat, w_uv3, att_all, row_block0):
    nh, ms, lora = o_lat.shape
    vd = w_uv3.shape[2]
    return pl.pallas_call(
        _vup_kernel,
        grid=(nh,),
        in_specs=[pl.BlockSpec((1, ms, lora), lambda h: (h, 0, 0)),
                  pl.BlockSpec((1, lora, vd), lambda h: (h, 0, 0)),
                  pl.BlockSpec(memory_space=pl.ANY)],
        out_specs=pl.BlockSpec((ms, vd), lambda h: (row_block0, h)),
        out_shape=jax.ShapeDtypeStruct(att_all.shape, att_all.dtype),
        input_output_aliases={2: 0},
        compiler_params=_cparams(("arbitrary",)),
        name="vup",
    )(o_lat, w_uv3, att_all)


def _glu_kernel(g_ref, wa_ref, wb_ref, z_ref, o_ref):
    g = g_ref[...]
    ga = _dot(g, wa_ref[...])
    gb = _dot(g, wb_ref[...])
    o_ref[...] = (ga * _sigmoid(gb) * z_ref[...].astype(F32)).astype(o_ref.dtype)


def _glu_call(gy, w_glu, zs_act, tm, tn):
    m, k = gy.shape
    n = w_glu.shape[1] // 2
    nn = n // tn
    return pl.pallas_call(
        _glu_kernel,
        grid=(m // tm, nn),
        in_specs=[pl.BlockSpec((tm, k), lambda i, j: (i, 0)),
                  pl.BlockSpec((k, tn), lambda i, j: (0, j)),
                  pl.BlockSpec((k, tn), lambda i, j: (0, nn + j)),
                  pl.BlockSpec((tm, tn), lambda i, j: (i, j))],
        out_specs=pl.BlockSpec((tm, tn), lambda i, j: (i, j)),
        out_shape=jax.ShapeDtypeStruct((m, n), BF16),
        compiler_params=_cparams(("arbitrary", "arbitrary")),
        name="glu",
    )(gy, w_glu, w_glu, zs_act)


def _merge_kernel(ys_ref, att_ref, za_ref, wbs_ref, wo_ref, gs_ref, ga_ref, o_ref, v_sc):
    @pl.when(pl.program_id(1) == 0)
    def _gate():
        v_sc[...] = (att_ref[...].astype(F32) * za_ref[...].astype(F32)).astype(BF16)

    p_s = _dot(ys_ref[...], wbs_ref[...])
    p_a = _dot(v_sc[...], wo_ref[...])
    o_ref[...] = (gs_ref[...].astype(F32) * p_s + ga_ref[...].astype(F32) * p_a).astype(o_ref.dtype)


def _merge_call(ys2, att, za_act, w_bs, w_o, gates, tm, tn):
    m, ks = ys2.shape
    ka = att.shape[1]
    n = w_bs.shape[1]
    nn = n // tn
    return pl.pallas_call(
        _merge_kernel,
        grid=(m // tm, nn),
        in_specs=[pl.BlockSpec((tm, ks), lambda i, j: (i, 0)),
                  pl.BlockSpec((tm, ka), lambda i, j: (i, 0)),
                  pl.BlockSpec((tm, ka), lambda i, j: (i, 0)),
                  pl.BlockSpec((ks, tn), lambda i, j: (0, j)),
                  pl.BlockSpec((ka, tn), lambda i, j: (0, j)),
                  pl.BlockSpec((tm, tn), lambda i, j: (i, j)),
                  pl.BlockSpec((tm, tn), lambda i, j: (i, nn + j))],
        out_specs=pl.BlockSpec((tm, tn), lambda i, j: (i, j)),
        out_shape=jax.ShapeDtypeStruct((m, n), BF16),
        scratch_shapes=[pltpu.VMEM((tm, ka), BF16)],
        compiler_params=_cparams(("arbitrary", "arbitrary")),
        name="merge",
    )(ys2, att, za_act, w_bs, w_o, gates, gates)


def _final_kernel(mg_ref, w_ref, x_ref, gate_ref, g_ref, b_ref, o_ref, *, alpha):
    out = _dot(mg_ref[...], w_ref[...])
    x = x_ref[...]
    y = alpha * x + gate_ref[...] * out.reshape(x.shape)
    mu = jnp.mean(y, axis=-1, keepdims=True)
    yc = y - mu
    var = jnp.mean(yc * yc, axis=-1, keepdims=True)
    o_ref[...] = yc * lax.rsqrt(var + LN_EPS) * g_ref[...] + b_ref[...]


def _final_call(merged, w_out, x, mod3, ln_g, ln_b, alpha, x_blk, gate_blk, gate_map, merged_map, grid):
    d = w_out.shape[0]
    rows = x_blk[0] * x_blk[1]
    nd = len(grid)
    x_map = (lambda i, j: (i, j, 0)) if nd == 2 else (lambda i: (i, 0, 0))
    const2 = (lambda i, j: (0, 0)) if nd == 2 else (lambda i: (0, 0))
    const3 = (lambda i, j: (0, 0, 0)) if nd == 2 else (lambda i: (0, 0, 0))
    return pl.pallas_call(
        functools.partial(_final_kernel, alpha=alpha),
        grid=grid,
        in_specs=[pl.BlockSpec((rows, d), merged_map),
                  pl.BlockSpec((d, d), const2),
                  pl.BlockSpec(x_blk, x_map),
                  pl.BlockSpec(gate_blk, gate_map),
                  pl.BlockSpec((1, 1, d), const3),
                  pl.BlockSpec((1, 1, d), const3)],
        out_specs=pl.BlockSpec(x_blk, x_map),
        out_shape=jax.ShapeDtypeStruct(x.shape, F32),
        compiler_params=_cparams(("arbitrary",) * nd),
        name="final",
    )(merged, w_out, x, mod3, ln_g.reshape(1, 1, d), ln_b.reshape(1, 1, d))


def _rope_tables(pos, rope):
    freqs = ROPE_BASE ** (-jnp.arange(0, rope, 2, dtype=F32) / rope)
    ang = pos.astype(F32)[:, None] * freqs[None, :]
    cos, sin = jnp.cos(ang), jnp.sin(ang)
    return jnp.concatenate([cos, cos, sin, sin], axis=-1)


def _rot_rows(wt, rope):
    half = rope // 2
    return jnp.concatenate([-wt[..., half:, :], wt[..., :half, :]], axis=-2)


def _block_diag(x, gblk):
    g, r, c = x.shape
    x4 = x.reshape(g // gblk, gblk, r, c)
    eye = jnp.eye(gblk, dtype=x.dtype)
    return jnp.einsum("bgrc,gh->bgrhc", x4, eye).reshape(g // gblk, gblk * r, gblk * c)


def kernel(x_prompt, x_sample, c_prompt, c_sample, cache_ckv, cache_krope, state_ssm_re, state_ssm_im, page_table, w_ada, b_ada, w_in, g_kv, w_uk, w_uv, w_o, lam_re, lam_im, log_dt, b_re, b_im, c_re, c_im, d_skip, w_glu, w_bs, w_out, ln_g, ln_b):
    nb, t, d = x_prompt.shape
    ns, ts, _ = x_sample.shape
    depth = w_in.shape[0]
    assert depth == 1, "single-layer step"
    lora, nh, nope = w_uk.shape[1:]
    vd = w_uv.shape[3]
    rope = cache_krope.shape[3]
    d_ssm = d_skip.shape[1]
    ngrp, nstate, gch = b_re.shape[1:]
    d_attn = nh * vd
    pg = cache_ckv.shape[2]
    past = page_table.shape[1] * pg
    mp, ms = nb * t, ns * ts
    m = mp + ms
    scale = float((nope + rope) ** -0.5) * LOG2E
    alpha = float((2 * depth) ** 0.25)

    tm = math.gcd(math.gcd(mp, ms), 1024)
    tmh = min(tm, 512)
    tn = 512
    assert rope * 2 == 128 and nope == 128 and vd == 128

    wt = jnp.swapaxes(w_in, 1, 2)[0]
    o_u, o_zs, o_q = 0, d_ssm, 2 * d_ssm
    o_ckv = o_q + nh * (nope + rope)
    o_kr = o_ckv + lora
    o_za = o_kr + rope
    o_gs = o_za + d_attn
    o_ga = o_gs + d
    w_u = wt[o_u:o_zs].astype(BF16)
    w_za = wt[o_za:o_gs].astype(BF16)
    w_zs = wt[o_zs:o_q].astype(BF16)
    w_g = wt[o_gs:].astype(BF16)
    wq = wt[o_q:o_ckv].reshape(nh, nope + rope, d)
    w_q3 = jnp.concatenate([wq, _rot_rows(wq[:, nope:, :], rope)], axis=1).astype(BF16)
    wkr = wt[o_kr:o_za]
    w_ck = jnp.concatenate([wt[o_ckv:o_kr], wkr, _rot_rows(wkr, rope)], axis=0).astype(BF16)
    w_uk3 = w_uk[0].transpose(1, 0, 2).astype(BF16)
    w_uvt = w_uv[0].transpose(1, 2, 0).astype(BF16)
    w_ukt = w_uk[0].transpose(1, 2, 0).astype(BF16)
    w_uv3 = w_uv[0].transpose(1, 0, 2).astype(BF16)
    w_glu_b = w_glu[0].astype(BF16)
    w_bs_b = w_bs[0].astype(BF16)
    w_o_b = w_o[0].astype(BF16)
    w_out_b = w_out[0].astype(BF16)

    cs = jnp.concatenate([jnp.tile(_rope_tables(jnp.arange(t), rope), (nb, 1)),
                          jnp.tile(_rope_tables(past + jnp.arange(ts), rope), (ns, 1))], axis=0)

    c_all = jnp.concatenate([c_sample, c_prompt], axis=0)
    pad = (-c_all.shape[0]) % 8
    c_all = jnp.concatenate([c_all, jnp.zeros((pad, d), F32)], axis=0)
    mod = _mod_call(c_all, w_ada[0], b_ada[0], tn)
    mod3 = mod.reshape(mod.shape[0], 1, 3 * d)

    sb = max(1, min(ns, tmh // ts))
    h_all = _hmod_prompt(x_prompt, mod3, ns, m, tmh)
    h_all = _hmod_sample(x_sample, mod3, h_all, mp // (sb * ts), sb)

    u_all = _proj_call(h_all, w_u, None, F32, tm, tn, "proj_u")
    za_act = _proj_call(h_all, w_za, "silu", BF16, tm, 2 * tn, "proj_za")
    zs_act = _proj_call(h_all, w_zs, "silu", BF16, tm, 2 * tn, "proj_zs")
    gates = _proj_call(h_all, w_g, "sigmoid", BF16, tm, 2 * tn, "proj_g")
    q_hm = _qproj_call(h_all, w_q3, cs, tm, nope, rope, scale)
    ckv_all, kr_all, ckv_b, kr_b = _ckv_call(h_all, w_ck, g_kv[0], cs, tmh, lora, rope)

    gblk = SSM_GROUP_BLOCK
    a_re, a_im, bb_re, bb_im = _s5_prep_call(lam_re[0], lam_im[0], log_dt[0],
                                             b_re[0].transpose(0, 2, 1), b_im[0].transpose(0, 2, 1))
    bcat = jnp.concatenate([_block_diag(bb_re, gblk), _block_diag(bb_im, gblk)], axis=-1)
    b2 = jnp.concatenate([bcat, bcat], axis=1).astype(BF16)
    ccat = jnp.concatenate([_block_diag(c_re[0].transpose(0, 2, 1), gblk),
                            _block_diag(-c_im[0].transpose(0, 2, 1), gblk)], axis=1).astype(BF16)
    ngb = ngrp // gblk
    sw = gblk * nstate
    amat = jnp.stack([a_re.reshape(ngb, sw), a_im.reshape(ngb, sw)], axis=1)
    s5w = (b2, ccat, amat, d_skip)
    tc = min(t, S5_TIME_CHUNK)
    pair = 2 if (2 * nb <= 8 and ngb % 2 == 0) else 1
    gkb = gblk * gch
    s5w_p = s5w if pair == 1 else (
        bcat.astype(BF16).reshape(ngb // 2, 2 * gkb, 2 * sw),
        ccat.reshape(ngb // 2, 2, 2 * sw, gkb).transpose(0, 2, 1, 3).reshape(ngb // 2, 2 * sw, 2 * gkb),
        amat.reshape(ngb // 2, 4, sw), d_skip)
    gy_p, sre_p, sim_p = _s5_prompt_call(u_all, s5w_p, nb, t, tc, pair)
    ssb = min(ns, 32)
    gy_s, sre_s, sim_s = _s5_sample_call(u_all, s5w, state_ssm_re[0].reshape(ns, ngrp * nstate),
                                         state_ssm_im[0].reshape(ns, ngrp * nstate), ns, ts, mp // (ssb * ts), ssb)
    gy = jnp.concatenate([gy_p.reshape(mp, d_ssm), gy_s], axis=0)

    tq = min(t, FLASH_BLOCK)
    att = _flash_call(q_hm, ckv_b, kr_b, w_uk3, w_uvt, nb, t, m, tq)
    q_lat = _qlat_call(q_hm, w_ukt, ms, mp // ms)
    q_rope_s = q_hm[:, mp:, nope:nope + rope].astype(F32)
    sq = DECODE_SEQS_PER_STEP
    krope_t = jnp.swapaxes(cache_krope, 2, 3).reshape(cache_krope.shape[1], rope, pg)
    o_lat = _decode_call(page_table, q_lat, q_rope_s, ckv_all, kr_all, cache_ckv.reshape(cache_ckv.shape[1:]), krope_t,
                         ns, ts, mp // (sq * ts), PAGES_PER_STEP, sq)
    att = _vup_call(o_lat, w_uv3, att, mp // ms)

    ys2 = _glu_call(gy, w_glu_b, zs_act, tm, tn)
    merged = _merge_call(ys2, att, za_act, w_bs_b, w_o_b, gates, tm, tn)
    ntp = t // tmh
    y_p = _final_call(merged, w_out_b, x_prompt, mod3, ln_g[0], ln_b[0], alpha,
                      (1, tmh, d), (1, 1, d), lambda i, j: (ns + i, 0, 2), lambda i, j: (i * ntp + j, 0), (nb, ntp))
    y_s = _final_call(merged, w_out_b, x_sample, mod3, ln_g[0], ln_b[0], alpha,
                      (sb, ts, d), (sb, 1, d), lambda i: (i, 0, 2), lambda i: (mp // (sb * ts) + i, 0), (ns // sb,))

    st = lambda a, n: a.reshape(1, n, ngrp, nstate)
    return (y_p, y_s,
            ckv_all[:mp].reshape(1, nb, t, lora), kr_all[:mp].reshape(1, nb, t, rope), st(sre_p, nb), st(sim_p, nb),
            ckv_all[mp:].reshape(1, ns, ts, lora), kr_all[mp:].reshape(1, ns, ts, rope), st(sre_s, ns), st(sim_s, ns))
```

```python
import functools
import math

import jax
import jax.numpy as jnp
from jax import lax
from jax.experimental import pallas as pl
from jax.experimental.pallas import tpu as pltpu

F32 = jnp.float32
BF16 = jnp.bfloat16

LN_EPS = 1e-5
RMS_EPS = 1e-6
ROPE_BASE = 10000.0
NEG_BIG = -1e30
VMEM_LIMIT = 48 * 1024 * 1024
FLASH_VMEM_LIMIT = 56 * 1024 * 1024
PAGES_PER_STEP = 16
DECODE_SEQS_PER_STEP = 2
DECODE_SLOTS = 3
LOG2E = 1.4426950408889634
SSM_GROUP_BLOCK = 8
LANES = 128
S5_TIME_CHUNK = 256
FLASH_BLOCK = 1024


def _cparams(sem):
    return pltpu.CompilerParams(dimension_semantics=sem, vmem_limit_bytes=VMEM_LIMIT)


def _sigmoid(x):
    return 1.0 / (1.0 + jnp.exp(-x))


def _silu(x):
    return x * _sigmoid(x)


def _dot(a, b):
    return jnp.dot(a, b, preferred_element_type=F32)


def _dot_nt(a, b):
    return lax.dot_general(a, b, (((1,), (1,)), ((), ())), preferred_element_type=F32)


def _mod_kernel(c_ref, w_ref, b_ref, o_ref):
    a = _silu(c_ref[...]).astype(BF16)
    o_ref[...] = _dot(a, w_ref[...].astype(BF16)) + b_ref[...]


def _mod_call(c_all, w_ada, b_ada, tn):
    r, d = c_all.shape
    n = w_ada.shape[1]
    return pl.pallas_call(
        _mod_kernel,
        grid=(n // tn,),
        in_specs=[pl.BlockSpec((r, d), lambda j: (0, 0)),
                  pl.BlockSpec((d, tn), lambda j: (0, j)),
                  pl.BlockSpec((1, tn), lambda j: (0, j))],
        out_specs=pl.BlockSpec((r, tn), lambda j: (0, j)),
        out_shape=jax.ShapeDtypeStruct((r, n), F32),
        compiler_params=_cparams(("arbitrary",)),
        name="mod",
    )(c_all, w_ada, b_ada.reshape(1, n))


def _hmod_kernel(x_ref, sh_ref, sc_ref, *rest):
    o_ref = rest[-1]
    h = x_ref[...] * (1.0 + sc_ref[...]) + sh_ref[...]
    o_ref[...] = h.reshape(o_ref.shape).astype(o_ref.dtype)


def _hmod_prompt_kernel(x_ref, sh_ref, sc_ref, o_ref, *, nb):
    @pl.when(pl.program_id(0) < nb)
    def _rows():
        _hmod_kernel(x_ref, sh_ref, sc_ref, o_ref)

    @pl.when(pl.program_id(0) == nb)
    def _tail():
        o_ref[...] = jnp.zeros_like(o_ref)


def _hmod_prompt(x, mod3, mod_row0, m_total, tm):
    b, t, d = x.shape
    nt = t // tm
    last = m_total // tm - 1
    assert b * nt + nt > last
    bc = lambda i: jnp.minimum(i, b - 1)
    return pl.pallas_call(
        functools.partial(_hmod_prompt_kernel, nb=b),
        grid=(b + 1, nt),
        in_specs=[pl.BlockSpec((1, tm, d), lambda i, j: (bc(i), jnp.where(i < b, j, nt - 1), 0)),
                  pl.BlockSpec((1, 1, d), lambda i, j: (mod_row0 + bc(i), 0, 0)),
                  pl.BlockSpec((1, 1, d), lambda i, j: (mod_row0 + bc(i), 0, 1))],
        out_specs=pl.BlockSpec((tm, d), lambda i, j: (jnp.minimum(i * nt + j, last), 0)),
        out_shape=jax.ShapeDtypeStruct((m_total, d), BF16),
        compiler_params=_cparams(("arbitrary", "arbitrary")),
        name="hmod_prompt",
    )(x, mod3, mod3)


def _hmod_sample(x, mod3, h_all, row_block0, sb):
    ns, ts, d = x.shape
    return pl.pallas_call(
        _hmod_kernel,
        grid=(ns // sb,),
        in_specs=[pl.BlockSpec((sb, ts, d), lambda i: (i, 0, 0)),
                  pl.BlockSpec((sb, 1, d), lambda i: (i, 0, 0)),
                  pl.BlockSpec((sb, 1, d), lambda i: (i, 0, 1)),
                  pl.BlockSpec(memory_space=pl.ANY)],
        out_specs=pl.BlockSpec((sb * ts, d), lambda i: (row_block0 + i, 0)),
        out_shape=jax.ShapeDtypeStruct(h_all.shape, h_all.dtype),
        input_output_aliases={3: 0},
        compiler_params=_cparams(("arbitrary",)),
        name="hmod_sample",
    )(x, mod3, mod3, h_all)


def _proj_kernel(h_ref, w_ref, o_ref, *, act):
    acc = _dot_nt(h_ref[...], w_ref[...])
    if act == "silu":
        acc = _silu(acc)
    elif act == "sigmoid":
        acc = _sigmoid(acc)
    o_ref[...] = acc.astype(o_ref.dtype)


def _proj_call(h, w, act, out_dtype, tm, tn, name):
    m, k = h.shape
    n = w.shape[0]
    return pl.pallas_call(
        functools.partial(_proj_kernel, act=act),
        grid=(m // tm, n // tn),
        in_specs=[pl.BlockSpec((tm, k), lambda i, j: (i, 0)),
                  pl.BlockSpec((tn, k), lambda i, j: (j, 0))],
        out_specs=pl.BlockSpec((tm, tn), lambda i, j: (i, j)),
        out_shape=jax.ShapeDtypeStruct((m, n), out_dtype),
        compiler_params=_cparams(("arbitrary", "arbitrary")),
        name=name,
    )(h, w)


def _qproj_kernel(h_ref, w_ref, cs_ref, o_ref, *, nope, rope, scale, hb):
    half = rope // 2
    acc = _dot_nt(h_ref[...], w_ref[0])
    cos = cs_ref[:, :LANES]
    sin = cs_ref[:, LANES:]
    lane = lax.broadcasted_iota(jnp.int32, cos.shape, 1)
    first_half = (lane & half) == 0
    for pr in range(hb // 2):
        x = acc[:, hb * nope + pr * LANES:hb * nope + (pr + 1) * LANES]
        rot = jnp.where(first_half, -pltpu.roll(x, LANES - half, 1), pltpu.roll(x, half, 1))
        r = (x * cos + rot * sin) * scale
        for e, rr in ((2 * pr, r), (2 * pr + 1, pltpu.roll(r, rope, 1))):
            o_ref[e, :, :nope] = (acc[:, e * nope:(e + 1) * nope] * scale).astype(o_ref.dtype)
            o_ref[e, :, nope:] = jnp.where(lane < rope, rr, 0.0).astype(o_ref.dtype)


def _qproj_call(h, w_q4, cs4, tm, nh, nope, rope, scale, hb):
    m, k = h.shape
    wd = w_q4.shape[1]
    assert 2 * rope == LANES and nope == LANES and hb % 2 == 0 and wd == hb * (nope + rope)
    return pl.pallas_call(
        functools.partial(_qproj_kernel, nope=nope, rope=rope, scale=scale, hb=hb),
        grid=(m // tm, nh // hb),
        in_specs=[pl.BlockSpec((tm, k), lambda i, j: (i, 0)),
                  pl.BlockSpec((1, wd, k), lambda i, j: (j, 0, 0)),
                  pl.BlockSpec((tm, 2 * LANES), lambda i, j: (i, 0))],
        out_specs=pl.BlockSpec((hb, tm, nope + LANES), lambda i, j: (j, i, 0)),
        out_shape=jax.ShapeDtypeStruct((nh, m, nope + LANES), BF16),
        compiler_params=_cparams(("arbitrary", "arbitrary")),
        name="qproj",
    )(h, w_q4, cs4)


def _ckv_kernel(h_ref, w_ref, g_ref, cs_ref, ckv_ref, kr_ref, ckvb_ref, krb_ref, *, lora, rope):
    acc = _dot_nt(h_ref[...], w_ref[...])
    c = acc[:, :lora]
    ms = jnp.mean(c * c, axis=-1, keepdims=True)
    ckv = c * lax.rsqrt(ms + RMS_EPS) * g_ref[...]
    ckv_ref[...] = ckv
    ckvb_ref[...] = ckv.astype(BF16)
    t = acc[:, lora:] * cs_ref[...]
    r = t + pltpu.roll(t, rope, 1)
    kr_ref[...] = r[:, :rope]
    lane = lax.broadcasted_iota(jnp.int32, r.shape, 1)
    krb_ref[...] = jnp.where(lane < rope, r, 0.0).astype(BF16)


def _ckv_call(h, w_ck, g_kv, cs, tm, lora, rope):
    m, k = h.shape
    wd = w_ck.shape[0]
    return pl.pallas_call(
        functools.partial(_ckv_kernel, lora=lora, rope=rope),
        grid=(m // tm,),
        in_specs=[pl.BlockSpec((tm, k), lambda i: (i, 0)),
                  pl.BlockSpec((wd, k), lambda i: (0, 0)),
                  pl.BlockSpec((1, lora), lambda i: (0, 0)),
                  pl.BlockSpec((tm, 2 * rope), lambda i: (i, 0))],
        out_specs=[pl.BlockSpec((tm, lora), lambda i: (i, 0)),
                   pl.BlockSpec((tm, rope), lambda i: (i, 0)),
                   pl.BlockSpec((tm, lora), lambda i: (i, 0)),
                   pl.BlockSpec((tm, 2 * rope), lambda i: (i, 0))],
        out_shape=[jax.ShapeDtypeStruct((m, lora), F32),
                   jax.ShapeDtypeStruct((m, rope), F32),
                   jax.ShapeDtypeStruct((m, lora), BF16),
                   jax.ShapeDtypeStruct((m, 2 * rope), BF16)],
        compiler_params=_cparams(("arbitrary",)),
        name="ckvproj",
    )(h, w_ck, g_kv.reshape(1, lora), cs)


def _s5_prep_kernel(lre_ref, lim_ref, ldt_ref, bre_ref, bim_ref, are_ref, aim_ref, bbre_ref, bbim_ref):
    lre = lre_ref[...]
    lim = lim_ref[...]
    dt = jnp.exp(ldt_ref[...])
    mag = jnp.exp(lre * dt)
    a_re = mag * jnp.cos(lim * dt)
    a_im = mag * jnp.sin(lim * dt)
    are_ref[...] = a_re
    aim_ref[...] = a_im
    den = lre * lre + lim * lim
    nr = a_re - 1.0
    c_re = (nr * lre + a_im * lim) / den
    c_im = (a_im * lre - nr * lim) / den
    bre = bre_ref[...]
    bim = bim_ref[...]
    bbre_ref[...] = c_re * bre - c_im * bim
    bbim_ref[...] = c_re * bim + c_im * bre


def _s5_prep_call(lam_re, lam_im, log_dt, b_re_t, b_im_t):
    g, p = lam_re.shape
    n = b_re_t.shape[1]
    a_re, a_im, bb_re, bb_im = pl.pallas_call(
        _s5_prep_kernel,
        out_shape=[jax.ShapeDtypeStruct((g, 1, p), F32), jax.ShapeDtypeStruct((g, 1, p), F32),
                   jax.ShapeDtypeStruct((g, n, p), F32), jax.ShapeDtypeStruct((g, n, p), F32)],
        name="s5prep",
    )(lam_re.reshape(g, 1, p), lam_im.reshape(g, 1, p), log_dt.reshape(g, 1, 1), b_re_t, b_im_t)
    return a_re.reshape(g, p), a_im.reshape(g, p), bb_re, bb_im


def _s5_kernel(*refs, n_u, tc, nreal, pair, sw, has_h0):
    u_refs = refs[:n_u]
    i = n_u
    b2_ref, c_ref, a_ref, d_ref = refs[i:i + 4]
    i += 4
    if has_h0:
        h0re_ref, h0im_ref = refs[i:i + 2]
        i += 2
    gy_ref, sre_ref, sim_ref = refs[i:i + 3]
    i += 3
    us_sc, ut_sc, bu_sc, y_sc, st_sc = refs[i:i + 5]
    nseq = nreal * pair
    gk = d_ref.shape[1] // pair
    assert tc % 8 == 0 and nseq % 8 == 0 and (pair == 1 or n_u == nreal)

    step = pl.program_id(1)

    @pl.when(step == 0)
    def _init():
        if has_h0:
            st_sc[:, :sw] = h0re_ref[...]
            st_sc[:, sw:] = h0im_ref[...]
        else:
            st_sc[...] = jnp.zeros_like(st_sc)

    def u_of(v):
        b, p = v % nreal, v // nreal
        return u_refs[b][:, p * gk:(p + 1) * gk] if n_u > 1 else u_refs[0][b * tc:(b + 1) * tc, :]

    nkb = gk // LANES
    nub = pair * nkb
    for v in range(nseq):
        p, uv = v // nreal, u_of(v)
        for kb in range(nub):
            own = kb // nkb == p
            us_sc[kb, v * tc:(v + 1) * tc, :] = (uv[:, (kb - p * nkb) * LANES:(kb - p * nkb + 1) * LANES] if own
                                                 else jnp.zeros((tc, LANES), F32))
    for sg in range(0, nseq, 8):
        def permute(t, c, base=sg * tc):
            dst = pl.multiple_of(base + t * 8, 8)
            for kb in range(nub):
                ut_sc[kb, pl.ds(dst, 8), :] = us_sc[kb, pl.ds(base + t, 8, stride=tc), :]
            return c

        lax.fori_loop(0, tc, permute, 0, unroll=8)
    up = jnp.concatenate([ut_sc[kb] for kb in range(nub)], axis=1)
    if pair == 1:
        u_hi = up.astype(BF16)
        u_lo = (up - u_hi.astype(F32)).astype(BF16)
        lhs = jnp.concatenate([u_hi, u_lo], axis=1)
    else:
        lhs = up.astype(BF16)
    bu = _dot(lhs, b2_ref[0])
    nj = sw // LANES
    for j in range(2 * nj):
        bu_sc[j] = bu[:, j * LANES:(j + 1) * LANES]

    for sg in range(0, nseq, 8):
        n8 = 8
        base = sg * tc
        def a_tile(r, j, sg=sg, n8=n8):
            ps = [v // nreal for v in range(sg, sg + n8)]
            runs = [(p, ps.count(p)) for p in sorted(set(ps))]
            parts = [jnp.broadcast_to(a_ref[0, 2 * p + r:2 * p + r + 1, j * LANES:(j + 1) * LANES], (cnt, LANES))
                     for p, cnt in runs]
            return parts[0] if len(parts) == 1 else jnp.concatenate(parts, axis=0)

        a_b = [a_tile(r, j) for r in range(2) for j in range(nj)]

        def body(i8, carry, base=base, a_b=a_b):
            h = list(carry)
            for k in range(8):
                rows = pl.ds(pl.multiple_of(base + (i8 * 8 + k) * 8, 8), 8)
                new = [None] * (2 * nj)
                for j in range(nj):
                    new[j] = a_b[j] * h[j] - a_b[nj + j] * h[nj + j] + bu_sc[j, rows, :]
                    new[nj + j] = a_b[j] * h[nj + j] + a_b[nj + j] * h[j] + bu_sc[nj + j, rows, :]
                for j in range(2 * nj):
                    bu_sc[j, rows, :] = new[j]
                h = new
            return tuple(h)

        init = tuple(st_sc[sg:sg + n8, j * LANES:(j + 1) * LANES] for j in range(2 * nj))
        fin = lax.fori_loop(0, tc // 8, body, init)
        for j in range(2 * nj):
            st_sc[sg:sg + n8, j * LANES:(j + 1) * LANES] = fin[j]

    hs = jnp.concatenate([bu_sc[j] for j in range(2 * nj)], axis=1)
    yp = _dot(hs.astype(BF16), c_ref[0])
    for jb in range(nub):
        y_sc[jb] = yp[:, jb * LANES:(jb + 1) * LANES]
    for v in range(nseq):
        b, p = v % nreal, v // nreal
        sg = (v // 8) * 8
        rows = pl.ds(sg * tc + (v - sg), tc, stride=8)
        y = jnp.concatenate([y_sc[p * nkb + jb, rows, :] for jb in range(nkb)], axis=1)
        y = y + d_ref[:, p * gk:(p + 1) * gk] * u_of(v)
        gy = (0.5 * y * (1.0 + lax.erf(y * (1.0 / math.sqrt(2.0))))).astype(gy_ref.dtype)
        if len(gy_ref.shape) == 3:
            gy_ref[b, :, p * gk:(p + 1) * gk] = gy
        else:
            gy_ref[b * tc:(b + 1) * tc, :] = gy

    @pl.when(step == pl.num_programs(1) - 1)
    def _fin():
        for p in range(pair):
            sre_ref[:, p * sw:(p + 1) * sw] = st_sc[p * nreal:(p + 1) * nreal, :sw]
            sim_ref[:, p * sw:(p + 1) * sw] = st_sc[p * nreal:(p + 1) * nreal, sw:]


def _s5_scratch(nseq, tc, gk, sw):
    rows = nseq * tc
    return [pltpu.VMEM((gk // LANES, rows, LANES), F32),
            pltpu.VMEM((gk // LANES, rows, LANES), F32),
            pltpu.VMEM((2 * sw // LANES, rows, LANES), F32),
            pltpu.VMEM((gk // LANES, rows, LANES), F32),
            pltpu.VMEM((nseq, 2 * sw), F32)]


def _s5_weight_specs(kin, gk, sw, pair, idx):
    return [pl.BlockSpec((1, kin, 2 * sw), lambda *a: (idx(*a), 0, 0)),
            pl.BlockSpec((1, 2 * sw, pair * gk), lambda *a: (idx(*a), 0, 0)),
            pl.BlockSpec((1, 2 * pair, sw), lambda *a: (idx(*a), 0, 0)),
            pl.BlockSpec((1, pair * gk), lambda *a: (0, idx(*a)))]


def _s5_prompt_call(u_all, wts, nb, t, tc, pair):
    bw, cmat, amat, dsk = wts
    ngp, kin, sw2 = bw.shape
    sw = sw2 // 2
    gk = cmat.shape[2] // pair
    ntc = t // tc
    gb_of = lambda g, s: g
    u_specs = [pl.BlockSpec((tc, pair * gk), functools.partial(lambda g, s, b: (b * ntc + s, g), b=b))
               for b in range(nb)]
    kern = functools.partial(_s5_kernel, n_u=nb, tc=tc, nreal=nb, pair=pair, sw=sw, has_h0=False)
    return pl.pallas_call(
        kern,
        grid=(ngp, ntc),
        in_specs=u_specs + _s5_weight_specs(kin, gk, sw, pair, gb_of),
        out_specs=[pl.BlockSpec((nb, tc, pair * gk), lambda g, s: (0, s, g)),
                   pl.BlockSpec((nb, pair * sw), lambda g, s: (0, g)),
                   pl.BlockSpec((nb, pair * sw), lambda g, s: (0, g))],
        out_shape=[jax.ShapeDtypeStruct((nb, t, ngp * pair * gk), BF16),
                   jax.ShapeDtypeStruct((nb, ngp * pair * sw), F32),
                   jax.ShapeDtypeStruct((nb, ngp * pair * sw), F32)],
        scratch_shapes=_s5_scratch(nb * pair, tc, pair * gk, sw),
        compiler_params=_cparams(("arbitrary", "arbitrary")),
        name="s5_prompt",
    )(*([u_all] * nb), bw, cmat, amat, dsk)


def _s5_sample_call(u_all, wts, h0_re, h0_im, ns, ts, row_block0, sb):
    b2, cmat, amat, dsk = wts
    ngb, gk2, sw2 = b2.shape
    gk, sw = gk2 // 2, sw2 // 2
    gb_of = lambda g, s, z: g
    kern = functools.partial(_s5_kernel, n_u=1, tc=ts, nreal=sb, pair=1, sw=sw, has_h0=True)
    return pl.pallas_call(
        kern,
        grid=(ngb, 1, ns // sb),
        in_specs=[pl.BlockSpec((sb * ts, gk), lambda g, s, z: (row_block0 + z, g))]
        + _s5_weight_specs(gk2, gk, sw, 1, gb_of)
        + [pl.BlockSpec((sb, sw), lambda g, s, z: (z, g)), pl.BlockSpec((sb, sw), lambda g, s, z: (z, g))],
        out_specs=[pl.BlockSpec((sb * ts, gk), lambda g, s, z: (z, g)),
                   pl.BlockSpec((sb, sw), lambda g, s, z: (z, g)),
                   pl.BlockSpec((sb, sw), lambda g, s, z: (z, g))],
        out_shape=[jax.ShapeDtypeStruct((ns * ts, ngb * gk), BF16),
                   jax.ShapeDtypeStruct((ns, ngb * sw), F32),
                   jax.ShapeDtypeStruct((ns, ngb * sw), F32)],
        scratch_shapes=_s5_scratch(sb, ts, gk, sw),
        compiler_params=_cparams(("arbitrary", "arbitrary", "arbitrary")),
        name="s5_sample",
    )(u_all, b2, cmat, amat, dsk, h0_re, h0_im)


def _flash_kernel(q_ref, c_ref, kr_ref, wk_ref, wvt_ref, o_ref, k_sc, vt_sc, m_sc, l_sc, acc_sc, *, tq, nope):
    hd = pl.program_id(1)
    qi = pl.program_id(2)
    nh, vd = vt_sc.shape[:2]

    @pl.when(jnp.logical_and(hd == 0, qi == 0))
    def _keys_values():
        c = c_ref[...]
        kr = kr_ref[...]
        for hp in range(0, nh, 2):
            kk = _dot(c, wk_ref[:, hp * nope:(hp + 2) * nope])
            vv = _dot_nt(wvt_ref[hp * vd:(hp + 2) * vd, :], c)
            for e in range(2):
                k_sc[hp + e, :, :nope] = kk[:, e * nope:(e + 1) * nope].astype(k_sc.dtype)
                k_sc[hp + e, :, nope:] = kr
                vt_sc[hp + e] = vv[e * vd:(e + 1) * vd, :].astype(vt_sc.dtype)

    q = q_ref[0]
    m_sc[...] = jnp.full_like(m_sc, NEG_BIG)
    l_sc[...] = jnp.zeros_like(l_sc)
    acc_sc[...] = jnp.zeros_like(acc_sc)

    def block(ki, diagonal):
        start = pl.multiple_of(ki * tq, tq)
        s = _dot_nt(k_sc[hd, pl.ds(start, tq), :], q)
        if diagonal:
            key = lax.broadcasted_iota(jnp.int32, s.shape, 0)
            qry = lax.broadcasted_iota(jnp.int32, s.shape, 1)
            s = jnp.where(key <= qry, s, NEG_BIG)
        m_prev = m_sc[...]
        m_new = jnp.maximum(m_prev, jnp.max(s, axis=0, keepdims=True))
        alpha = jnp.exp2(m_prev - m_new)
        p = jnp.exp2(s - m_new)
        l_sc[...] = alpha * l_sc[...] + jnp.sum(p, axis=0, keepdims=True)
        acc_sc[...] = alpha * acc_sc[...] + _dot(vt_sc[hd, :, pl.ds(start, tq)], p.astype(BF16))
        m_sc[...] = m_new

    def below_diagonal(ki, carry):
        block(ki, False)
        return carry

    lax.fori_loop(0, qi, below_diagonal, 0)
    block(qi, True)
    o_ref[...] = (acc_sc[...] / l_sc[...]).T.astype(o_ref.dtype)


def _flash_outer_kernel(*refs, tq, nb, nope):
    o_ref = refs[5]

    @pl.when(pl.program_id(0) < nb)
    def _rows():
        _flash_kernel(*refs, tq=tq, nope=nope)

    @pl.when(pl.program_id(0) == nb)
    def _tail():
        o_ref[...] = jnp.zeros_like(o_ref)


def _flash_call(q_hm, ckv_b, kr_b, w_uk2, w_uvt2, nb, t, m_total, tq):
    nh, _, qd = q_hm.shape
    lora = w_uk2.shape[0]
    nope = w_uk2.shape[1] // nh
    vd = w_uvt2.shape[0] // nh
    rw = kr_b.shape[1]
    assert nope + rw == qd
    nq = t // tq
    last = m_total // tq - 1
    assert nb * nq + nq > last
    bc = lambda b: jnp.minimum(b, nb - 1)
    hc = lambda b, h: jnp.where(b < nb, h, nh - 1)
    return pl.pallas_call(
        functools.partial(_flash_outer_kernel, tq=tq, nb=nb, nope=nope),
        grid=(nb + 1, nh, nq),
        in_specs=[pl.BlockSpec((1, tq, qd), lambda b, h, i: (hc(b, h), bc(b) * nq + jnp.where(b < nb, i, nq - 1), 0)),
                  pl.BlockSpec((t, lora), lambda b, h, i: (bc(b), 0)),
                  pl.BlockSpec((t, rw), lambda b, h, i: (bc(b), 0)),
                  pl.BlockSpec((lora, nh * nope), lambda b, h, i: (0, 0)),
                  pl.BlockSpec((nh * vd, lora), lambda b, h, i: (0, 0))],
        out_specs=pl.BlockSpec((tq, vd), lambda b, h, i: (jnp.minimum(b * nq + i, last), h)),
        out_shape=jax.ShapeDtypeStruct((m_total, nh * vd), BF16),
        scratch_shapes=[pltpu.VMEM((nh, t, qd), BF16), pltpu.VMEM((nh, vd, t), BF16),
                        pltpu.VMEM((1, tq), F32), pltpu.VMEM((1, tq), F32), pltpu.VMEM((vd, tq), F32)],
        compiler_params=pltpu.CompilerParams(dimension_semantics=("arbitrary", "arbitrary", "arbitrary"),
                                             vmem_limit_bytes=FLASH_VMEM_LIMIT),
        name="flash",
    )(q_hm, ckv_b, kr_b, w_uk2, w_uvt2)


def _qlat_kernel(q_ref, w_ref, o_ref):
    o_ref[0] = _dot(q_ref[0], w_ref[0])


def _qlat_call(q_hm, w_ukt, ms, row_block0):
    nh, nope, lora = w_ukt.shape
    return pl.pallas_call(
        _qlat_kernel,
        grid=(nh,),
        in_specs=[pl.BlockSpec((1, ms, nope), lambda h: (h, row_block0, 0)),
                  pl.BlockSpec((1, nope, lora), lambda h: (h, 0, 0))],
        out_specs=pl.BlockSpec((1, ms, lora), lambda h: (h, 0, 0)),
        out_shape=jax.ShapeDtypeStruct((nh, ms, lora), F32),
        compiler_params=_cparams(("arbitrary",)),
        name="qlat",
    )(q_hm, w_ukt)


def _decode_kernel(pt_ref, ql_ref, qr_ref, cn_ref, kn_ref, ck_hbm, kr_hbm, o_ref,
                   ck_buf, kr_buf, sem, qlb_sc, qrb_sc, kb_sc, rb_sc, m_sc, l_sc, acc_sc, *, npg, sq, ts, rope, n_pages):
    s_blk = pl.program_id(0)
    j = pl.program_id(1)
    nsteps = pl.num_programs(1)
    g = s_blk * nsteps + j
    total = pl.num_programs(0) * nsteps
    nslot = ck_buf.shape[0]
    ahead = nslot - 1
    slot = lax.rem(g, nslot)
    nh, _, lora = ql_ref.shape
    rows = nh * ts
    pg = ck_buf.shape[2]

    def page_copies(step):
        sb = lax.div(step, nsteps)
        jj = lax.rem(step, nsteps)
        sl = lax.rem(step, nslot)
        out = []
        for a in range(sq):
            for i in range(npg):
                page = pt_ref[(sb * sq + a) * n_pages + jj * npg + i]
                out.append(pltpu.make_async_copy(ck_hbm.at[page], ck_buf.at[sl, a * npg + i], sem.at[0, sl]))
                out.append(pltpu.make_async_copy(kr_hbm.at[page], kr_buf.at[sl, a * npg + i], sem.at[1, sl]))
        return out

    for k in range(ahead):
        @pl.when(jnp.logical_and(g == 0, k < total))
        def _prime(k=k):
            for c in page_copies(k):
                c.start()

    @pl.when(g + ahead < total)
    def _start_ahead():
        for c in page_copies(g + ahead):
            c.start()

    for c in page_copies(g):
        c.wait()
    ck_refs = [ck_buf.at[slot, i] for i in range(sq * npg)]
    kr_refs = [kr_buf.at[slot, i] for i in range(sq * npg)]

    @pl.when(j == 0)
    def _init():
        for a in range(sq):
            qlb_sc[a] = ql_ref[:, a * ts:(a + 1) * ts, :].reshape(rows, lora).astype(BF16)
            qrb_sc[a] = qr_ref[:, a * ts:(a + 1) * ts, :].reshape(rows, rope).astype(BF16)
        m_sc[...] = jnp.full_like(m_sc, NEG_BIG)
        l_sc[...] = jnp.zeros_like(l_sc)
        acc_sc[...] = jnp.zeros_like(acc_sc)

    def update(a, s, kb):
        m_prev = m_sc[a]
        m_new = jnp.maximum(m_prev, jnp.max(s, axis=-1, keepdims=True))
        alpha = jnp.exp2(m_prev - m_new)
        p = jnp.exp2(s - m_new)
        l_sc[a] = alpha * l_sc[a] + jnp.sum(p, axis=-1, keepdims=True)
        acc_sc[a] = alpha * acc_sc[a] + _dot(p.astype(BF16), kb)
        m_sc[a] = m_new

    for a in range(sq):
        for i in range(npg):
            kb_sc[a, i * pg:(i + 1) * pg, :] = ck_refs[a * npg + i][...].astype(BF16)
            rb_sc[a, :, i * pg:(i + 1) * pg] = kr_refs[a * npg + i][...].astype(BF16)
    for a in range(sq):
        kb = kb_sc[a]
        update(a, _dot_nt(qlb_sc[a], kb) + _dot(qrb_sc[a], rb_sc[a]), kb)

    @pl.when(j == pl.num_programs(1) - 1)
    def _fin():
        for a in range(sq):
            kn = jnp.concatenate([cn_ref[a * ts:(a + 1) * ts, :], jnp.zeros((pg - ts, lora), F32)], axis=0).astype(BF16)
            rn = jnp.concatenate([kn_ref[a * ts:(a + 1) * ts, :], jnp.zeros((pg - ts, rope), F32)], axis=0).astype(BF16)
            s = _dot_nt(qlb_sc[a], kn) + _dot_nt(qrb_sc[a], rn)
            tq = lax.broadcasted_iota(jnp.int32, s.shape, 0) % ts
            tk = lax.broadcasted_iota(jnp.int32, s.shape, 1)
            update(a, jnp.where(tk <= tq, s, NEG_BIG), kn)
            o = acc_sc[a] / l_sc[a]
            o_ref[:, a * ts:(a + 1) * ts, :] = o.reshape(nh, ts, lora)


def _decode_call(page_table, q_lat, q_rope, ckv_all, kr_all, cache_ckv, cache_krope_t, ns, ts, row_block0, npg, sq):
    nh, ms, lora = q_lat.shape
    rope = q_rope.shape[2]
    n_pages = page_table.shape[1]
    pg = cache_ckv.shape[1]
    nsteps = n_pages // npg
    rows = nh * ts

    n = sq * npg
    grid_spec = pltpu.PrefetchScalarGridSpec(
        num_scalar_prefetch=1,
        grid=(ns // sq, nsteps),
        in_specs=[pl.BlockSpec((nh, sq * ts, lora), lambda s, j, pt: (0, s, 0)),
                  pl.BlockSpec((nh, sq * ts, rope), lambda s, j, pt: (0, s, 0)),
                  pl.BlockSpec((sq * ts, lora), lambda s, j, pt: (row_block0 + s, 0)),
                  pl.BlockSpec((sq * ts, rope), lambda s, j, pt: (row_block0 + s, 0)),
                  pl.BlockSpec(memory_space=pl.ANY),
                  pl.BlockSpec(memory_space=pl.ANY)],
        out_specs=pl.BlockSpec((nh, sq * ts, lora), lambda s, j, pt: (0, s, 0)),
        scratch_shapes=[pltpu.VMEM((DECODE_SLOTS, n, pg, lora), F32), pltpu.VMEM((DECODE_SLOTS, n, rope, pg), F32),
                        pltpu.SemaphoreType.DMA((2, DECODE_SLOTS)),
                        pltpu.VMEM((sq, rows, lora), BF16), pltpu.VMEM((sq, rows, rope), BF16),
                        pltpu.VMEM((sq, npg * pg, lora), BF16), pltpu.VMEM((sq, rope, npg * pg), BF16),
                        pltpu.VMEM((sq, rows, 1), F32), pltpu.VMEM((sq, rows, 1), F32),
                        pltpu.VMEM((sq, rows, lora), F32)],
    )
    return pl.pallas_call(
        functools.partial(_decode_kernel, npg=npg, sq=sq, ts=ts, rope=rope, n_pages=n_pages),
        grid_spec=grid_spec,
        out_shape=jax.ShapeDtypeStruct((nh, ms, lora), F32),
        compiler_params=_cparams(("arbitrary", "arbitrary")),
        name="decode",
    )(page_table.reshape(-1), q_lat, q_rope, ckv_all, kr_all, cache_ckv, cache_krope_t)


def _vup_kernel(o_ref, w_ref, att_any, out_ref):
    del att_any
    out_ref[...] = _dot(o_ref[0].astype(BF16), w_ref[0]).astype(out_ref.dtype)


def _vup_call(o_lat, w_uv3, att_all, row_block0):
    nh, ms, lora = o_lat.shape
    vd = w_uv3.shape[2]
    return pl.pallas_call(
        _vup_kernel,
        grid=(nh,),
        in_specs=[pl.BlockSpec((1, ms, lora), lambda h: (h, 0, 0)),
                  pl.BlockSpec((1, lora, vd), lambda h: (h, 0, 0)),
                  pl.BlockSpec(memory_space=pl.ANY)],
        out_specs=pl.BlockSpec((ms, vd), lambda h: (row_block0, h)),
        out_shape=jax.ShapeDtypeStruct(att_all.shape, att_all.dtype),
        input_output_aliases={2: 0},
        compiler_params=_cparams(("arbitrary",)),
        name="vup",
    )(o_lat, w_uv3, att_all)


def _glu_kernel(g_ref, wa_ref, wb_ref, z_ref, o_ref):
    g = g_ref[...]
    ga = _dot(g, wa_ref[...])
    gb = _dot(g, wb_ref[...])
    o_ref[...] = (ga * _sigmoid(gb) * z_ref[...].astype(F32)).astype(o_ref.dtype)


def _glu_call(gy, w_glu, zs_act, tm, tn):
    m, k = gy.shape
    n = w_glu.shape[1] // 2
    nn = n // tn
    return pl.pallas_call(
        _glu_kernel,
        grid=(m // tm, nn),
        in_specs=[pl.BlockSpec((tm, k), lambda i, j: (i, 0)),
                  pl.BlockSpec((k, tn), lambda i, j: (0, j)),
                  pl.BlockSpec((k, tn), lambda i, j: (0, nn + j)),
                  pl.BlockSpec((tm, tn), lambda i, j: (i, j))],
        out_specs=pl.BlockSpec((tm, tn), lambda i, j: (i, j)),
        out_shape=jax.ShapeDtypeStruct((m, n), BF16),
        compiler_params=_cparams(("arbitrary", "arbitrary")),
        name="glu",
    )(gy, w_glu, w_glu, zs_act)


def _merge_kernel(ys_ref, att_ref, za_ref, wbs_ref, wo_ref, gs_ref, ga_ref, o_ref, v_sc):
    @pl.when(pl.program_id(1) == 0)
    def _gate():
        v_sc[...] = (att_ref[...].astype(F32) * za_ref[...].astype(F32)).astype(BF16)

    p_s = _dot(ys_ref[...], wbs_ref[...])
    p_a = _dot(v_sc[...], wo_ref[...])
    o_ref[...] = (gs_ref[...].astype(F32) * p_s + ga_ref[...].astype(F32) * p_a).astype(o_ref.dtype)


def _merge_call(ys2, att, za_act, w_bs, w_o, gates, tm, tn):
    m, ks = ys2.shape
    ka = att.shape[1]
    n = w_bs.shape[1]
    nn = n // tn
    return pl.pallas_call(
        _merge_kernel,
        grid=(m // tm, nn),
        in_specs=[pl.BlockSpec((tm, ks), lambda i, j: (i, 0)),
                  pl.BlockSpec((tm, ka), lambda i, j: (i, 0)),
                  pl.BlockSpec((tm, ka), lambda i, j: (i, 0)),
                  pl.BlockSpec((ks, tn), lambda i, j: (0, j)),
                  pl.BlockSpec((ka, tn), lambda i, j: (0, j)),
                  pl.BlockSpec((tm, tn), lambda i, j: (i, j)),
                  pl.BlockSpec((tm, tn), lambda i, j: (i, nn + j))],
        out_specs=pl.BlockSpec((tm, tn), lambda i, j: (i, j)),
        out_shape=jax.ShapeDtypeStruct((m, n), BF16),
        scratch_shapes=[pltpu.VMEM((tm, ka), BF16)],
        compiler_params=_cparams(("arbitrary", "arbitrary")),
        name="merge",
    )(ys2, att, za_act, w_bs, w_o, gates, gates)


def _final_kernel(mg_ref, w_ref, x_ref, gate_ref, g_ref, b_ref, o_ref, *, alpha):
    out = _dot(mg_ref[...], w_ref[...])
    x = x_ref[...]
    y = alpha * x + gate_ref[...] * out.reshape(x.shape)
    mu = jnp.mean(y, axis=-1, keepdims=True)
    yc = y - mu
    var = jnp.mean(yc * yc, axis=-1, keepdims=True)
    o_ref[...] = yc * lax.rsqrt(var + LN_EPS) * g_ref[...] + b_ref[...]


def _final_call(merged, w_out, x, mod3, ln_g, ln_b, alpha, x_blk, gate_blk, gate_map, merged_map, grid):
    d = w_out.shape[0]
    rows = x_blk[0] * x_blk[1]
    nd = len(grid)
    x_map = (lambda i, j: (i, j, 0)) if nd == 2 else (lambda i: (i, 0, 0))
    const2 = (lambda i, j: (0, 0)) if nd == 2 else (lambda i: (0, 0))
    const3 = (lambda i, j: (0, 0, 0)) if nd == 2 else (lambda i: (0, 0, 0))
    return pl.pallas_call(
        functools.partial(_final_kernel, alpha=alpha),
        grid=grid,
        in_specs=[pl.BlockSpec((rows, d), merged_map),
                  pl.BlockSpec((d, d), const2),
                  pl.BlockSpec(x_blk, x_map),
                  pl.BlockSpec(gate_blk, gate_map),
                  pl.BlockSpec((1, 1, d), const3),
                  pl.BlockSpec((1, 1, d), const3)],
        out_specs=pl.BlockSpec(x_blk, x_map),
        out_shape=jax.ShapeDtypeStruct(x.shape, F32),
        compiler_params=_cparams(("arbitrary",) * nd),
        name="final",
    )(merged, w_out, x, mod3, ln_g.reshape(1, 1, d), ln_b.reshape(1, 1, d))


def _rope_tables(pos, rope):
    freqs = ROPE_BASE ** (-jnp.arange(0, rope, 2, dtype=F32) / rope)
    ang = pos.astype(F32)[:, None] * freqs[None, :]
    cos, sin = jnp.cos(ang), jnp.sin(ang)
    return jnp.concatenate([cos] * 4 + [sin] * 4, axis=-1)


def _rot_rows(wt, rope):
    half = rope // 2
    return jnp.concatenate([-wt[..., half:, :], wt[..., :half, :]], axis=-2)


def _block_diag(x, gblk):
    g, r, c = x.shape
    x4 = x.reshape(g // gblk, gblk, r, c)
    eye = jnp.eye(gblk, dtype=x.dtype)
    return jnp.einsum("bgrc,gh->bgrhc", x4, eye).reshape(g // gblk, gblk * r, gblk * c)


def kernel(x_prompt, x_sample, c_prompt, c_sample, cache_ckv, cache_krope, state_ssm_re, state_ssm_im, page_table, w_ada, b_ada, w_in, g_kv, w_uk, w_uv, w_o, lam_re, lam_im, log_dt, b_re, b_im, c_re, c_im, d_skip, w_glu, w_bs, w_out, ln_g, ln_b):
    nb, t, d = x_prompt.shape
    ns, ts, _ = x_sample.shape
    depth = w_in.shape[0]
    assert depth == 1, "single-layer step"
    lora, nh, nope = w_uk.shape[1:]
    vd = w_uv.shape[3]
    rope = cache_krope.shape[3]
    d_ssm = d_skip.shape[1]
    ngrp, nstate, gch = b_re.shape[1:]
    d_attn = nh * vd
    pg = cache_ckv.shape[2]
    past = page_table.shape[1] * pg
    mp, ms = nb * t, ns * ts
    m = mp + ms
    scale = float((nope + rope) ** -0.5) * LOG2E
    alpha = float((2 * depth) ** 0.25)

    tm = math.gcd(math.gcd(mp, ms), 1024)
    tmh = min(tm, 512)
    tn = 512
    assert rope * 2 == 128 and nope == 128 and vd == 128

    wt = jnp.swapaxes(w_in, 1, 2)[0]
    o_u, o_zs, o_q = 0, d_ssm, 2 * d_ssm
    o_ckv = o_q + nh * (nope + rope)
    o_kr = o_ckv + lora
    o_za = o_kr + rope
    o_gs = o_za + d_attn
    o_ga = o_gs + d
    w_u = wt[o_u:o_zs].astype(BF16)
    w_za = wt[o_za:o_gs].astype(BF16)
    w_zs = wt[o_zs:o_q].astype(BF16)
    w_g = wt[o_gs:].astype(BF16)
    wq = wt[o_q:o_ckv].reshape(nh, nope + rope, d)
    qhb = 4
    wq4 = wq.reshape(nh // qhb, qhb, nope + rope, d)
    w_q4 = jnp.concatenate([wq4[:, :, :nope, :].reshape(nh // qhb, qhb * nope, d),
                            wq4[:, :, nope:, :].reshape(nh // qhb, qhb * rope, d)], axis=1).astype(BF16)
    wkr = wt[o_kr:o_za]
    w_ck = jnp.concatenate([wt[o_ckv:o_kr], wkr, _rot_rows(wkr, rope)], axis=0).astype(BF16)
    w_uk2 = w_uk[0].reshape(lora, nh * nope).astype(BF16)
    w_uvt2 = w_uv[0].transpose(1, 2, 0).reshape(nh * vd, lora).astype(BF16)
    w_ukt = w_uk[0].transpose(1, 2, 0).astype(BF16)
    w_uv3 = w_uv[0].transpose(1, 0, 2).astype(BF16)
    w_glu_b = w_glu[0].astype(BF16)
    w_bs_b = w_bs[0].astype(BF16)
    w_o_b = w_o[0].astype(BF16)
    w_out_b = w_out[0].astype(BF16)

    cs4 = jnp.concatenate([jnp.tile(_rope_tables(jnp.arange(t), rope), (nb, 1)),
                           jnp.tile(_rope_tables(past + jnp.arange(ts), rope), (ns, 1))], axis=0)
    cs = jnp.concatenate([cs4[:, :rope], cs4[:, 2 * rope:3 * rope]], axis=1)

    c_all = jnp.concatenate([c_sample, c_prompt], axis=0)
    pad = (-c_all.shape[0]) % 8
    c_all = jnp.concatenate([c_all, jnp.zeros((pad, d), F32)], axis=0)
    mod = _mod_call(c_all, w_ada[0], b_ada[0], tn)
    mod3 = mod.reshape(mod.shape[0], 1, 3 * d)

    sb = max(1, min(ns, tmh // ts))
    h_all = _hmod_prompt(x_prompt, mod3, ns, m, tmh)
    h_all = _hmod_sample(x_sample, mod3, h_all, mp // (sb * ts), sb)

    u_all = _proj_call(h_all, w_u, None, F32, tm, tn, "proj_u")
    za_act = _proj_call(h_all, w_za, "silu", BF16, tm, 2 * tn, "proj_za")
    zs_act = _proj_call(h_all, w_zs, "silu", BF16, tm, 2 * tn, "proj_zs")
    gates = _proj_call(h_all, w_g, "sigmoid", BF16, tm, 2 * tn, "proj_g")
    q_hm = _qproj_call(h_all, w_q4, cs4, tm, nh, nope, rope, scale, qhb)
    ckv_all, kr_all, ckv_b, kr_b = _ckv_call(h_all, w_ck, g_kv[0], cs, tmh, lora, rope)

    gblk = SSM_GROUP_BLOCK
    a_re, a_im, bb_re, bb_im = _s5_prep_call(lam_re[0], lam_im[0], log_dt[0],
                                             b_re[0].transpose(0, 2, 1), b_im[0].transpose(0, 2, 1))
    bcat = jnp.concatenate([_block_diag(bb_re, gblk), _block_diag(bb_im, gblk)], axis=-1)
    b2 = jnp.concatenate([bcat, bcat], axis=1).astype(BF16)
    ccat = jnp.concatenate([_block_diag(c_re[0].transpose(0, 2, 1), gblk),
                            _block_diag(-c_im[0].transpose(0, 2, 1), gblk)], axis=1).astype(BF16)
    ngb = ngrp // gblk
    sw = gblk * nstate
    amat = jnp.stack([a_re.reshape(ngb, sw), a_im.reshape(ngb, sw)], axis=1)
    s5w = (b2, ccat, amat, d_skip)
    tc = min(t, S5_TIME_CHUNK)
    pair = 2 if (2 * nb <= 8 and ngb % 2 == 0) else 1
    gkb = gblk * gch
    s5w_p = s5w if pair == 1 else (
        bcat.astype(BF16).reshape(ngb // 2, 2 * gkb, 2 * sw),
        ccat.reshape(ngb // 2, 2, 2 * sw, gkb).transpose(0, 2, 1, 3).reshape(ngb // 2, 2 * sw, 2 * gkb),
        amat.reshape(ngb // 2, 4, sw), d_skip)
    gy_p, sre_p, sim_p = _s5_prompt_call(u_all, s5w_p, nb, t, tc, pair)
    ssb = min(ns, 32)
    gy_s, sre_s, sim_s = _s5_sample_call(u_all, s5w, state_ssm_re[0].reshape(ns, ngrp * nstate),
                                         state_ssm_im[0].reshape(ns, ngrp * nstate), ns, ts, mp // (ssb * ts), ssb)
    gy = jnp.concatenate([gy_p.reshape(mp, d_ssm), gy_s], axis=0)

    tq = min(t, FLASH_BLOCK)
    att = _flash_call(q_hm, ckv_b, kr_b, w_uk2, w_uvt2, nb, t, m, tq)
    q_lat = _qlat_call(q_hm, w_ukt, ms, mp // ms)
    q_rope_s = q_hm[:, mp:, nope:nope + rope].astype(F32)
    sq = DECODE_SEQS_PER_STEP
    krope_t = jnp.swapaxes(cache_krope, 2, 3).reshape(cache_krope.shape[1], rope, pg)
    o_lat = _decode_call(page_table, q_lat, q_rope_s, ckv_all, kr_all, cache_ckv.reshape(cache_ckv.shape[1:]), krope_t,
                         ns, ts, mp // (sq * ts), PAGES_PER_STEP, sq)
    att = _vup_call(o_lat, w_uv3, att, mp // ms)

    ys2 = _glu_call(gy, w_glu_b, zs_act, tm, tn)
    merged = _merge_call(ys2, att, za_act, w_bs_b, w_o_b, gates, tm, tn)
    ntp = t // tmh
    y_p = _final_call(merged, w_out_b, x_prompt, mod3, ln_g[0], ln_b[0], alpha,
                      (1, tmh, d), (1, 1, d), lambda i, j: (ns + i, 0, 2), lambda i, j: (i * ntp + j, 0), (nb, ntp))
    y_s = _final_call(merged, w_out_b, x_sample, mod3, ln_g[0], ln_b[0], alpha,
                      (sb, ts, d), (sb, 1, d), lambda i: (i, 0, 2), lambda i: (mp // (sb * ts) + i, 0), (ns // sb,))

    st = lambda a, n: a.reshape(1, n, ngrp, nstate)
    return (y_p, y_s,
            ckv_all[:mp].reshape(1, nb, t, lora), kr_all[:mp].reshape(1, nb, t, rope), st(sre_p, nb), st(sim_p, nb),
            ckv_all[mp:].reshape(1, ns, ts, lora), kr_all[mp:].reshape(1, ns, ts, rope), st(sre_s, ns), st(sim_s, ns))
```

```python
import functools
import math

import jax
import jax.numpy as jnp
from jax import lax
from jax.experimental import pallas as pl
from jax.experimental.pallas import tpu as pltpu

F32 = jnp.float32
BF16 = jnp.bfloat16

LN_EPS = 1e-5
RMS_EPS = 1e-6
ROPE_BASE = 10000.0
NEG_BIG = -1e30
VMEM_LIMIT = 48 * 1024 * 1024
FLASH_VMEM_LIMIT = 56 * 1024 * 1024
PAGES_PER_STEP = 16
DECODE_SEQS_PER_STEP = 2
DECODE_SLOTS = 3
LOG2E = 1.4426950408889634
SSM_GROUP_BLOCK = 8
LANES = 128
S5_TIME_CHUNK = 256
FLASH_BLOCK = 1024


def _cparams(sem):
    return pltpu.CompilerParams(dimension_semantics=sem, vmem_limit_bytes=VMEM_LIMIT)


def _sigmoid(x):
    return 1.0 / (1.0 + jnp.exp(-x))


def _silu(x):
    return x * _sigmoid(x)


def _dot(a, b):
    return jnp.dot(a, b, preferred_element_type=F32)


def _dot_nt(a, b):
    return lax.dot_general(a, b, (((1,), (1,)), ((), ())), preferred_element_type=F32)


def _mod_kernel(c_ref, w_ref, b_ref, o_ref):
    a = _silu(c_ref[...]).astype(BF16)
    o_ref[...] = _dot(a, w_ref[...].astype(BF16)) + b_ref[...]


def _mod_call(c_all, w_ada, b_ada, tn):
    r, d = c_all.shape
    n = w_ada.shape[1]
    return pl.pallas_call(
        _mod_kernel,
        grid=(n // tn,),
        in_specs=[pl.BlockSpec((r, d), lambda j: (0, 0)),
                  pl.BlockSpec((d, tn), lambda j: (0, j)),
                  pl.BlockSpec((1, tn), lambda j: (0, j))],
        out_specs=pl.BlockSpec((r, tn), lambda j: (0, j)),
        out_shape=jax.ShapeDtypeStruct((r, n), F32),
        compiler_params=_cparams(("arbitrary",)),
        name="mod",
    )(c_all, w_ada, b_ada.reshape(1, n))


def _hmod_kernel(x_ref, sh_ref, sc_ref, *rest):
    o_ref = rest[-1]
    h = x_ref[...] * (1.0 + sc_ref[...]) + sh_ref[...]
    o_ref[...] = h.reshape(o_ref.shape).astype(o_ref.dtype)


def _hmod_prompt_kernel(x_ref, sh_ref, sc_ref, o_ref, *, nb):
    @pl.when(pl.program_id(0) < nb)
    def _rows():
        _hmod_kernel(x_ref, sh_ref, sc_ref, o_ref)

    @pl.when(pl.program_id(0) == nb)
    def _tail():
        o_ref[...] = jnp.zeros_like(o_ref)


def _hmod_prompt(x, mod3, mod_row0, m_total, tm):
    b, t, d = x.shape
    nt = t // tm
    last = m_total // tm - 1
    assert b * nt + nt > last
    bc = lambda i: jnp.minimum(i, b - 1)
    return pl.pallas_call(
        functools.partial(_hmod_prompt_kernel, nb=b),
        grid=(b + 1, nt),
        in_specs=[pl.BlockSpec((1, tm, d), lambda i, j: (bc(i), jnp.where(i < b, j, nt - 1), 0)),
                  pl.BlockSpec((1, 1, d), lambda i, j: (mod_row0 + bc(i), 0, 0)),
                  pl.BlockSpec((1, 1, d), lambda i, j: (mod_row0 + bc(i), 0, 1))],
        out_specs=pl.BlockSpec((tm, d), lambda i, j: (jnp.minimum(i * nt + j, last), 0)),
        out_shape=jax.ShapeDtypeStruct((m_total, d), BF16),
        compiler_params=_cparams(("arbitrary", "arbitrary")),
        name="hmod_prompt",
    )(x, mod3, mod3)


def _hmod_sample(x, mod3, h_all, row_block0, sb):
    ns, ts, d = x.shape
    return pl.pallas_call(
        _hmod_kernel,
        grid=(ns // sb,),
        in_specs=[pl.BlockSpec((sb, ts, d), lambda i: (i, 0, 0)),
                  pl.BlockSpec((sb, 1, d), lambda i: (i, 0, 0)),
                  pl.BlockSpec((sb, 1, d), lambda i: (i, 0, 1)),
                  pl.BlockSpec(memory_space=pl.ANY)],
        out_specs=pl.BlockSpec((sb * ts, d), lambda i: (row_block0 + i, 0)),
        out_shape=jax.ShapeDtypeStruct(h_all.shape, h_all.dtype),
        input_output_aliases={3: 0},
        compiler_params=_cparams(("arbitrary",)),
        name="hmod_sample",
    )(x, mod3, mod3, h_all)


def _proj_kernel(h_ref, w_ref, o_ref, *, act):
    acc = _dot_nt(h_ref[...], w_ref[...])
    if act == "silu":
        acc = _silu(acc)
    elif act == "sigmoid":
        acc = _sigmoid(acc)
    o_ref[...] = acc.astype(o_ref.dtype)


def _proj_call(h, w, act, out_dtype, tm, tn, name):
    m, k = h.shape
    n = w.shape[0]
    return pl.pallas_call(
        functools.partial(_proj_kernel, act=act),
        grid=(m // tm, n // tn),
        in_specs=[pl.BlockSpec((tm, k), lambda i, j: (i, 0)),
                  pl.BlockSpec((tn, k), lambda i, j: (j, 0))],
        out_specs=pl.BlockSpec((tm, tn), lambda i, j: (i, j)),
        out_shape=jax.ShapeDtypeStruct((m, n), out_dtype),
        compiler_params=_cparams(("arbitrary", "arbitrary")),
        name=name,
    )(h, w)


def _qproj_kernel(h_ref, w_ref, cs_ref, o_ref, *, nope, rope, scale, hb):
    half = rope // 2
    acc = _dot_nt(h_ref[...], w_ref[0])
    cos = cs_ref[:, :LANES]
    sin = cs_ref[:, LANES:]
    lane = lax.broadcasted_iota(jnp.int32, cos.shape, 1)
    first_half = (lane & half) == 0
    for pr in range(hb // 2):
        x = acc[:, hb * nope + pr * LANES:hb * nope + (pr + 1) * LANES]
        rot = jnp.where(first_half, -pltpu.roll(x, LANES - half, 1), pltpu.roll(x, half, 1))
        r = (x * cos + rot * sin) * scale
        for e, rr in ((2 * pr, r), (2 * pr + 1, pltpu.roll(r, rope, 1))):
            o_ref[e, :, :nope] = (acc[:, e * nope:(e + 1) * nope] * scale).astype(o_ref.dtype)
            o_ref[e, :, nope:] = jnp.where(lane < rope, rr, 0.0).astype(o_ref.dtype)


def _qproj_call(h, w_q4, cs4, tm, nh, nope, rope, scale, hb):
    m, k = h.shape
    wd = w_q4.shape[1]
    assert 2 * rope == LANES and nope == LANES and hb % 2 == 0 and wd == hb * (nope + rope)
    return pl.pallas_call(
        functools.partial(_qproj_kernel, nope=nope, rope=rope, scale=scale, hb=hb),
        grid=(m // tm, nh // hb),
        in_specs=[pl.BlockSpec((tm, k), lambda i, j: (i, 0)),
                  pl.BlockSpec((1, wd, k), lambda i, j: (j, 0, 0)),
                  pl.BlockSpec((tm, 2 * LANES), lambda i, j: (i, 0))],
        out_specs=pl.BlockSpec((hb, tm, nope + LANES), lambda i, j: (j, i, 0)),
        out_shape=jax.ShapeDtypeStruct((nh, m, nope + LANES), BF16),
        compiler_params=_cparams(("arbitrary", "arbitrary")),
        name="qproj",
    )(h, w_q4, cs4)


def _ckv_kernel(h_ref, w_ref, g_ref, cs_ref, ckv_ref, kr_ref, ckvb_ref, krb_ref, *, lora, rope):
    acc = _dot_nt(h_ref[...], w_ref[...])
    c = acc[:, :lora]
    ms = jnp.mean(c * c, axis=-1, keepdims=True)
    ckv = c * lax.rsqrt(ms + RMS_EPS) * g_ref[...]
    ckv_ref[...] = ckv
    ckvb_ref[...] = ckv.astype(BF16)
    t = acc[:, lora:] * cs_ref[...]
    r = t + pltpu.roll(t, rope, 1)
    kr_ref[...] = r[:, :rope]
    lane = lax.broadcasted_iota(jnp.int32, r.shape, 1)
    krb_ref[...] = jnp.where(lane < rope, r, 0.0).astype(BF16)


def _ckv_call(h, w_ck, g_kv, cs, tm, lora, rope):
    m, k = h.shape
    wd = w_ck.shape[0]
    return pl.pallas_call(
        functools.partial(_ckv_kernel, lora=lora, rope=rope),
        grid=(m // tm,),
        in_specs=[pl.BlockSpec((tm, k), lambda i: (i, 0)),
                  pl.BlockSpec((wd, k), lambda i: (0, 0)),
                  pl.BlockSpec((1, lora), lambda i: (0, 0)),
                  pl.BlockSpec((tm, 2 * rope), lambda i: (i, 0))],
        out_specs=[pl.BlockSpec((tm, lora), lambda i: (i, 0)),
                   pl.BlockSpec((tm, rope), lambda i: (i, 0)),
                   pl.BlockSpec((tm, lora), lambda i: (i, 0)),
                   pl.BlockSpec((tm, 2 * rope), lambda i: (i, 0))],
        out_shape=[jax.ShapeDtypeStruct((m, lora), F32),
                   jax.ShapeDtypeStruct((m, rope), F32),
                   jax.ShapeDtypeStruct((m, lora), BF16),
                   jax.ShapeDtypeStruct((m, 2 * rope), BF16)],
        compiler_params=_cparams(("arbitrary",)),
        name="ckvproj",
    )(h, w_ck, g_kv.reshape(1, lora), cs)


def _s5_prep_kernel(lre_ref, lim_ref, ldt_ref, bre_ref, bim_ref, are_ref, aim_ref, bbre_ref, bbim_ref):
    lre = lre_ref[...]
    lim = lim_ref[...]
    dt = jnp.exp(ldt_ref[...])
    mag = jnp.exp(lre * dt)
    a_re = mag * jnp.cos(lim * dt)
    a_im = mag * jnp.sin(lim * dt)
    are_ref[...] = a_re
    aim_ref[...] = a_im
    den = lre * lre + lim * lim
    nr = a_re - 1.0
    c_re = (nr * lre + a_im * lim) / den
    c_im = (a_im * lre - nr * lim) / den
    bre = bre_ref[...]
    bim = bim_ref[...]
    bbre_ref[...] = c_re * bre - c_im * bim
    bbim_ref[...] = c_re * bim + c_im * bre


def _s5_prep_call(lam_re, lam_im, log_dt, b_re_t, b_im_t):
    g, p = lam_re.shape
    n = b_re_t.shape[1]
    a_re, a_im, bb_re, bb_im = pl.pallas_call(
        _s5_prep_kernel,
        out_shape=[jax.ShapeDtypeStruct((g, 1, p), F32), jax.ShapeDtypeStruct((g, 1, p), F32),
                   jax.ShapeDtypeStruct((g, n, p), F32), jax.ShapeDtypeStruct((g, n, p), F32)],
        name="s5prep",
    )(lam_re.reshape(g, 1, p), lam_im.reshape(g, 1, p), log_dt.reshape(g, 1, 1), b_re_t, b_im_t)
    return a_re.reshape(g, p), a_im.reshape(g, p), bb_re, bb_im


def _s5_kernel(*refs, n_u, tc, nreal, pair, sw, has_h0):
    u_refs = refs[:n_u]
    i = n_u
    b2_ref, c_ref, a_ref, d_ref = refs[i:i + 4]
    i += 4
    if has_h0:
        h0re_ref, h0im_ref = refs[i:i + 2]
        i += 2
    gy_ref, sre_ref, sim_ref = refs[i:i + 3]
    i += 3
    us_sc, ut_sc, bu_sc, y_sc, st_sc = refs[i:i + 5]
    nseq = nreal * pair
    gk = d_ref.shape[1] // pair
    assert tc % 8 == 0 and nseq % 8 == 0 and (pair == 1 or n_u == nreal)

    step = pl.program_id(1)

    @pl.when(step == 0)
    def _init():
        if has_h0:
            st_sc[:, :sw] = h0re_ref[...]
            st_sc[:, sw:] = h0im_ref[...]
        else:
            st_sc[...] = jnp.zeros_like(st_sc)

    def u_of(v):
        b, p = v % nreal, v // nreal
        return u_refs[b][:, p * gk:(p + 1) * gk] if n_u > 1 else u_refs[0][b * tc:(b + 1) * tc, :]

    nkb = gk // LANES
    nub = pair * nkb
    for v in range(nseq):
        p, uv = v // nreal, u_of(v)
        for kb in range(nub):
            own = kb // nkb == p
            us_sc[kb, v * tc:(v + 1) * tc, :] = (uv[:, (kb - p * nkb) * LANES:(kb - p * nkb + 1) * LANES] if own
                                                 else jnp.zeros((tc, LANES), F32))
    for sg in range(0, nseq, 8):
        def permute(t, c, base=sg * tc):
            dst = pl.multiple_of(base + t * 8, 8)
            for kb in range(nub):
                ut_sc[kb, pl.ds(dst, 8), :] = us_sc[kb, pl.ds(base + t, 8, stride=tc), :]
            return c

        lax.fori_loop(0, tc, permute, 0, unroll=8)
    up = jnp.concatenate([ut_sc[kb] for kb in range(nub)], axis=1)
    if pair == 1:
        u_hi = up.astype(BF16)
        u_lo = (up - u_hi.astype(F32)).astype(BF16)
        lhs = jnp.concatenate([u_hi, u_lo], axis=1)
    else:
        lhs = up.astype(BF16)
    bu = _dot(lhs, b2_ref[0])
    nj = sw // LANES
    for j in range(2 * nj):
        bu_sc[j] = bu[:, j * LANES:(j + 1) * LANES]

    for sg in range(0, nseq, 8):
        n8 = 8
        base = sg * tc
        def a_tile(r, j, sg=sg, n8=n8):
            ps = [v // nreal for v in range(sg, sg + n8)]
            runs = [(p, ps.count(p)) for p in sorted(set(ps))]
            parts = [jnp.broadcast_to(a_ref[0, 2 * p + r:2 * p + r + 1, j * LANES:(j + 1) * LANES], (cnt, LANES))
                     for p, cnt in runs]
            return parts[0] if len(parts) == 1 else jnp.concatenate(parts, axis=0)

        a_b = [a_tile(r, j) for r in range(2) for j in range(nj)]

        def body(i8, carry, base=base, a_b=a_b):
            h = list(carry)
            for k in range(8):
                rows = pl.ds(pl.multiple_of(base + (i8 * 8 + k) * 8, 8), 8)
                new = [None] * (2 * nj)
                for j in range(nj):
                    new[j] = a_b[j] * h[j] - a_b[nj + j] * h[nj + j] + bu_sc[j, rows, :]
                    new[nj + j] = a_b[j] * h[nj + j] + a_b[nj + j] * h[j] + bu_sc[nj + j, rows, :]
                for j in range(2 * nj):
                    bu_sc[j, rows, :] = new[j]
                h = new
            return tuple(h)

        init = tuple(st_sc[sg:sg + n8, j * LANES:(j + 1) * LANES] for j in range(2 * nj))
        fin = lax.fori_loop(0, tc // 8, body, init)
        for j in range(2 * nj):
            st_sc[sg:sg + n8, j * LANES:(j + 1) * LANES] = fin[j]

    hs = jnp.concatenate([bu_sc[j] for j in range(2 * nj)], axis=1)
    yp = _dot(hs.astype(BF16), c_ref[0])
    for jb in range(nub):
        y_sc[jb] = yp[:, jb * LANES:(jb + 1) * LANES]
    for v in range(nseq):
        b, p = v % nreal, v // nreal
        sg = (v // 8) * 8
        rows = pl.ds(sg * tc + (v - sg), tc, stride=8)
        y = jnp.concatenate([y_sc[p * nkb + jb, rows, :] for jb in range(nkb)], axis=1)
        y = y + d_ref[:, p * gk:(p + 1) * gk] * u_of(v)
        gy = (0.5 * y * (1.0 + lax.erf(y * (1.0 / math.sqrt(2.0))))).astype(gy_ref.dtype)
        if len(gy_ref.shape) == 3:
            gy_ref[b, :, p * gk:(p + 1) * gk] = gy
        else:
            gy_ref[b * tc:(b + 1) * tc, :] = gy

    @pl.when(step == pl.num_programs(1) - 1)
    def _fin():
        for p in range(pair):
            sre_ref[:, p * sw:(p + 1) * sw] = st_sc[p * nreal:(p + 1) * nreal, :sw]
            sim_ref[:, p * sw:(p + 1) * sw] = st_sc[p * nreal:(p + 1) * nreal, sw:]


def _s5_scratch(nseq, tc, gk, sw):
    rows = nseq * tc
    return [pltpu.VMEM((gk // LANES, rows, LANES), F32),
            pltpu.VMEM((gk // LANES, rows, LANES), F32),
            pltpu.VMEM((2 * sw // LANES, rows, LANES), F32),
            pltpu.VMEM((gk // LANES, rows, LANES), F32),
            pltpu.VMEM((nseq, 2 * sw), F32)]


def _s5_weight_specs(kin, gk, sw, pair, idx):
    return [pl.BlockSpec((1, kin, 2 * sw), lambda *a: (idx(*a), 0, 0)),
            pl.BlockSpec((1, 2 * sw, pair * gk), lambda *a: (idx(*a), 0, 0)),
            pl.BlockSpec((1, 2 * pair, sw), lambda *a: (idx(*a), 0, 0)),
            pl.BlockSpec((1, pair * gk), lambda *a: (0, idx(*a)))]


def _s5_prompt_call(u_all, wts, nb, t, tc, pair):
    bw, cmat, amat, dsk = wts
    ngp, kin, sw2 = bw.shape
    sw = sw2 // 2
    gk = cmat.shape[2] // pair
    ntc = t // tc
    gb_of = lambda g, s: g
    u_specs = [pl.BlockSpec((tc, pair * gk), functools.partial(lambda g, s, b: (b * ntc + s, g), b=b))
               for b in range(nb)]
    kern = functools.partial(_s5_kernel, n_u=nb, tc=tc, nreal=nb, pair=pair, sw=sw, has_h0=False)
    return pl.pallas_call(
        kern,
        grid=(ngp, ntc),
        in_specs=u_specs + _s5_weight_specs(kin, gk, sw, pair, gb_of),
        out_specs=[pl.BlockSpec((nb, tc, pair * gk), lambda g, s: (0, s, g)),
                   pl.BlockSpec((nb, pair * sw), lambda g, s: (0, g)),
                   pl.BlockSpec((nb, pair * sw), lambda g, s: (0, g))],
        out_shape=[jax.ShapeDtypeStruct((nb, t, ngp * pair * gk), BF16),
                   jax.ShapeDtypeStruct((nb, ngp * pair * sw), F32),
                   jax.ShapeDtypeStruct((nb, ngp * pair * sw), F32)],
        scratch_shapes=_s5_scratch(nb * pair, tc, pair * gk, sw),
        compiler_params=_cparams(("arbitrary", "arbitrary")),
        name="s5_prompt",
    )(*([u_all] * nb), bw, cmat, amat, dsk)


def _s5_sample_call(u_all, wts, h0_re, h0_im, ns, ts, row_block0, sb):
    b2, cmat, amat, dsk = wts
    ngb, gk2, sw2 = b2.shape
    gk, sw = gk2 // 2, sw2 // 2
    gb_of = lambda g, s, z: g
    kern = functools.partial(_s5_kernel, n_u=1, tc=ts, nreal=sb, pair=1, sw=sw, has_h0=True)
    return pl.pallas_call(
        kern,
        grid=(ngb, 1, ns // sb),
        in_specs=[pl.BlockSpec((sb * ts, gk), lambda g, s, z: (row_block0 + z, g))]
        + _s5_weight_specs(gk2, gk, sw, 1, gb_of)
        + [pl.BlockSpec((sb, sw), lambda g, s, z: (z, g)), pl.BlockSpec((sb, sw), lambda g, s, z: (z, g))],
        out_specs=[pl.BlockSpec((sb * ts, gk), lambda g, s, z: (z, g)),
                   pl.BlockSpec((sb, sw), lambda g, s, z: (z, g)),
                   pl.BlockSpec((sb, sw), lambda g, s, z: (z, g))],
        out_shape=[jax.ShapeDtypeStruct((ns * ts, ngb * gk), BF16),
                   jax.ShapeDtypeStruct((ns, ngb * sw), F32),
                   jax.ShapeDtypeStruct((ns, ngb * sw), F32)],
        scratch_shapes=_s5_scratch(sb, ts, gk, sw),
        compiler_params=_cparams(("arbitrary", "arbitrary", "arbitrary")),
        name="s5_sample",
    )(u_all, b2, cmat, amat, dsk, h0_re, h0_im)


def _flash_kernel(q_ref, c_ref, kr_ref, wk_ref, wvt_ref, o_ref, k_sc, vt_sc, m_sc, l_sc, acc_sc, *, tq, nope):
    hd = pl.program_id(1)
    qi = pl.program_id(2)
    nh, vd = vt_sc.shape[:2]

    @pl.when(jnp.logical_and(hd == 0, qi == 0))
    def _keys_values():
        c = c_ref[...]
        kr = kr_ref[...]
        for hp in range(0, nh, 2):
            kk = _dot(c, wk_ref[:, hp * nope:(hp + 2) * nope])
            vv = _dot_nt(wvt_ref[hp * vd:(hp + 2) * vd, :], c)
            for e in range(2):
                k_sc[hp + e, :, :nope] = kk[:, e * nope:(e + 1) * nope].astype(k_sc.dtype)
                k_sc[hp + e, :, nope:] = kr
                vt_sc[hp + e] = vv[e * vd:(e + 1) * vd, :].astype(vt_sc.dtype)

    q = q_ref[0]
    m_sc[...] = jnp.full_like(m_sc, NEG_BIG)
    l_sc[...] = jnp.zeros_like(l_sc)
    acc_sc[...] = jnp.zeros_like(acc_sc)

    def block(ki, diagonal):
        start = pl.multiple_of(ki * tq, tq)
        s = _dot_nt(k_sc[hd, pl.ds(start, tq), :], q)
        if diagonal:
            key = lax.broadcasted_iota(jnp.int32, s.shape, 0)
            qry = lax.broadcasted_iota(jnp.int32, s.shape, 1)
            s = jnp.where(key <= qry, s, NEG_BIG)
        m_prev = m_sc[...]
        m_new = jnp.maximum(m_prev, jnp.max(s, axis=0, keepdims=True))
        alpha = jnp.exp2(m_prev - m_new)
        p = jnp.exp2(s - m_new)
        l_sc[...] = alpha * l_sc[...] + jnp.sum(p, axis=0, keepdims=True)
        acc_sc[...] = alpha * acc_sc[...] + _dot(vt_sc[hd, :, pl.ds(start, tq)], p.astype(BF16))
        m_sc[...] = m_new

    def below_diagonal(ki, carry):
        block(ki, False)
        return carry

    lax.fori_loop(0, qi, below_diagonal, 0)
    block(qi, True)
    o_ref[...] = (acc_sc[...] / l_sc[...]).T.astype(o_ref.dtype)


def _flash_outer_kernel(*refs, tq, nb, nope):
    o_ref = refs[5]

    @pl.when(pl.program_id(0) < nb)
    def _rows():
        _flash_kernel(*refs, tq=tq, nope=nope)

    @pl.when(pl.program_id(0) == nb)
    def _tail():
        o_ref[...] = jnp.zeros_like(o_ref)


def _flash_call(q_hm, ckv_b, kr_b, w_uk2, w_uvt2, nb, t, m_total, tq):
    nh, _, qd = q_hm.shape
    lora = w_uk2.shape[0]
    nope = w_uk2.shape[1] // nh
    vd = w_uvt2.shape[0] // nh
    rw = kr_b.shape[1]
    assert nope + rw == qd
    nq = t // tq
    last = m_total // tq - 1
    assert nb * nq + nq > last
    bc = lambda b: jnp.minimum(b, nb - 1)
    hc = lambda b, h: jnp.where(b < nb, h, nh - 1)
    return pl.pallas_call(
        functools.partial(_flash_outer_kernel, tq=tq, nb=nb, nope=nope),
        grid=(nb + 1, nh, nq),
        in_specs=[pl.BlockSpec((1, tq, qd), lambda b, h, i: (hc(b, h), bc(b) * nq + jnp.where(b < nb, i, nq - 1), 0)),
                  pl.BlockSpec((t, lora), lambda b, h, i: (bc(b), 0)),
                  pl.BlockSpec((t, rw), lambda b, h, i: (bc(b), 0)),
                  pl.BlockSpec((lora, nh * nope), lambda b, h, i: (0, 0)),
                  pl.BlockSpec((nh * vd, lora), lambda b, h, i: (0, 0))],
        out_specs=pl.BlockSpec((tq, vd), lambda b, h, i: (jnp.minimum(b * nq + i, last), h)),
        out_shape=jax.ShapeDtypeStruct((m_total, nh * vd), BF16),
        scratch_shapes=[pltpu.VMEM((nh, t, qd), BF16), pltpu.VMEM((nh, vd, t), BF16),
                        pltpu.VMEM((1, tq), F32), pltpu.VMEM((1, tq), F32), pltpu.VMEM((vd, tq), F32)],
        compiler_params=pltpu.CompilerParams(dimension_semantics=("arbitrary", "arbitrary", "arbitrary"),
                                             vmem_limit_bytes=FLASH_VMEM_LIMIT),
        name="flash",
    )(q_hm, ckv_b, kr_b, w_uk2, w_uvt2)


def _qlat_kernel(q_ref, w_ref, o_ref):
    o_ref[0] = _dot(q_ref[0], w_ref[0]).astype(o_ref.dtype)


def _qlat_call(q_hm, w_ukt, ms, row_block0):
    nh, nope, lora = w_ukt.shape
    return pl.pallas_call(
        _qlat_kernel,
        grid=(nh,),
        in_specs=[pl.BlockSpec((1, ms, nope), lambda h: (h, row_block0, 0)),
                  pl.BlockSpec((1, nope, lora), lambda h: (h, 0, 0))],
        out_specs=pl.BlockSpec((1, ms, lora), lambda h: (h, 0, 0)),
        out_shape=jax.ShapeDtypeStruct((nh, ms, lora), BF16),
        compiler_params=_cparams(("arbitrary",)),
        name="qlat",
    )(q_hm, w_ukt)


def _decode_kernel(pt_ref, ql_ref, qr_ref, cn_ref, kn_ref, ck_hbm, kr_hbm, o_ref,
                   ck_buf, kr_buf, sem, qlb_sc, qrb_sc, kb_sc, rb_sc, m_sc, l_sc, acc_sc, *, npg, sq, ts, rope, n_pages):
    s_blk = pl.program_id(0)
    j = pl.program_id(1)
    nsteps = pl.num_programs(1)
    g = s_blk * nsteps + j
    total = pl.num_programs(0) * nsteps
    nslot = ck_buf.shape[0]
    ahead = nslot - 1
    slot = lax.rem(g, nslot)
    nh, _, lora = ql_ref.shape
    rows = nh * ts
    pg = ck_buf.shape[2]

    def page_copies(step):
        sb = lax.div(step, nsteps)
        jj = lax.rem(step, nsteps)
        sl = lax.rem(step, nslot)
        out = []
        for a in range(sq):
            for i in range(npg):
                page = pt_ref[(sb * sq + a) * n_pages + jj * npg + i]
                out.append(pltpu.make_async_copy(ck_hbm.at[page], ck_buf.at[sl, a * npg + i], sem.at[0, sl]))
                out.append(pltpu.make_async_copy(kr_hbm.at[page], kr_buf.at[sl, a * npg + i], sem.at[1, sl]))
        return out

    for k in range(ahead):
        @pl.when(jnp.logical_and(g == 0, k < total))
        def _prime(k=k):
            for c in page_copies(k):
                c.start()

    @pl.when(g + ahead < total)
    def _start_ahead():
        for c in page_copies(g + ahead):
            c.start()

    for c in page_copies(g):
        c.wait()
    ck_refs = [ck_buf.at[slot, i] for i in range(sq * npg)]
    kr_refs = [kr_buf.at[slot, i] for i in range(sq * npg)]

    @pl.when(j == 0)
    def _init():
        ql = ql_ref[...].astype(F32)
        for a in range(sq):
            qlb_sc[a] = ql[:, a * ts:(a + 1) * ts, :].reshape(rows, lora).astype(BF16)
            qrb_sc[a] = qr_ref[:, a * ts:(a + 1) * ts, :].reshape(rows, rope).astype(BF16)
        m_sc[...] = jnp.full_like(m_sc, NEG_BIG)
        l_sc[...] = jnp.zeros_like(l_sc)
        acc_sc[...] = jnp.zeros_like(acc_sc)

    def update(a, s, kb):
        m_prev = m_sc[a]
        m_new = jnp.maximum(m_prev, jnp.max(s, axis=-1, keepdims=True))
        alpha = jnp.exp2(m_prev - m_new)
        p = jnp.exp2(s - m_new)
        l_sc[a] = alpha * l_sc[a] + jnp.sum(p, axis=-1, keepdims=True)
        acc_sc[a] = alpha * acc_sc[a] + _dot(p.astype(BF16), kb)
        m_sc[a] = m_new

    for a in range(sq):
        for i in range(npg):
            kb_sc[a, i * pg:(i + 1) * pg, :] = ck_refs[a * npg + i][...].astype(BF16)
            rb_sc[a, :, i * pg:(i + 1) * pg] = kr_refs[a * npg + i][...].astype(BF16)
    for a in range(sq):
        kb = kb_sc[a]
        update(a, _dot_nt(qlb_sc[a], kb) + _dot(qrb_sc[a], rb_sc[a]), kb)

    @pl.when(j == pl.num_programs(1) - 1)
    def _fin():
        outs = []
        for a in range(sq):
            kn = jnp.concatenate([cn_ref[a * ts:(a + 1) * ts, :], jnp.zeros((pg - ts, lora), F32)], axis=0).astype(BF16)
            rn = jnp.concatenate([kn_ref[a * ts:(a + 1) * ts, :], jnp.zeros((pg - ts, rope), F32)], axis=0).astype(BF16)
            s = _dot_nt(qlb_sc[a], kn) + _dot_nt(qrb_sc[a], rn)
            tq = lax.broadcasted_iota(jnp.int32, s.shape, 0) % ts
            tk = lax.broadcasted_iota(jnp.int32, s.shape, 1)
            update(a, jnp.where(tk <= tq, s, NEG_BIG), kn)
            outs.append((acc_sc[a] / l_sc[a]).reshape(nh, ts, lora))
        o_ref[...] = jnp.concatenate(outs, axis=1).astype(o_ref.dtype)


def _decode_call(page_table, q_lat, q_rope, ckv_all, kr_all, cache_ckv, cache_krope_t, ns, ts, row_block0, npg, sq):
    nh, ms, lora = q_lat.shape
    rope = q_rope.shape[2]
    n_pages = page_table.shape[1]
    pg = cache_ckv.shape[1]
    nsteps = n_pages // npg
    rows = nh * ts

    n = sq * npg
    grid_spec = pltpu.PrefetchScalarGridSpec(
        num_scalar_prefetch=1,
        grid=(ns // sq, nsteps),
        in_specs=[pl.BlockSpec((nh, sq * ts, lora), lambda s, j, pt: (0, s, 0)),
                  pl.BlockSpec((nh, sq * ts, rope), lambda s, j, pt: (0, s, 0)),
                  pl.BlockSpec((sq * ts, lora), lambda s, j, pt: (row_block0 + s, 0)),
                  pl.BlockSpec((sq * ts, rope), lambda s, j, pt: (row_block0 + s, 0)),
                  pl.BlockSpec(memory_space=pl.ANY),
                  pl.BlockSpec(memory_space=pl.ANY)],
        out_specs=pl.BlockSpec((nh, sq * ts, lora), lambda s, j, pt: (0, s, 0)),
        scratch_shapes=[pltpu.VMEM((DECODE_SLOTS, n, pg, lora), F32), pltpu.VMEM((DECODE_SLOTS, n, rope, pg), F32),
                        pltpu.SemaphoreType.DMA((2, DECODE_SLOTS)),
                        pltpu.VMEM((sq, rows, lora), BF16), pltpu.VMEM((sq, rows, rope), BF16),
                        pltpu.VMEM((sq, npg * pg, lora), BF16), pltpu.VMEM((sq, rope, npg * pg), BF16),
                        pltpu.VMEM((sq, rows, 1), F32), pltpu.VMEM((sq, rows, 1), F32),
                        pltpu.VMEM((sq, rows, lora), F32)],
    )
    return pl.pallas_call(
        functools.partial(_decode_kernel, npg=npg, sq=sq, ts=ts, rope=rope, n_pages=n_pages),
        grid_spec=grid_spec,
        out_shape=jax.ShapeDtypeStruct((nh, ms, lora), BF16),
        compiler_params=_cparams(("arbitrary", "arbitrary")),
        name="decode",
    )(page_table.reshape(-1), q_lat, q_rope, ckv_all, kr_all, cache_ckv, cache_krope_t)


def _vup_kernel(o_ref, w_ref, att_any, out_ref):
    del att_any
    out_ref[...] = _dot(o_ref[0].astype(BF16), w_ref[0]).astype(out_ref.dtype)


def _vup_call(o_lat, w_uv3, att_all, row_block0):
    nh, ms, lora = o_lat.shape
    vd = w_uv3.shape[2]
    return pl.pallas_call(
        _vup_kernel,
        grid=(nh,),
        in_specs=[pl.BlockSpec((1, ms, lora), lambda h: (h, 0, 0)),
                  pl.BlockSpec((1, lora, vd), lambda h: (h, 0, 0)),
                  pl.BlockSpec(memory_space=pl.ANY)],
        out_specs=pl.BlockSpec((ms, vd), lambda h: (row_block0, h)),
        out_shape=jax.ShapeDtypeStruct(att_all.shape, att_all.dtype),
        input_output_aliases={2: 0},
        compiler_params=_cparams(("arbitrary",)),
        name="vup",
    )(o_lat, w_uv3, att_all)


def _glu_kernel(gp_ref, gs_ref, wa_ref, wb_ref, z_ref, o_ref, *, np_tiles):
    g = jnp.where(pl.program_id(0) < np_tiles, gp_ref[...], gs_ref[...])
    ga = _dot(g, wa_ref[...])
    gb = _dot(g, wb_ref[...])
    o_ref[...] = (ga * _sigmoid(gb) * z_ref[...].astype(F32)).astype(o_ref.dtype)


def _glu_call(gy_p, gy_s, w_glu, zs_act, tm, tn):
    mp, k = gy_p.shape
    ms = gy_s.shape[0]
    m = mp + ms
    n = w_glu.shape[1] // 2
    nn = n // tn
    npt, nst = mp // tm, ms // tm
    return pl.pallas_call(
        functools.partial(_glu_kernel, np_tiles=npt),
        grid=(m // tm, nn),
        in_specs=[pl.BlockSpec((tm, k), lambda i, j: (jnp.minimum(i, npt - 1), 0)),
                  pl.BlockSpec((tm, k), lambda i, j: (jnp.clip(i - npt, 0, nst - 1), 0)),
                  pl.BlockSpec((k, tn), lambda i, j: (0, j)),
                  pl.BlockSpec((k, tn), lambda i, j: (0, nn + j)),
                  pl.BlockSpec((tm, tn), lambda i, j: (i, j))],
        out_specs=pl.BlockSpec((tm, tn), lambda i, j: (i, j)),
        out_shape=jax.ShapeDtypeStruct((m, n), BF16),
        compiler_params=_cparams(("arbitrary", "arbitrary")),
        name="glu",
    )(gy_p, gy_s, w_glu, w_glu, zs_act)


def _merge_kernel(ys_ref, att_ref, za_ref, wbs_ref, wo_ref, gs_ref, ga_ref, o_ref, v_sc):
    @pl.when(pl.program_id(1) == 0)
    def _gate():
        v_sc[...] = (att_ref[...].astype(F32) * za_ref[...].astype(F32)).astype(BF16)

    p_s = _dot(ys_ref[...], wbs_ref[...])
    p_a = _dot(v_sc[...], wo_ref[...])
    o_ref[...] = (gs_ref[...].astype(F32) * p_s + ga_ref[...].astype(F32) * p_a).astype(o_ref.dtype)


def _merge_call(ys2, att, za_act, w_bs, w_o, gates, tm, tn):
    m, ks = ys2.shape
    ka = att.shape[1]
    n = w_bs.shape[1]
    nn = n // tn
    return pl.pallas_call(
        _merge_kernel,
        grid=(m // tm, nn),
        in_specs=[pl.BlockSpec((tm, ks), lambda i, j: (i, 0)),
                  pl.BlockSpec((tm, ka), lambda i, j: (i, 0)),
                  pl.BlockSpec((tm, ka), lambda i, j: (i, 0)),
                  pl.BlockSpec((ks, tn), lambda i, j: (0, j)),
                  pl.BlockSpec((ka, tn), lambda i, j: (0, j)),
                  pl.BlockSpec((tm, tn), lambda i, j: (i, j)),
                  pl.BlockSpec((tm, tn), lambda i, j: (i, nn + j))],
        out_specs=pl.BlockSpec((tm, tn), lambda i, j: (i, j)),
        out_shape=jax.ShapeDtypeStruct((m, n), BF16),
        scratch_shapes=[pltpu.VMEM((tm, ka), BF16)],
        compiler_params=_cparams(("arbitrary", "arbitrary")),
        name="merge",
    )(ys2, att, za_act, w_bs, w_o, gates, gates)


def _final_kernel(mg_ref, w_ref, x_ref, gate_ref, g_ref, b_ref, o_ref, *, alpha):
    out = _dot(mg_ref[...], w_ref[...])
    x = x_ref[...]
    y = alpha * x + gate_ref[...] * out.reshape(x.shape)
    mu = jnp.mean(y, axis=-1, keepdims=True)
    yc = y - mu
    var = jnp.mean(yc * yc, axis=-1, keepdims=True)
    o_ref[...] = yc * lax.rsqrt(var + LN_EPS) * g_ref[...] + b_ref[...]


def _final_call(merged, w_out, x, mod3, ln_g, ln_b, alpha, x_blk, gate_blk, gate_map, merged_map, grid):
    d = w_out.shape[0]
    rows = x_blk[0] * x_blk[1]
    nd = len(grid)
    x_map = (lambda i, j: (i, j, 0)) if nd == 2 else (lambda i: (i, 0, 0))
    const2 = (lambda i, j: (0, 0)) if nd == 2 else (lambda i: (0, 0))
    const3 = (lambda i, j: (0, 0, 0)) if nd == 2 else (lambda i: (0, 0, 0))
    return pl.pallas_call(
        functools.partial(_final_kernel, alpha=alpha),
        grid=grid,
        in_specs=[pl.BlockSpec((rows, d), merged_map),
                  pl.BlockSpec((d, d), const2),
                  pl.BlockSpec(x_blk, x_map),
                  pl.BlockSpec(gate_blk, gate_map),
                  pl.BlockSpec((1, 1, d), const3),
                  pl.BlockSpec((1, 1, d), const3)],
        out_specs=pl.BlockSpec(x_blk, x_map),
        out_shape=jax.ShapeDtypeStruct(x.shape, F32),
        compiler_params=_cparams(("arbitrary",) * nd),
        name="final",
    )(merged, w_out, x, mod3, ln_g.reshape(1, 1, d), ln_b.reshape(1, 1, d))


def _rope_tables(pos, rope):
    freqs = ROPE_BASE ** (-jnp.arange(0, rope, 2, dtype=F32) / rope)
    ang = pos.astype(F32)[:, None] * freqs[None, :]
    cos, sin = jnp.cos(ang), jnp.sin(ang)
    return jnp.concatenate([cos] * 4 + [sin] * 4, axis=-1)


def _rot_rows(wt, rope):
    half = rope // 2
    return jnp.concatenate([-wt[..., half:, :], wt[..., :half, :]], axis=-2)


def _block_diag(x, gblk):
    g, r, c = x.shape
    x4 = x.reshape(g // gblk, gblk, r, c)
    eye = jnp.eye(gblk, dtype=x.dtype)
    return jnp.einsum("bgrc,gh->bgrhc", x4, eye).reshape(g // gblk, gblk * r, gblk * c)


def kernel(x_prompt, x_sample, c_prompt, c_sample, cache_ckv, cache_krope, state_ssm_re, state_ssm_im, page_table, w_ada, b_ada, w_in, g_kv, w_uk, w_uv, w_o, lam_re, lam_im, log_dt, b_re, b_im, c_re, c_im, d_skip, w_glu, w_bs, w_out, ln_g, ln_b):
    nb, t, d = x_prompt.shape
    ns, ts, _ = x_sample.shape
    depth = w_in.shape[0]
    assert depth == 1, "single-layer step"
    lora, nh, nope = w_uk.shape[1:]
    vd = w_uv.shape[3]
    rope = cache_krope.shape[3]
    d_ssm = d_skip.shape[1]
    ngrp, nstate, gch = b_re.shape[1:]
    d_attn = nh * vd
    pg = cache_ckv.shape[2]
    past = page_table.shape[1] * pg
    mp, ms = nb * t, ns * ts
    m = mp + ms
    scale = float((nope + rope) ** -0.5) * LOG2E
    alpha = float((2 * depth) ** 0.25)

    tm = math.gcd(math.gcd(mp, ms), 1024)
    tmh = min(tm, 512)
    tn = 512
    assert rope * 2 == 128 and nope == 128 and vd == 128

    wt = jnp.swapaxes(w_in, 1, 2)[0]
    o_u, o_zs, o_q = 0, d_ssm, 2 * d_ssm
    o_ckv = o_q + nh * (nope + rope)
    o_kr = o_ckv + lora
    o_za = o_kr + rope
    o_gs = o_za + d_attn
    o_ga = o_gs + d
    w_u = wt[o_u:o_zs].astype(BF16)
    w_za = wt[o_za:o_gs].astype(BF16)
    w_zs = wt[o_zs:o_q].astype(BF16)
    w_g = wt[o_gs:].astype(BF16)
    wq = wt[o_q:o_ckv].reshape(nh, nope + rope, d)
    qhb = 4
    wq4 = wq.reshape(nh // qhb, qhb, nope + rope, d)
    w_q4 = jnp.concatenate([wq4[:, :, :nope, :].reshape(nh // qhb, qhb * nope, d),
                            wq4[:, :, nope:, :].reshape(nh // qhb, qhb * rope, d)], axis=1).astype(BF16)
    wkr = wt[o_kr:o_za]
    w_ck = jnp.concatenate([wt[o_ckv:o_kr], wkr, _rot_rows(wkr, rope)], axis=0).astype(BF16)
    w_uk2 = w_uk[0].reshape(lora, nh * nope).astype(BF16)
    w_uvt2 = w_uv[0].transpose(1, 2, 0).reshape(nh * vd, lora).astype(BF16)
    w_ukt = w_uk[0].transpose(1, 2, 0).astype(BF16)
    w_uv3 = w_uv[0].transpose(1, 0, 2).astype(BF16)
    w_glu_b = w_glu[0].astype(BF16)
    w_bs_b = w_bs[0].astype(BF16)
    w_o_b = w_o[0].astype(BF16)
    w_out_b = w_out[0].astype(BF16)

    cs4 = jnp.concatenate([jnp.tile(_rope_tables(jnp.arange(t), rope), (nb, 1)),
                           jnp.tile(_rope_tables(past + jnp.arange(ts), rope), (ns, 1))], axis=0)
    cs = jnp.concatenate([cs4[:, :rope], cs4[:, 2 * rope:3 * rope]], axis=1)

    c_all = jnp.concatenate([c_sample, c_prompt], axis=0)
    pad = (-c_all.shape[0]) % 8
    c_all = jnp.concatenate([c_all, jnp.zeros((pad, d), F32)], axis=0)
    mod = _mod_call(c_all, w_ada[0], b_ada[0], tn)
    mod3 = mod.reshape(mod.shape[0], 1, 3 * d)

    sb = max(1, min(ns, tmh // ts))
    h_all = _hmod_prompt(x_prompt, mod3, ns, m, tmh)
    h_all = _hmod_sample(x_sample, mod3, h_all, mp // (sb * ts), sb)

    u_all = _proj_call(h_all, w_u, None, F32, tm, tn, "proj_u")
    za_act = _proj_call(h_all, w_za, "silu", BF16, tm, 2 * tn, "proj_za")
    zs_act = _proj_call(h_all, w_zs, "silu", BF16, tm, 2 * tn, "proj_zs")
    gates = _proj_call(h_all, w_g, "sigmoid", BF16, tm, 2 * tn, "proj_g")
    q_hm = _qproj_call(h_all, w_q4, cs4, tm, nh, nope, rope, scale, qhb)
    ckv_all, kr_all, ckv_b, kr_b = _ckv_call(h_all, w_ck, g_kv[0], cs, tmh, lora, rope)

    gblk = SSM_GROUP_BLOCK
    a_re, a_im, bb_re, bb_im = _s5_prep_call(lam_re[0], lam_im[0], log_dt[0],
                                             b_re[0].transpose(0, 2, 1), b_im[0].transpose(0, 2, 1))
    bcat = jnp.concatenate([_block_diag(bb_re, gblk), _block_diag(bb_im, gblk)], axis=-1)
    b2 = jnp.concatenate([bcat, bcat], axis=1).astype(BF16)
    ccat = jnp.concatenate([_block_diag(c_re[0].transpose(0, 2, 1), gblk),
                            _block_diag(-c_im[0].transpose(0, 2, 1), gblk)], axis=1).astype(BF16)
    ngb = ngrp // gblk
    sw = gblk * nstate
    amat = jnp.stack([a_re.reshape(ngb, sw), a_im.reshape(ngb, sw)], axis=1)
    s5w = (b2, ccat, amat, d_skip)
    tc = min(t, S5_TIME_CHUNK)
    pair = 2 if (2 * nb <= 8 and ngb % 2 == 0) else 1
    gkb = gblk * gch
    s5w_p = s5w if pair == 1 else (
        bcat.astype(BF16).reshape(ngb // 2, 2 * gkb, 2 * sw),
        ccat.reshape(ngb // 2, 2, 2 * sw, gkb).transpose(0, 2, 1, 3).reshape(ngb // 2, 2 * sw, 2 * gkb),
        amat.reshape(ngb // 2, 4, sw), d_skip)
    gy_p, sre_p, sim_p = _s5_prompt_call(u_all, s5w_p, nb, t, tc, pair)
    ssb = min(ns, 64)
    gy_s, sre_s, sim_s = _s5_sample_call(u_all, s5w, state_ssm_re[0].reshape(ns, ngrp * nstate),
                                         state_ssm_im[0].reshape(ns, ngrp * nstate), ns, ts, mp // (ssb * ts), ssb)

    tq = min(t, FLASH_BLOCK)
    att = _flash_call(q_hm, ckv_b, kr_b, w_uk2, w_uvt2, nb, t, m, tq)
    q_lat = _qlat_call(q_hm, w_ukt, ms, mp // ms)
    q_rope_s = q_hm[:, mp:, nope:nope + rope].astype(F32)
    sq = DECODE_SEQS_PER_STEP
    krope_t = jnp.swapaxes(cache_krope, 2, 3).reshape(cache_krope.shape[1], rope, pg)
    o_lat = _decode_call(page_table, q_lat, q_rope_s, ckv_all, kr_all, cache_ckv.reshape(cache_ckv.shape[1:]), krope_t,
                         ns, ts, mp // (sq * ts), PAGES_PER_STEP, sq)
    att = _vup_call(o_lat, w_uv3, att, mp // ms)

    ys2 = _glu_call(gy_p.reshape(mp, d_ssm), gy_s, w_glu_b, zs_act, tm, tn)
    merged = _merge_call(ys2, att, za_act, w_bs_b, w_o_b, gates, tm, tn)
    ntp = t // tmh
    y_p = _final_call(merged, w_out_b, x_prompt, mod3, ln_g[0], ln_b[0], alpha,
                      (1, tmh, d), (1, 1, d), lambda i, j: (ns + i, 0, 2), lambda i, j: (i * ntp + j, 0), (nb, ntp))
    y_s = _final_call(merged, w_out_b, x_sample, mod3, ln_g[0], ln_b[0], alpha,
                      (sb, ts, d), (sb, 1, d), lambda i: (i, 0, 2), lambda i: (mp // (sb * ts) + i, 0), (ns // sb,))

    st = lambda a, n: a.reshape(1, n, ngrp, nstate)
    return (y_p, y_s,
            ckv_all[:mp].reshape(1, nb, t, lora), kr_all[:mp].reshape(1, nb, t, rope), st(sre_p, nb), st(sim_p, nb),
            ckv_all[mp:].reshape(1, ns, ts, lora), kr_all[mp:].reshape(1, ns, ts, rope), st(sre_s, ns), st(sim_s, ns))
```

```python
import functools
import math

import jax
import jax.numpy as jnp
from jax import lax
from jax.experimental import pallas as pl
from jax.experimental.pallas import tpu as pltpu

F32 = jnp.float32
BF16 = jnp.bfloat16

LN_EPS = 1e-5
RMS_EPS = 1e-6
ROPE_BASE = 10000.0
NEG_BIG = -1e30
VMEM_LIMIT = 48 * 1024 * 1024
FLASH_VMEM_LIMIT = 56 * 1024 * 1024
PAGES_PER_STEP = 16
DECODE_SEQS_PER_STEP = 2
DECODE_SLOTS = 3
LOG2E = 1.4426950408889634
SSM_GROUP_BLOCK = 8
LANES = 128
S5_TIME_CHUNK = 256
FLASH_BLOCK = 1024


def _cparams(sem):
    return pltpu.CompilerParams(dimension_semantics=sem, vmem_limit_bytes=VMEM_LIMIT)


def _sigmoid(x):
    return 1.0 / (1.0 + jnp.exp(-x))


def _silu(x):
    return x * _sigmoid(x)


def _dot(a, b):
    return jnp.dot(a, b, preferred_element_type=F32)


def _dot_nt(a, b):
    return lax.dot_general(a, b, (((1,), (1,)), ((), ())), preferred_element_type=F32)


def _mod_kernel(c_ref, w_ref, b_ref, o_ref):
    a = _silu(c_ref[...]).astype(BF16)
    o_ref[...] = _dot(a, w_ref[...].astype(BF16)) + b_ref[...]


def _mod_call(c_all, w_ada, b_ada, tn):
    r, d = c_all.shape
    n = w_ada.shape[1]
    return pl.pallas_call(
        _mod_kernel,
        grid=(n // tn,),
        in_specs=[pl.BlockSpec((r, d), lambda j: (0, 0)),
                  pl.BlockSpec((d, tn), lambda j: (0, j)),
                  pl.BlockSpec((1, tn), lambda j: (0, j))],
        out_specs=pl.BlockSpec((r, tn), lambda j: (0, j)),
        out_shape=jax.ShapeDtypeStruct((r, n), F32),
        compiler_params=_cparams(("arbitrary",)),
        name="mod",
    )(c_all, w_ada, b_ada.reshape(1, n))


def _hmod_kernel(x_ref, sh_ref, sc_ref, *rest):
    o_ref = rest[-1]
    h = x_ref[...] * (1.0 + sc_ref[...]) + sh_ref[...]
    o_ref[...] = h.reshape(o_ref.shape).astype(o_ref.dtype)


def _hmod_prompt_kernel(x_ref, sh_ref, sc_ref, o_ref, *, nb):
    @pl.when(pl.program_id(0) < nb)
    def _rows():
        _hmod_kernel(x_ref, sh_ref, sc_ref, o_ref)

    @pl.when(pl.program_id(0) == nb)
    def _tail():
        o_ref[...] = jnp.zeros_like(o_ref)


def _hmod_prompt(x, mod3, mod_row0, m_total, tm):
    b, t, d = x.shape
    nt = t // tm
    last = m_total // tm - 1
    assert b * nt + nt > last
    bc = lambda i: jnp.minimum(i, b - 1)
    return pl.pallas_call(
        functools.partial(_hmod_prompt_kernel, nb=b),
        grid=(b + 1, nt),
        in_specs=[pl.BlockSpec((1, tm, d), lambda i, j: (bc(i), jnp.where(i < b, j, nt - 1), 0)),
                  pl.BlockSpec((1, 1, d), lambda i, j: (mod_row0 + bc(i), 0, 0)),
                  pl.BlockSpec((1, 1, d), lambda i, j: (mod_row0 + bc(i), 0, 1))],
        out_specs=pl.BlockSpec((tm, d), lambda i, j: (jnp.minimum(i * nt + j, last), 0)),
        out_shape=jax.ShapeDtypeStruct((m_total, d), BF16),
        compiler_params=_cparams(("arbitrary", "arbitrary")),
        name="hmod_prompt",
    )(x, mod3, mod3)


def _hmod_sample(x, mod3, h_all, row_block0, sb):
    ns, ts, d = x.shape
    return pl.pallas_call(
        _hmod_kernel,
        grid=(ns // sb,),
        in_specs=[pl.BlockSpec((sb, ts, d), lambda i: (i, 0, 0)),
                  pl.BlockSpec((sb, 1, d), lambda i: (i, 0, 0)),
                  pl.BlockSpec((sb, 1, d), lambda i: (i, 0, 1)),
                  pl.BlockSpec(memory_space=pl.ANY)],
        out_specs=pl.BlockSpec((sb * ts, d), lambda i: (row_block0 + i, 0)),
        out_shape=jax.ShapeDtypeStruct(h_all.shape, h_all.dtype),
        input_output_aliases={3: 0},
        compiler_params=_cparams(("arbitrary",)),
        name="hmod_sample",
    )(x, mod3, mod3, h_all)


def _proj_kernel(h_ref, w_ref, o_ref, *, act):
    acc = _dot_nt(h_ref[...], w_ref[...])
    if act == "silu":
        acc = _silu(acc)
    elif act == "sigmoid":
        acc = _sigmoid(acc)
    o_ref[...] = acc.astype(o_ref.dtype)


def _proj_call(h, w, act, out_dtype, tm, tn, name):
    m, k = h.shape
    n = w.shape[0]
    return pl.pallas_call(
        functools.partial(_proj_kernel, act=act),
        grid=(m // tm, n // tn),
        in_specs=[pl.BlockSpec((tm, k), lambda i, j: (i, 0)),
                  pl.BlockSpec((tn, k), lambda i, j: (j, 0))],
        out_specs=pl.BlockSpec((tm, tn), lambda i, j: (i, j)),
        out_shape=jax.ShapeDtypeStruct((m, n), out_dtype),
        compiler_params=_cparams(("arbitrary", "arbitrary")),
        name=name,
    )(h, w)


def _qproj_kernel(h_ref, w_ref, cs_ref, o_ref, *, nope, rope, scale, hb):
    half = rope // 2
    acc = _dot_nt(h_ref[...], w_ref[0])
    cos = cs_ref[:, :LANES]
    sin = cs_ref[:, LANES:]
    lane = lax.broadcasted_iota(jnp.int32, cos.shape, 1)
    first_half = (lane & half) == 0
    for pr in range(hb // 2):
        x = acc[:, hb * nope + pr * LANES:hb * nope + (pr + 1) * LANES]
        rot = jnp.where(first_half, -pltpu.roll(x, LANES - half, 1), pltpu.roll(x, half, 1))
        r = (x * cos + rot * sin) * scale
        for e, rr in ((2 * pr, r), (2 * pr + 1, pltpu.roll(r, rope, 1))):
            o_ref[e, :, :nope] = (acc[:, e * nope:(e + 1) * nope] * scale).astype(o_ref.dtype)
            o_ref[e, :, nope:] = jnp.where(lane < rope, rr, 0.0).astype(o_ref.dtype)


def _qproj_call(h, w_q4, cs4, tm, nh, nope, rope, scale, hb):
    m, k = h.shape
    wd = w_q4.shape[1]
    assert 2 * rope == LANES and nope == LANES and hb % 2 == 0 and wd == hb * (nope + rope)
    return pl.pallas_call(
        functools.partial(_qproj_kernel, nope=nope, rope=rope, scale=scale, hb=hb),
        grid=(m // tm, nh // hb),
        in_specs=[pl.BlockSpec((tm, k), lambda i, j: (i, 0)),
                  pl.BlockSpec((1, wd, k), lambda i, j: (j, 0, 0)),
                  pl.BlockSpec((tm, 2 * LANES), lambda i, j: (i, 0))],
        out_specs=pl.BlockSpec((hb, tm, nope + LANES), lambda i, j: (j, i, 0)),
        out_shape=jax.ShapeDtypeStruct((nh, m, nope + LANES), BF16),
        compiler_params=_cparams(("arbitrary", "arbitrary")),
        name="qproj",
    )(h, w_q4, cs4)


def _ckv_kernel(h_ref, w_ref, g_ref, cs_ref, ckv_ref, kr_ref, ckvb_ref, krb_ref, *, lora, rope):
    acc = _dot_nt(h_ref[...], w_ref[...])
    c = acc[:, :lora]
    ms = jnp.mean(c * c, axis=-1, keepdims=True)
    ckv = c * lax.rsqrt(ms + RMS_EPS) * g_ref[...]
    ckv_ref[...] = ckv
    ckvb_ref[...] = ckv.astype(BF16)
    t = acc[:, lora:] * cs_ref[...]
    r = t + pltpu.roll(t, rope, 1)
    kr_ref[...] = r[:, :rope]
    lane = lax.broadcasted_iota(jnp.int32, r.shape, 1)
    krb_ref[...] = jnp.where(lane < rope, r, 0.0).astype(BF16)


def _ckv_call(h, w_ck, g_kv, cs, tm, lora, rope):
    m, k = h.shape
    wd = w_ck.shape[0]
    return pl.pallas_call(
        functools.partial(_ckv_kernel, lora=lora, rope=rope),
        grid=(m // tm,),
        in_specs=[pl.BlockSpec((tm, k), lambda i: (i, 0)),
                  pl.BlockSpec((wd, k), lambda i: (0, 0)),
                  pl.BlockSpec((1, lora), lambda i: (0, 0)),
                  pl.BlockSpec((tm, 2 * rope), lambda i: (i, 0))],
        out_specs=[pl.BlockSpec((tm, lora), lambda i: (i, 0)),
                   pl.BlockSpec((tm, rope), lambda i: (i, 0)),
                   pl.BlockSpec((tm, lora), lambda i: (i, 0)),
                   pl.BlockSpec((tm, 2 * rope), lambda i: (i, 0))],
        out_shape=[jax.ShapeDtypeStruct((m, lora), F32),
                   jax.ShapeDtypeStruct((m, rope), F32),
                   jax.ShapeDtypeStruct((m, lora), BF16),
                   jax.ShapeDtypeStruct((m, 2 * rope), BF16)],
        compiler_params=_cparams(("arbitrary",)),
        name="ckvproj",
    )(h, w_ck, g_kv.reshape(1, lora), cs)


def _s5_prep_kernel(lre_ref, lim_ref, ldt_ref, bre_ref, bim_ref, are_ref, aim_ref, bbre_ref, bbim_ref):
    lre = lre_ref[...]
    lim = lim_ref[...]
    dt = jnp.exp(ldt_ref[...])
    mag = jnp.exp(lre * dt)
    a_re = mag * jnp.cos(lim * dt)
    a_im = mag * jnp.sin(lim * dt)
    are_ref[...] = a_re
    aim_ref[...] = a_im
    den = lre * lre + lim * lim
    nr = a_re - 1.0
    c_re = (nr * lre + a_im * lim) / den
    c_im = (a_im * lre - nr * lim) / den
    bre = bre_ref[...]
    bim = bim_ref[...]
    bbre_ref[...] = c_re * bre - c_im * bim
    bbim_ref[...] = c_re * bim + c_im * bre


def _s5_prep_call(lam_re, lam_im, log_dt, b_re_t, b_im_t):
    g, p = lam_re.shape
    n = b_re_t.shape[1]
    a_re, a_im, bb_re, bb_im = pl.pallas_call(
        _s5_prep_kernel,
        out_shape=[jax.ShapeDtypeStruct((g, 1, p), F32), jax.ShapeDtypeStruct((g, 1, p), F32),
                   jax.ShapeDtypeStruct((g, n, p), F32), jax.ShapeDtypeStruct((g, n, p), F32)],
        name="s5prep",
    )(lam_re.reshape(g, 1, p), lam_im.reshape(g, 1, p), log_dt.reshape(g, 1, 1), b_re_t, b_im_t)
    return a_re.reshape(g, p), a_im.reshape(g, p), bb_re, bb_im


def _s5_kernel(*refs, n_u, tc, nreal, pair, sw, has_h0):
    u_refs = refs[:n_u]
    i = n_u
    b2_ref, c_ref, a_ref, d_ref = refs[i:i + 4]
    i += 4
    if has_h0:
        h0re_ref, h0im_ref = refs[i:i + 2]
        i += 2
    gy_ref, sre_ref, sim_ref = refs[i:i + 3]
    i += 3
    us_sc, ut_sc, bu_sc, y_sc, st_sc = refs[i:i + 5]
    nseq = nreal * pair
    gk = d_ref.shape[1] // pair
    assert tc % 8 == 0 and nseq % 8 == 0 and (pair == 1 or n_u == nreal)

    step = pl.program_id(1)

    @pl.when(step == 0)
    def _init():
        if has_h0:
            st_sc[:, :sw] = h0re_ref[...]
            st_sc[:, sw:] = h0im_ref[...]
        else:
            st_sc[...] = jnp.zeros_like(st_sc)

    def u_of(v):
        b, p = v % nreal, v // nreal
        return u_refs[b][:, p * gk:(p + 1) * gk] if n_u > 1 else u_refs[0][b * tc:(b + 1) * tc, :]

    nkb = gk // LANES
    nub = pair * nkb
    for v in range(nseq):
        p, uv = v // nreal, u_of(v)
        for kb in range(nub):
            own = kb // nkb == p
            us_sc[kb, v * tc:(v + 1) * tc, :] = (uv[:, (kb - p * nkb) * LANES:(kb - p * nkb + 1) * LANES] if own
                                                 else jnp.zeros((tc, LANES), F32))
    for sg in range(0, nseq, 8):
        def permute(t, c, base=sg * tc):
            dst = pl.multiple_of(base + t * 8, 8)
            for kb in range(nub):
                ut_sc[kb, pl.ds(dst, 8), :] = us_sc[kb, pl.ds(base + t, 8, stride=tc), :]
            return c

        lax.fori_loop(0, tc, permute, 0, unroll=8)
    up = jnp.concatenate([ut_sc[kb] for kb in range(nub)], axis=1)
    if pair == 1:
        u_hi = up.astype(BF16)
        u_lo = (up - u_hi.astype(F32)).astype(BF16)
        lhs = jnp.concatenate([u_hi, u_lo], axis=1)
    else:
        lhs = up.astype(BF16)
    bu = _dot(lhs, b2_ref[0])
    nj = sw // LANES
    for j in range(2 * nj):
        bu_sc[j] = bu[:, j * LANES:(j + 1) * LANES]

    for sg in range(0, nseq, 8):
        n8 = 8
        base = sg * tc
        def a_tile(r, j, sg=sg, n8=n8):
            ps = [v // nreal for v in range(sg, sg + n8)]
            runs = [(p, ps.count(p)) for p in sorted(set(ps))]
            parts = [jnp.broadcast_to(a_ref[0, 2 * p + r:2 * p + r + 1, j * LANES:(j + 1) * LANES], (cnt, LANES))
                     for p, cnt in runs]
            return parts[0] if len(parts) == 1 else jnp.concatenate(parts, axis=0)

        a_b = [a_tile(r, j) for r in range(2) for j in range(nj)]

        def body(i8, carry, base=base, a_b=a_b):
            h = list(carry)
            for k in range(8):
                rows = pl.ds(pl.multiple_of(base + (i8 * 8 + k) * 8, 8), 8)
                new = [None] * (2 * nj)
                for j in range(nj):
                    new[j] = a_b[j] * h[j] - a_b[nj + j] * h[nj + j] + bu_sc[j, rows, :]
                    new[nj + j] = a_b[j] * h[nj + j] + a_b[nj + j] * h[j] + bu_sc[nj + j, rows, :]
                for j in range(2 * nj):
                    bu_sc[j, rows, :] = new[j]
                h = new
            return tuple(h)

        init = tuple(st_sc[sg:sg + n8, j * LANES:(j + 1) * LANES] for j in range(2 * nj))
        fin = lax.fori_loop(0, tc // 8, body, init)
        for j in range(2 * nj):
            st_sc[sg:sg + n8, j * LANES:(j + 1) * LANES] = fin[j]

    hs = jnp.concatenate([bu_sc[j] for j in range(2 * nj)], axis=1)
    yp = _dot(hs.astype(BF16), c_ref[0])
    for jb in range(nub):
        y_sc[jb] = yp[:, jb * LANES:(jb + 1) * LANES]
    for v in range(nseq):
        b, p = v % nreal, v // nreal
        sg = (v // 8) * 8
        rows = pl.ds(sg * tc + (v - sg), tc, stride=8)
        y = jnp.concatenate([y_sc[p * nkb + jb, rows, :] for jb in range(nkb)], axis=1)
        y = y + d_ref[:, p * gk:(p + 1) * gk] * u_of(v)
        gy = (0.5 * y * (1.0 + lax.erf(y * (1.0 / math.sqrt(2.0))))).astype(gy_ref.dtype)
        if len(gy_ref.shape) == 3:
            gy_ref[b, :, p * gk:(p + 1) * gk] = gy
        else:
            gy_ref[b * tc:(b + 1) * tc, :] = gy

    @pl.when(step == pl.num_programs(1) - 1)
    def _fin():
        for p in range(pair):
            sre_ref[:, p * sw:(p + 1) * sw] = st_sc[p * nreal:(p + 1) * nreal, :sw]
            sim_ref[:, p * sw:(p + 1) * sw] = st_sc[p * nreal:(p + 1) * nreal, sw:]


def _s5_scratch(nseq, tc, gk, sw):
    rows = nseq * tc
    return [pltpu.VMEM((gk // LANES, rows, LANES), F32),
            pltpu.VMEM((gk // LANES, rows, LANES), F32),
            pltpu.VMEM((2 * sw // LANES, rows, LANES), F32),
            pltpu.VMEM((gk // LANES, rows, LANES), F32),
            pltpu.VMEM((nseq, 2 * sw), F32)]


def _s5_weight_specs(kin, gk, sw, pair, idx):
    return [pl.BlockSpec((1, kin, 2 * sw), lambda *a: (idx(*a), 0, 0)),
            pl.BlockSpec((1, 2 * sw, pair * gk), lambda *a: (idx(*a), 0, 0)),
            pl.BlockSpec((1, 2 * pair, sw), lambda *a: (idx(*a), 0, 0)),
            pl.BlockSpec((1, pair * gk), lambda *a: (0, idx(*a)))]


def _s5_prompt_call(u_all, wts, nb, t, tc, pair):
    bw, cmat, amat, dsk = wts
    ngp, kin, sw2 = bw.shape
    sw = sw2 // 2
    gk = cmat.shape[2] // pair
    ntc = t // tc
    gb_of = lambda g, s: g
    u_specs = [pl.BlockSpec((tc, pair * gk), functools.partial(lambda g, s, b: (b * ntc + s, g), b=b))
               for b in range(nb)]
    kern = functools.partial(_s5_kernel, n_u=nb, tc=tc, nreal=nb, pair=pair, sw=sw, has_h0=False)
    return pl.pallas_call(
        kern,
        grid=(ngp, ntc),
        in_specs=u_specs + _s5_weight_specs(kin, gk, sw, pair, gb_of),
        out_specs=[pl.BlockSpec((nb, tc, pair * gk), lambda g, s: (0, s, g)),
                   pl.BlockSpec((nb, pair * sw), lambda g, s: (0, g)),
                   pl.BlockSpec((nb, pair * sw), lambda g, s: (0, g))],
        out_shape=[jax.ShapeDtypeStruct((nb, t, ngp * pair * gk), BF16),
                   jax.ShapeDtypeStruct((nb, ngp * pair * sw), F32),
                   jax.ShapeDtypeStruct((nb, ngp * pair * sw), F32)],
        scratch_shapes=_s5_scratch(nb * pair, tc, pair * gk, sw),
        compiler_params=_cparams(("arbitrary", "arbitrary")),
        name="s5_prompt",
    )(*([u_all] * nb), bw, cmat, amat, dsk)


def _s5_sample_call(u_all, wts, h0_re, h0_im, ns, ts, row_block0, sb):
    b2, cmat, amat, dsk = wts
    ngb, gk2, sw2 = b2.shape
    gk, sw = gk2 // 2, sw2 // 2
    gb_of = lambda g, s, z: g
    kern = functools.partial(_s5_kernel, n_u=1, tc=ts, nreal=sb, pair=1, sw=sw, has_h0=True)
    return pl.pallas_call(
        kern,
        grid=(ngb, 1, ns // sb),
        in_specs=[pl.BlockSpec((sb * ts, gk), lambda g, s, z: (row_block0 + z, g))]
        + _s5_weight_specs(gk2, gk, sw, 1, gb_of)
        + [pl.BlockSpec((sb, sw), lambda g, s, z: (z, g)), pl.BlockSpec((sb, sw), lambda g, s, z: (z, g))],
        out_specs=[pl.BlockSpec((sb * ts, gk), lambda g, s, z: (z, g)),
                   pl.BlockSpec((sb, sw), lambda g, s, z: (z, g)),
                   pl.BlockSpec((sb, sw), lambda g, s, z: (z, g))],
        out_shape=[jax.ShapeDtypeStruct((ns * ts, ngb * gk), BF16),
                   jax.ShapeDtypeStruct((ns, ngb * sw), F32),
                   jax.ShapeDtypeStruct((ns, ngb * sw), F32)],
        scratch_shapes=_s5_scratch(sb, ts, gk, sw),
        compiler_params=_cparams(("arbitrary", "arbitrary", "arbitrary")),
        name="s5_sample",
    )(u_all, b2, cmat, amat, dsk, h0_re, h0_im)


def _flash_kernel(q_ref, c_ref, kr_ref, wk_ref, wvt_ref, o_ref, k_sc, vt_sc, m_sc, l_sc, acc_sc, *, tq, nope):
    hd = pl.program_id(1)
    qi = pl.program_id(2)
    nh, vd = vt_sc.shape[:2]

    @pl.when(jnp.logical_and(hd == 0, qi == 0))
    def _keys_values():
        c = c_ref[...]
        kr = kr_ref[...]
        for hp in range(0, nh, 2):
            kk = _dot(c, wk_ref[:, hp * nope:(hp + 2) * nope])
            vv = _dot_nt(wvt_ref[hp * vd:(hp + 2) * vd, :], c)
            for e in range(2):
                k_sc[hp + e, :, :nope] = kk[:, e * nope:(e + 1) * nope].astype(k_sc.dtype)
                k_sc[hp + e, :, nope:] = kr
                vt_sc[hp + e] = vv[e * vd:(e + 1) * vd, :].astype(vt_sc.dtype)

    q = q_ref[0]
    m_sc[...] = jnp.full_like(m_sc, NEG_BIG)
    l_sc[...] = jnp.zeros_like(l_sc)
    acc_sc[...] = jnp.zeros_like(acc_sc)

    def block(ki, diagonal):
        start = pl.multiple_of(ki * tq, tq)
        s = _dot_nt(k_sc[hd, pl.ds(start, tq), :], q)
        if diagonal:
            key = lax.broadcasted_iota(jnp.int32, s.shape, 0)
            qry = lax.broadcasted_iota(jnp.int32, s.shape, 1)
            s = jnp.where(key <= qry, s, NEG_BIG)
        m_prev = m_sc[...]
        m_new = jnp.maximum(m_prev, jnp.max(s, axis=0, keepdims=True))
        alpha = jnp.exp2(m_prev - m_new)
        p = jnp.exp2(s - m_new)
        l_sc[...] = alpha * l_sc[...] + jnp.sum(p, axis=0, keepdims=True)
        acc_sc[...] = alpha * acc_sc[...] + _dot(vt_sc[hd, :, pl.ds(start, tq)], p.astype(BF16))
        m_sc[...] = m_new

    def below_diagonal(ki, carry):
        block(ki, False)
        return carry

    lax.fori_loop(0, qi, below_diagonal, 0)
    block(qi, True)
    o_ref[...] = (acc_sc[...] / l_sc[...]).T.astype(o_ref.dtype)


def _flash_outer_kernel(*refs, tq, nb, nope):
    o_ref = refs[5]

    @pl.when(pl.program_id(0) < nb)
    def _rows():
        _flash_kernel(*refs, tq=tq, nope=nope)

    @pl.when(pl.program_id(0) == nb)
    def _tail():
        o_ref[...] = jnp.zeros_like(o_ref)


def _flash_call(q_hm, ckv_b, kr_b, w_uk2, w_uvt2, nb, t, m_total, tq):
    nh, _, qd = q_hm.shape
    lora = w_uk2.shape[0]
    nope = w_uk2.shape[1] // nh
    vd = w_uvt2.shape[0] // nh
    rw = kr_b.shape[1]
    assert nope + rw == qd
    nq = t // tq
    last = m_total // tq - 1
    assert nb * nq + nq > last
    bc = lambda b: jnp.minimum(b, nb - 1)
    hc = lambda b, h: jnp.where(b < nb, h, nh - 1)
    return pl.pallas_call(
        functools.partial(_flash_outer_kernel, tq=tq, nb=nb, nope=nope),
        grid=(nb + 1, nh, nq),
        in_specs=[pl.BlockSpec((1, tq, qd), lambda b, h, i: (hc(b, h), bc(b) * nq + jnp.where(b < nb, i, nq - 1), 0)),
                  pl.BlockSpec((t, lora), lambda b, h, i: (bc(b), 0)),
                  pl.BlockSpec((t, rw), lambda b, h, i: (bc(b), 0)),
                  pl.BlockSpec((lora, nh * nope), lambda b, h, i: (0, 0)),
                  pl.BlockSpec((nh * vd, lora), lambda b, h, i: (0, 0))],
        out_specs=pl.BlockSpec((tq, vd), lambda b, h, i: (jnp.minimum(b * nq + i, last), h)),
        out_shape=jax.ShapeDtypeStruct((m_total, nh * vd), BF16),
        scratch_shapes=[pltpu.VMEM((nh, t, qd), BF16), pltpu.VMEM((nh, vd, t), BF16),
                        pltpu.VMEM((1, tq), F32), pltpu.VMEM((1, tq), F32), pltpu.VMEM((vd, tq), F32)],
        compiler_params=pltpu.CompilerParams(dimension_semantics=("arbitrary", "arbitrary", "arbitrary"),
                                             vmem_limit_bytes=FLASH_VMEM_LIMIT),
        name="flash",
    )(q_hm, ckv_b, kr_b, w_uk2, w_uvt2)


def _qlat_kernel(q_ref, w_ref, o_ref):
    o_ref[0] = _dot(q_ref[0], w_ref[0]).astype(o_ref.dtype)


def _qlat_call(q_hm, w_ukt, ms, row_block0):
    nh, nope, lora = w_ukt.shape
    return pl.pallas_call(
        _qlat_kernel,
        grid=(nh,),
        in_specs=[pl.BlockSpec((1, ms, nope), lambda h: (h, row_block0, 0)),
                  pl.BlockSpec((1, nope, lora), lambda h: (h, 0, 0))],
        out_specs=pl.BlockSpec((1, ms, lora), lambda h: (h, 0, 0)),
        out_shape=jax.ShapeDtypeStruct((nh, ms, lora), BF16),
        compiler_params=_cparams(("arbitrary",)),
        name="qlat",
    )(q_hm, w_ukt)


def _decode_kernel(pt_ref, ql_ref, qr_ref, cn_ref, kn_ref, ck_hbm, kr_hbm, o_ref,
                   ck_buf, kr_buf, sem, qlb_sc, qrb_sc, kb_sc, rb_sc, m_sc, l_sc, acc_sc, *, npg, sq, ts, rope, n_pages):
    s_blk = pl.program_id(0)
    j = pl.program_id(1)
    nsteps = pl.num_programs(1)
    g = s_blk * nsteps + j
    total = pl.num_programs(0) * nsteps
    nslot = ck_buf.shape[0]
    ahead = nslot - 1
    slot = lax.rem(g, nslot)
    nh, _, lora = ql_ref.shape
    rows = nh * ts
    pg = ck_buf.shape[2]

    def page_copies(step):
        sb = lax.div(step, nsteps)
        jj = lax.rem(step, nsteps)
        sl = lax.rem(step, nslot)
        out = []
        for a in range(sq):
            for i in range(npg):
                page = pt_ref[(sb * sq + a) * n_pages + jj * npg + i]
                out.append(pltpu.make_async_copy(ck_hbm.at[page], ck_buf.at[sl, a * npg + i], sem.at[0, sl]))
                out.append(pltpu.make_async_copy(kr_hbm.at[page], kr_buf.at[sl, a * npg + i], sem.at[1, sl]))
        return out

    for k in range(ahead):
        @pl.when(jnp.logical_and(g == 0, k < total))
        def _prime(k=k):
            for c in page_copies(k):
                c.start()

    @pl.when(g + ahead < total)
    def _start_ahead():
        for c in page_copies(g + ahead):
            c.start()

    for c in page_copies(g):
        c.wait()
    ck_refs = [ck_buf.at[slot, i] for i in range(sq * npg)]
    kr_refs = [kr_buf.at[slot, i] for i in range(sq * npg)]

    @pl.when(j == 0)
    def _init():
        ql = ql_ref[...].astype(F32)
        for a in range(sq):
            qlb_sc[a] = ql[:, a * ts:(a + 1) * ts, :].reshape(rows, lora).astype(BF16)
            qrb_sc[a] = qr_ref[:, a * ts:(a + 1) * ts, :].reshape(rows, rope).astype(BF16)
        m_sc[...] = jnp.full_like(m_sc, NEG_BIG)
        l_sc[...] = jnp.zeros_like(l_sc)
        acc_sc[...] = jnp.zeros_like(acc_sc)

    def update(a, s, kb):
        m_prev = m_sc[a]
        m_new = jnp.maximum(m_prev, jnp.max(s, axis=-1, keepdims=True))
        alpha = jnp.exp2(m_prev - m_new)
        p = jnp.exp2(s - m_new)
        l_sc[a] = alpha * l_sc[a] + jnp.sum(p, axis=-1, keepdims=True)
        acc_sc[a] = alpha * acc_sc[a] + _dot(p.astype(BF16), kb)
        m_sc[a] = m_new

    for a in range(sq):
        for i in range(npg):
            kb_sc[a, i * pg:(i + 1) * pg, :] = ck_refs[a * npg + i][...].astype(BF16)
            rb_sc[a, :, i * pg:(i + 1) * pg] = kr_refs[a * npg + i][...].astype(BF16)
    for a in range(sq):
        kb = kb_sc[a]
        update(a, _dot_nt(qlb_sc[a], kb) + _dot(qrb_sc[a], rb_sc[a]), kb)

    @pl.when(j == pl.num_programs(1) - 1)
    def _fin():
        outs = []
        for a in range(sq):
            kn = jnp.concatenate([cn_ref[a * ts:(a + 1) * ts, :], jnp.zeros((pg - ts, lora), F32)], axis=0).astype(BF16)
            rn = jnp.concatenate([kn_ref[a * ts:(a + 1) * ts, :], jnp.zeros((pg - ts, rope), F32)], axis=0).astype(BF16)
            s = _dot_nt(qlb_sc[a], kn) + _dot_nt(qrb_sc[a], rn)
            tq = lax.broadcasted_iota(jnp.int32, s.shape, 0) % ts
            tk = lax.broadcasted_iota(jnp.int32, s.shape, 1)
            update(a, jnp.where(tk <= tq, s, NEG_BIG), kn)
            outs.append((acc_sc[a] / l_sc[a]).reshape(nh, ts, lora))
        o_ref[...] = jnp.concatenate(outs, axis=1).astype(o_ref.dtype)


def _decode_call(page_table, q_lat, q_rope, ckv_all, kr_all, cache_ckv, cache_krope_t, ns, ts, row_block0, npg, sq):
    nh, ms, lora = q_lat.shape
    rope = q_rope.shape[2]
    n_pages = page_table.shape[1]
    pg = cache_ckv.shape[1]
    nsteps = n_pages // npg
    rows = nh * ts

    n = sq * npg
    grid_spec = pltpu.PrefetchScalarGridSpec(
        num_scalar_prefetch=1,
        grid=(ns // sq, nsteps),
        in_specs=[pl.BlockSpec((nh, sq * ts, lora), lambda s, j, pt: (0, s, 0)),
                  pl.BlockSpec((nh, sq * ts, rope), lambda s, j, pt: (0, s, 0)),
                  pl.BlockSpec((sq * ts, lora), lambda s, j, pt: (row_block0 + s, 0)),
                  pl.BlockSpec((sq * ts, rope), lambda s, j, pt: (row_block0 + s, 0)),
                  pl.BlockSpec(memory_space=pl.ANY),
                  pl.BlockSpec(memory_space=pl.ANY)],
        out_specs=pl.BlockSpec((nh, sq * ts, lora), lambda s, j, pt: (0, s, 0)),
        scratch_shapes=[pltpu.VMEM((DECODE_SLOTS, n, pg, lora), F32), pltpu.VMEM((DECODE_SLOTS, n, rope, pg), F32),
                        pltpu.SemaphoreType.DMA((2, DECODE_SLOTS)),
                        pltpu.VMEM((sq, rows, lora), BF16), pltpu.VMEM((sq, rows, rope), BF16),
                        pltpu.VMEM((sq, npg * pg, lora), BF16), pltpu.VMEM((sq, rope, npg * pg), BF16),
                        pltpu.VMEM((sq, rows, 1), F32), pltpu.VMEM((sq, rows, 1), F32),
                        pltpu.VMEM((sq, rows, lora), F32)],
    )
    return pl.pallas_call(
        functools.partial(_decode_kernel, npg=npg, sq=sq, ts=ts, rope=rope, n_pages=n_pages),
        grid_spec=grid_spec,
        out_shape=jax.ShapeDtypeStruct((nh, ms, lora), BF16),
        compiler_params=_cparams(("arbitrary", "arbitrary")),
        name="decode",
    )(page_table.reshape(-1), q_lat, q_rope, ckv_all, kr_all, cache_ckv, cache_krope_t)


def _vup_kernel(o_ref, w_ref, att_any, out_ref):
    del att_any
    out_ref[...] = _dot(o_ref[0].astype(BF16), w_ref[0]).astype(out_ref.dtype)


def _vup_call(o_lat, w_uv3, att_all, row_block0):
    nh, ms, lora = o_lat.shape
    vd = w_uv3.shape[2]
    return pl.pallas_call(
        _vup_kernel,
        grid=(nh,),
        in_specs=[pl.BlockSpec((1, ms, lora), lambda h: (h, 0, 0)),
                  pl.BlockSpec((1, lora, vd), lambda h: (h, 0, 0)),
                  pl.BlockSpec(memory_space=pl.ANY)],
        out_specs=pl.BlockSpec((ms, vd), lambda h: (row_block0, h)),
        out_shape=jax.ShapeDtypeStruct(att_all.shape, att_all.dtype),
        input_output_aliases={2: 0},
        compiler_params=_cparams(("arbitrary",)),
        name="vup",
    )(o_lat, w_uv3, att_all)


def _glu_kernel(gp_ref, gs_ref, wa_ref, wb_ref, z_ref, o_ref, *, np_tiles):
    g = jnp.where(pl.program_id(0) < np_tiles, gp_ref[...], gs_ref[...])
    ga = _dot(g, wa_ref[...])
    gb = _dot(g, wb_ref[...])
    o_ref[...] = (ga * _sigmoid(gb) * z_ref[...].astype(F32)).astype(o_ref.dtype)


def _glu_call(gy_p, gy_s, w_glu, zs_act, tm, tn):
    mp, k = gy_p.shape
    ms = gy_s.shape[0]
    m = mp + ms
    n = w_glu.shape[1] // 2
    nn = n // tn
    npt, nst = mp // tm, ms // tm
    return pl.pallas_call(
        functools.partial(_glu_kernel, np_tiles=npt),
        grid=(m // tm, nn),
        in_specs=[pl.BlockSpec((tm, k), lambda i, j: (jnp.minimum(i, npt - 1), 0)),
                  pl.BlockSpec((tm, k), lambda i, j: (jnp.clip(i - npt, 0, nst - 1), 0)),
                  pl.BlockSpec((k, tn), lambda i, j: (0, j)),
                  pl.BlockSpec((k, tn), lambda i, j: (0, nn + j)),
                  pl.BlockSpec((tm, tn), lambda i, j: (i, j))],
        out_specs=pl.BlockSpec((tm, tn), lambda i, j: (i, j)),
        out_shape=jax.ShapeDtypeStruct((m, n), BF16),
        compiler_params=_cparams(("arbitrary", "arbitrary")),
        name="glu",
    )(gy_p, gy_s, w_glu, w_glu, zs_act)


def _merge_kernel(ys_ref, att_ref, za_ref, wbs_ref, wo_ref, gs_ref, ga_ref, o_ref, v_sc):
    @pl.when(pl.program_id(1) == 0)
    def _gate():
        v_sc[...] = (att_ref[...].astype(F32) * za_ref[...].astype(F32)).astype(BF16)

    p_s = _dot(ys_ref[...], wbs_ref[...])
    p_a = _dot(v_sc[...], wo_ref[...])
    o_ref[...] = (gs_ref[...].astype(F32) * p_s + ga_ref[...].astype(F32) * p_a).astype(o_ref.dtype)


def _merge_call(ys2, att, za_act, w_bs, w_o, gates, tm, tn):
    m, ks = ys2.shape
    ka = att.shape[1]
    n = w_bs.shape[1]
    nn = n // tn
    return pl.pallas_call(
        _merge_kernel,
        grid=(m // tm, nn),
        in_specs=[pl.BlockSpec((tm, ks), lambda i, j: (i, 0)),
                  pl.BlockSpec((tm, ka), lambda i, j: (i, 0)),
                  pl.BlockSpec((tm, ka), lambda i, j: (i, 0)),
                  pl.BlockSpec((ks, tn), lambda i, j: (0, j)),
                  pl.BlockSpec((ka, tn), lambda i, j: (0, j)),
                  pl.BlockSpec((tm, tn), lambda i, j: (i, j)),
                  pl.BlockSpec((tm, tn), lambda i, j: (i, nn + j))],
        out_specs=pl.BlockSpec((tm, tn), lambda i, j: (i, j)),
        out_shape=jax.ShapeDtypeStruct((m, n), BF16),
        scratch_shapes=[pltpu.VMEM((tm, ka), BF16)],
        compiler_params=_cparams(("arbitrary", "arbitrary")),
        name="merge",
    )(ys2, att, za_act, w_bs, w_o, gates, gates)


def _final_kernel(mg_ref, w_ref, x_ref, gate_ref, g_ref, b_ref, o_ref, *, alpha):
    out = _dot(mg_ref[...], w_ref[...])
    x = x_ref[...]
    y = alpha * x + gate_ref[...] * out.reshape(x.shape)
    mu = jnp.mean(y, axis=-1, keepdims=True)
    yc = y - mu
    var = jnp.mean(yc * yc, axis=-1, keepdims=True)
    o_ref[...] = yc * lax.rsqrt(var + LN_EPS) * g_ref[...] + b_ref[...]


def _final_call(merged, w_out, x, mod3, ln_g, ln_b, alpha, x_blk, gate_blk, gate_map, merged_map, grid):
    d = w_out.shape[0]
    rows = x_blk[0] * x_blk[1]
    nd = len(grid)
    x_map = (lambda i, j: (i, j, 0)) if nd == 2 else (lambda i: (i, 0, 0))
    const2 = (lambda i, j: (0, 0)) if nd == 2 else (lambda i: (0, 0))
    const3 = (lambda i, j: (0, 0, 0)) if nd == 2 else (lambda i: (0, 0, 0))
    return pl.pallas_call(
        functools.partial(_final_kernel, alpha=alpha),
        grid=grid,
        in_specs=[pl.BlockSpec((rows, d), merged_map),
                  pl.BlockSpec((d, d), const2),
                  pl.BlockSpec(x_blk, x_map),
                  pl.BlockSpec(gate_blk, gate_map),
                  pl.BlockSpec((1, 1, d), const3),
                  pl.BlockSpec((1, 1, d), const3)],
        out_specs=pl.BlockSpec(x_blk, x_map),
        out_shape=jax.ShapeDtypeStruct(x.shape, F32),
        compiler_params=_cparams(("arbitrary",) * nd),
        name="final",
    )(merged, w_out, x, mod3, ln_g.reshape(1, 1, d), ln_b.reshape(1, 1, d))


def _rope_tables(pos, rope):
    freqs = ROPE_BASE ** (-jnp.arange(0, rope, 2, dtype=F32) / rope)
    ang = pos.astype(F32)[:, None] * freqs[None, :]
    cos, sin = jnp.cos(ang), jnp.sin(ang)
    return jnp.concatenate([cos] * 4 + [sin] * 4, axis=-1)


def _rot_rows(wt, rope):
    half = rope // 2
    return jnp.concatenate([-wt[..., half:, :], wt[..., :half, :]], axis=-2)


def _block_diag(x, gblk):
    g, r, c = x.shape
    x4 = x.reshape(g // gblk, gblk, r, c)
    eye = jnp.eye(gblk, dtype=x.dtype)
    return jnp.einsum("bgrc,gh->bgrhc", x4, eye).reshape(g // gblk, gblk * r, gblk * c)


def kernel(x_prompt, x_sample, c_prompt, c_sample, cache_ckv, cache_krope, state_ssm_re, state_ssm_im, page_table, w_ada, b_ada, w_in, g_kv, w_uk, w_uv, w_o, lam_re, lam_im, log_dt, b_re, b_im, c_re, c_im, d_skip, w_glu, w_bs, w_out, ln_g, ln_b):
    nb, t, d = x_prompt.shape
    ns, ts, _ = x_sample.shape
    depth = w_in.shape[0]
    assert depth == 1, "single-layer step"
    lora, nh, nope = w_uk.shape[1:]
    vd = w_uv.shape[3]
    rope = cache_krope.shape[3]
    d_ssm = d_skip.shape[1]
    ngrp, nstate, gch = b_re.shape[1:]
    d_attn = nh * vd
    pg = cache_ckv.shape[2]
    past = page_table.shape[1] * pg
    mp, ms = nb * t, ns * ts
    m = mp + ms
    scale = float((nope + rope) ** -0.5) * LOG2E
    alpha = float((2 * depth) ** 0.25)

    tm = math.gcd(math.gcd(mp, ms), 1024)
    tmh = min(tm, 512)
    tn = 512
    assert rope * 2 == 128 and nope == 128 and vd == 128

    wt = jnp.swapaxes(w_in, 1, 2)[0].astype(BF16)
    o_u, o_zs, o_q = 0, d_ssm, 2 * d_ssm
    o_ckv = o_q + nh * (nope + rope)
    o_kr = o_ckv + lora
    o_za = o_kr + rope
    o_gs = o_za + d_attn
    o_ga = o_gs + d
    w_u = wt[o_u:o_zs].astype(BF16)
    w_za = wt[o_za:o_gs].astype(BF16)
    w_zs = wt[o_zs:o_q].astype(BF16)
    w_g = wt[o_gs:].astype(BF16)
    wq = wt[o_q:o_ckv].reshape(nh, nope + rope, d)
    qhb = 4
    wq4 = wq.reshape(nh // qhb, qhb, nope + rope, d)
    w_q4 = jnp.concatenate([wq4[:, :, :nope, :].reshape(nh // qhb, qhb * nope, d),
                            wq4[:, :, nope:, :].reshape(nh // qhb, qhb * rope, d)], axis=1).astype(BF16)
    wkr = wt[o_kr:o_za]
    w_ck = jnp.concatenate([wt[o_ckv:o_kr], wkr, _rot_rows(wkr, rope)], axis=0).astype(BF16)
    w_uk2 = w_uk[0].reshape(lora, nh * nope).astype(BF16)
    w_uvt2 = w_uv[0].transpose(1, 2, 0).reshape(nh * vd, lora).astype(BF16)
    w_ukt = w_uk[0].transpose(1, 2, 0).astype(BF16)
    w_uv3 = w_uv[0].transpose(1, 0, 2).astype(BF16)
    w_glu_b = w_glu[0].astype(BF16)
    w_bs_b = w_bs[0].astype(BF16)
    w_o_b = w_o[0].astype(BF16)
    w_out_b = w_out[0].astype(BF16)

    cs4 = jnp.concatenate([jnp.tile(_rope_tables(jnp.arange(t), rope), (nb, 1)),
                           jnp.tile(_rope_tables(past + jnp.arange(ts), rope), (ns, 1))], axis=0)
    cs = jnp.concatenate([cs4[:, :rope], cs4[:, 2 * rope:3 * rope]], axis=1)

    c_all = jnp.concatenate([c_sample, c_prompt], axis=0)
    pad = (-c_all.shape[0]) % 8
    c_all = jnp.concatenate([c_all, jnp.zeros((pad, d), F32)], axis=0)
    mod = _mod_call(c_all, w_ada[0], b_ada[0], tn)
    mod3 = mod.reshape(mod.shape[0], 1, 3 * d)

    sb = max(1, min(ns, tmh // ts))
    h_all = _hmod_prompt(x_prompt, mod3, ns, m, tmh)
    h_all = _hmod_sample(x_sample, mod3, h_all, mp // (sb * ts), sb)

    u_all = _proj_call(h_all, w_u, None, F32, tm, 2 * tn, "proj_u")
    za_act = _proj_call(h_all, w_za, "silu", BF16, tm, 2 * tn, "proj_za")
    zs_act = _proj_call(h_all, w_zs, "silu", BF16, tm, 2 * tn, "proj_zs")
    gates = _proj_call(h_all, w_g, "sigmoid", BF16, tm, 2 * tn, "proj_g")
    q_hm = _qproj_call(h_all, w_q4, cs4, tm, nh, nope, rope, scale, qhb)
    ckv_all, kr_all, ckv_b, kr_b = _ckv_call(h_all, w_ck, g_kv[0], cs, tmh, lora, rope)

    gblk = SSM_GROUP_BLOCK
    a_re, a_im, bb_re, bb_im = _s5_prep_call(lam_re[0], lam_im[0], log_dt[0],
                                             b_re[0].transpose(0, 2, 1), b_im[0].transpose(0, 2, 1))
    bcat = jnp.concatenate([_block_diag(bb_re, gblk), _block_diag(bb_im, gblk)], axis=-1)
    b2 = jnp.concatenate([bcat, bcat], axis=1).astype(BF16)
    ccat = jnp.concatenate([_block_diag(c_re[0].transpose(0, 2, 1), gblk),
                            _block_diag(-c_im[0].transpose(0, 2, 1), gblk)], axis=1).astype(BF16)
    ngb = ngrp // gblk
    sw = gblk * nstate
    amat = jnp.stack([a_re.reshape(ngb, sw), a_im.reshape(ngb, sw)], axis=1)
    s5w = (b2, ccat, amat, d_skip)
    tc = min(t, S5_TIME_CHUNK)
    pair = 2 if (2 * nb <= 8 and ngb % 2 == 0) else 1
    gkb = gblk * gch
    s5w_p = s5w if pair == 1 else (
        bcat.astype(BF16).reshape(ngb // 2, 2 * gkb, 2 * sw),
        ccat.reshape(ngb // 2, 2, 2 * sw, gkb).transpose(0, 2, 1, 3).reshape(ngb // 2, 2 * sw, 2 * gkb),
        amat.reshape(ngb // 2, 4, sw), d_skip)
    gy_p, sre_p, sim_p = _s5_prompt_call(u_all, s5w_p, nb, t, tc, pair)
    ssb = min(ns, 64)
    gy_s, sre_s, sim_s = _s5_sample_call(u_all, s5w, state_ssm_re[0].reshape(ns, ngrp * nstate),
                                         state_ssm_im[0].reshape(ns, ngrp * nstate), ns, ts, mp // (ssb * ts), ssb)

    tq = min(t, FLASH_BLOCK)
    att = _flash_call(q_hm, ckv_b, kr_b, w_uk2, w_uvt2, nb, t, m, tq)
    q_lat = _qlat_call(q_hm, w_ukt, ms, mp // ms)
    q_rope_s = q_hm[:, mp:, nope:nope + rope].astype(F32)
    sq = DECODE_SEQS_PER_STEP
    krope_t = jnp.swapaxes(cache_krope, 2, 3).reshape(cache_krope.shape[1], rope, pg)
    o_lat = _decode_call(page_table, q_lat, q_rope_s, ckv_all, kr_all, cache_ckv.reshape(cache_ckv.shape[1:]), krope_t,
                         ns, ts, mp // (sq * ts), PAGES_PER_STEP, sq)
    att = _vup_call(o_lat, w_uv3, att, mp // ms)

    ys2 = _glu_call(gy_p.reshape(mp, d_ssm), gy_s, w_glu_b, zs_act, tm, tn)
    merged = _merge_call(ys2, att, za_act, w_bs_b, w_o_b, gates, tm, tn)
    ntp = t // tmh
    y_p = _final_call(merged, w_out_b, x_prompt, mod3, ln_g[0], ln_b[0], alpha,
                      (1, tmh, d), (1, 1, d), lambda i, j: (ns + i, 0, 2), lambda i, j: (i * ntp + j, 0), (nb, ntp))
    y_s = _final_call(merged, w_out_b, x_sample, mod3, ln_g[0], ln_b[0], alpha,
                      (sb, ts, d), (sb, 1, d), lambda i: (i, 0, 2), lambda i: (mp // (sb * ts) + i, 0), (ns // sb,))

    st = lambda a, n: a.reshape(1, n, ngrp, nstate)
    return (y_p, y_s,
            ckv_all[:mp].reshape(1, nb, t, lora), kr_all[:mp].reshape(1, nb, t, rope), st(sre_p, nb), st(sim_p, nb),
            ckv_all[mp:].reshape(1, ns, ts, lora), kr_all[mp:].reshape(1, ns, ts, rope), st(sre_s, ns), st(sim_s, ns))
```
